```python
import math
import jax, jax.numpy as jnp
from jax import lax
import numpy as np

D_MODEL = 1024
BATCH = 8
SEQ = 4096
DEPTH = 4

HEAD_DIM = 64
D_MIX = D_MODEL
RWKV_HEADS = (3 * D_MIX) // (8 * HEAD_DIM)
D_RWKV = RWKV_HEADS * HEAD_DIM
ATTN_HEADS = (3 * D_MIX) // (8 * HEAD_DIM)
D_ATTN = ATTN_HEADS * HEAD_DIM
D_CONV = D_MIX - D_RWKV - D_ATTN
DECAY_LORA = 32
ICLR_LORA = 32
GATE_LORA = 64
CONV_WIDTH = 31
DILATED_PATTERNS = ((128, 1), (512, 4), (2048, 16))
ATTN_BLOCK = 128
D_FF = 2816
NORM_EPS = 1e-6
RWKV_GN_EPS = 64e-5
CONV_LN_EPS = 1e-5
D_SHIFT = 3 * D_RWKV + DECAY_LORA + ICLR_LORA + GATE_LORA
D_IN = D_SHIFT + 3 * D_ATTN + 2 * D_CONV

kernel_name = 'hybrid_rwkv7_dilattn_conformer_macaron'


def _alibi_slopes(n):
    def pow2(m):
        start = 2.0 ** (-8.0 / m)
        return [start ** (i + 1) for i in range(m)]
    if math.log2(n).is_integer():
        return pow2(n)
    c = 2 ** int(math.floor(math.log2(n)))
    return pow2(c) + pow2(2 * c)[0::2][: n - c]


def _rms_norm(x, g):
    xf = x.astype(jnp.float32)
    y = xf * lax.rsqrt(jnp.mean(xf * xf, axis=-1, keepdims=True) + NORM_EPS)
    return y.astype(x.dtype) * g


def _head_rms(t, g):
    return t * lax.rsqrt(jnp.mean(t * t, axis=-1, keepdims=True) + NORM_EPS) * g.astype(jnp.float32)


def _swiglu(x, w_gate, w_up, w_down):
    return (jax.nn.silu(x @ w_gate) * (x @ w_up)) @ w_down


def _token_shift(p, mu):
    prev = jnp.pad(p, ((0, 0), (1, 0), (0, 0)))[:, :-1]
    return p + (prev - p) * mu.astype(p.dtype)


def _rwkv7_scan(r, w, k, v, kk, a):
    B, S, H, N = r.shape
    xs = tuple(jnp.moveaxis(t, 1, 0) for t in (r, w, k, v, -kk, kk * a))

    def step(state, inp):
        r_t, w_t, k_t, v_t, a_t, b_t = inp
        sa = jnp.einsum('bhij,bhj->bhi', state, a_t)
        state = (state * w_t[:, :, None, :]
                 + sa[..., None] * b_t[:, :, None, :]
                 + v_t[..., None] * k_t[:, :, None, :])
        y_t = jnp.einsum('bhij,bhj->bhi', state, r_t)
        return state, y_t

    state0 = jnp.zeros((B, H, N, N), jnp.float32)
    _, y = lax.scan(step, state0, xs)
    return jnp.moveaxis(y, 0, 1)


def _rwkv7_mixer(p, w0, w2, a0, a2, g2, k_k, k_a, r_k, ln_w, ln_b):
    B, S, _ = p.shape
    o = 0
    r = p[..., o:o + D_RWKV]; o += D_RWKV
    k = p[..., o:o + D_RWKV]; o += D_RWKV
    v = p[..., o:o + D_RWKV]; o += D_RWKV
    xw = p[..., o:o + DECAY_LORA]; o += DECAY_LORA
    xa = p[..., o:o + ICLR_LORA]; o += ICLR_LORA
    xg = p[..., o:o + GATE_LORA]
    f32 = jnp.float32
    w_log = -jax.nn.softplus(-(w0.astype(f32) + jnp.tanh(xw) @ w2.astype(f32))) - 0.5
    decay = jnp.exp(-jnp.exp(w_log))
    a = jax.nn.sigmoid(a0.astype(f32) + xa @ a2.astype(f32))
    g = jax.nn.sigmoid(xg) @ g2.astype(f32)
    heads = lambda t: t.reshape(B, S, RWKV_HEADS, HEAD_DIM)
    kk = heads(k * k_k.astype(f32))
    kk = kk * lax.rsqrt(jnp.sum(kk * kk, axis=-1, keepdims=True) + 1e-12)
    k = k * (1.0 + (a - 1.0) * k_a.astype(f32))
    rh, kh, vh = heads(r), heads(k), heads(v)
    y = _rwkv7_scan(rh, heads(decay), kh, vh, kk, heads(a))
    mu = jnp.mean(y, axis=-1, keepdims=True)
    var = jnp.mean(jnp.square(y - mu), axis=-1, keepdims=True)
    y = ((y - mu) * lax.rsqrt(var + RWKV_GN_EPS)).reshape(B, S, D_RWKV)
    y = y * ln_w.astype(f32) + ln_b.astype(f32)
    bonus = jnp.sum(rh * kh * r_k.astype(f32), axis=-1, keepdims=True) * vh
    return (y + bonus.reshape(B, S, D_RWKV)) * g


def _dilated_attention(q, k, v, slopes, window, dilation):
    B, S, H, N = q.shape
    n_keys = window // dilation
    L = S // dilation
    nb = -(-L // ATTN_BLOCK)
    Lp = nb * ATTN_BLOCK

    def blocks(t):
        t = t.reshape(B, L, dilation, H, N)
        t = jnp.pad(t, ((0, 0), (0, Lp - L), (0, 0), (0, 0), (0, 0)))
        return t.reshape(B, nb, ATTN_BLOCK, dilation, H, N)

    def band(t):
        prev = jnp.pad(t, ((0, 0), (1, 0), (0, 0), (0, 0), (0, 0), (0, 0)))[:, :-1]
        return jnp.concatenate([prev, t], axis=2)

    qb = blocks(q)
    kb = band(blocks(k))
    vb = band(blocks(v))
    s = jnp.einsum('bnqchd,bnkchd->bnqchk', qb, kb)
    qi = jnp.arange(ATTN_BLOCK)
    ki = jnp.arange(2 * ATTN_BLOCK)
    dist = qi[:, None] + ATTN_BLOCK - ki[None, :]
    key_idx = jnp.arange(nb)[:, None] * ATTN_BLOCK + ki[None, :] - ATTN_BLOCK
    valid = ((dist >= 0) & (dist <= n_keys))[None] & (key_idx >= 0)[:, None, :]
    bias = -slopes[:, None] * (dist * dilation).astype(jnp.float32)[:, None, None, :]
    s = jnp.where(valid[None, :, :, None, None, :], s + bias, -jnp.inf)
    lse = jax.nn.logsumexp(s, axis=-1)
    p = jnp.exp(s - lse[..., None])
    o = jnp.einsum('bnqchk,bnkchd->bnqchd', p, vb)
    o = o.reshape(B, Lp, dilation, H, N)[:, :L].reshape(B, S, H, N)
    lse = lse.reshape(B, Lp, dilation, H)[:, :L].reshape(B, S, H)
    return o, lse


def _attention_mixer(qkv, q_norm, k_norm):
    B, S, _ = qkv.shape
    q = qkv[..., :D_ATTN].reshape(B, S, ATTN_HEADS, HEAD_DIM)
    k = qkv[..., D_ATTN:2 * D_ATTN].reshape(B, S, ATTN_HEADS, HEAD_DIM)
    v = qkv[..., 2 * D_ATTN:].reshape(B, S, ATTN_HEADS, HEAD_DIM)
    q = _head_rms(q, q_norm) * (HEAD_DIM ** -0.5)
    k = _head_rms(k, k_norm)
    slopes = jnp.asarray(_alibi_slopes(ATTN_HEADS), jnp.float32)
    res = [_dilated_attention(q, k, v, slopes, w, d) for (w, d) in DILATED_PATTERNS]
    outs = jnp.stack([r_[0] for r_ in res], axis=0)
    lses = jnp.stack([r_[1] for r_ in res], axis=0)
    alpha = jax.nn.softmax(lses, axis=0)
    out = jnp.sum(alpha[..., None] * outs, axis=0)
    return out.reshape(B, S, D_ATTN)


def _conv_mixer(u, dw_w, dw_b, ln_w, ln_b):
    f32 = jnp.float32
    z = u[..., :D_CONV] * jax.nn.sigmoid(u[..., D_CONV:])
    z = lax.conv_general_dilated(z, dw_w.astype(f32)[:, None, :], window_strides=(1,),
                                 padding=[(CONV_WIDTH - 1, 0)],
                                 dimension_numbers=('NWC', 'WIO', 'NWC'),
                                 feature_group_count=D_CONV) + dw_b.astype(f32)
    mu = jnp.mean(z, axis=-1, keepdims=True)
    var = jnp.mean(jnp.square(z - mu), axis=-1, keepdims=True)
    z = (z - mu) * lax.rsqrt(var + CONV_LN_EPS) * ln_w.astype(f32) + ln_b.astype(f32)
    return jax.nn.silu(z)


def setup_inputs(seed: int = 0) -> dict:
    key = jax.random.key(seed)
    ks = iter(jax.random.split(key, 40))
    f32 = jnp.float32
    L = DEPTH

    def normal(shape, scale):
        return jax.random.normal(next(ks), shape, f32) * scale

    def gain(shape):
        return 1.0 + 0.05 * jax.random.normal(next(ks), shape, f32)

    def unif(shape, lo, hi):
        return jax.random.uniform(next(ks), shape, f32, lo, hi)

    return {
        'x': normal((BATCH, SEQ, D_MODEL), 1.0),
        'norm_ffn1': gain((L, D_MODEL)),
        'ffn1_w_gate': normal((L, D_MODEL, D_FF), D_MODEL ** -0.5),
        'ffn1_w_up': normal((L, D_MODEL, D_FF), D_MODEL ** -0.5),
        'ffn1_w_down': normal((L, D_FF, D_MODEL), D_FF ** -0.5),
        'norm_mix': gain((L, D_MODEL)),
        'w_in': normal((L, D_MODEL, D_IN), D_MODEL ** -0.5),
        'shift_mu': unif((L, D_SHIFT), 0.0, 1.0),
        'rwkv_w0': unif((L, D_RWKV), -6.0, 1.0),
        'rwkv_w2': normal((L, DECAY_LORA, D_RWKV), 0.1),
        'rwkv_a0': normal((L, D_RWKV), 0.5),
        'rwkv_a2': normal((L, ICLR_LORA, D_RWKV), 0.1),
        'rwkv_g2': normal((L, GATE_LORA, D_RWKV), GATE_LORA ** -0.5),
        'rwkv_k_k': 0.85 + normal((L, D_RWKV), 0.05),
        'rwkv_k_a': gain((L, D_RWKV)),
        'rwkv_r_k': 0.5 + normal((L, RWKV_HEADS, HEAD_DIM), 0.1),
        'rwkv_ln_w': gain((L, D_RWKV)),
        'rwkv_ln_b': normal((L, D_RWKV), 0.02),
        'attn_q_norm': gain((L, HEAD_DIM)),
        'attn_k_norm': gain((L, HEAD_DIM)),
        'conv_dw_w': normal((L, CONV_WIDTH, D_CONV), CONV_WIDTH ** -0.5),
        'conv_dw_b': normal((L, D_CONV), 0.02),
        'conv_ln_w': gain((L, D_CONV)),
        'conv_ln_b': normal((L, D_CONV), 0.02),
        'w_out': normal((L, D_MIX, D_MODEL), D_MIX ** -0.5),
        'norm_ffn2': gain((L, D_MODEL)),
        'ffn2_w_gate': normal((L, D_MODEL, D_FF), D_MODEL ** -0.5),
        'ffn2_w_up': normal((L, D_MODEL, D_FF), D_MODEL ** -0.5),
        'ffn2_w_down': normal((L, D_FF, D_MODEL), D_FF ** -0.5),
    }


def reference(x, norm_ffn1, ffn1_w_gate, ffn1_w_up, ffn1_w_down, norm_mix, w_in, shift_mu,
              rwkv_w0, rwkv_w2, rwkv_a0, rwkv_a2, rwkv_g2, rwkv_k_k, rwkv_k_a, rwkv_r_k,
              rwkv_ln_w, rwkv_ln_b, attn_q_norm, attn_k_norm, conv_dw_w, conv_dw_b,
              conv_ln_w, conv_ln_b, w_out, norm_ffn2, ffn2_w_gate, ffn2_w_up, ffn2_w_down):
    for l in range(DEPTH):
        x = x + 0.5 * _swiglu(_rms_norm(x, norm_ffn1[l]), ffn1_w_gate[l], ffn1_w_up[l], ffn1_w_down[l])
        h = _rms_norm(x, norm_mix[l])
        proj = (h @ w_in[l]).astype(jnp.float32)
        p_rwkv = _token_shift(proj[..., :D_SHIFT], shift_mu[l])
        y_rwkv = _rwkv7_mixer(p_rwkv, rwkv_w0[l], rwkv_w2[l], rwkv_a0[l], rwkv_a2[l], rwkv_g2[l],
                              rwkv_k_k[l], rwkv_k_a[l], rwkv_r_k[l], rwkv_ln_w[l], rwkv_ln_b[l])
        y_attn = _attention_mixer(proj[..., D_SHIFT:D_SHIFT + 3 * D_ATTN], attn_q_norm[l], attn_k_norm[l])
        y_conv = _conv_mixer(proj[..., D_SHIFT + 3 * D_ATTN:], conv_dw_w[l], conv_dw_b[l],
                             conv_ln_w[l], conv_ln_b[l])
        mix = jnp.concatenate([y_rwkv, y_attn, y_conv], axis=-1).astype(x.dtype)
        x = x + mix @ w_out[l]
        x = x + 0.5 * _swiglu(_rms_norm(x, norm_ffn2[l]), ffn2_w_gate[l], ffn2_w_up[l], ffn2_w_down[l])
    return x
```

```python
import functools
import math

import jax
import jax.numpy as jnp
from jax import lax
from jax.experimental import pallas as pl
from jax.experimental.pallas import tpu as pltpu

F32 = jnp.float32
BF16 = jnp.bfloat16

HEAD_DIM = 64
NORM_EPS = 1e-6
RWKV_GN_EPS = 64e-5
CONV_LN_EPS = 1e-5
KK_EPS = 1e-12
CONV_WIDTH = 31
DILATED_PATTERNS = ((128, 1), (512, 4), (2048, 16))
ATTN_BLOCK = 128
LORA_SLAB = 128
CHUNK = 64
NEUMANN_STEPS = 6
MASK_VALUE = -1e30
EXP_NEG_HALF = math.exp(-0.5)
V7X_VMEM_LIMIT = 56 * 1024 * 1024

NT_DIMS = (((1,), (1,)), ((), ()))
TN_DIMS = (((0,), (0,)), ((), ()))


def _dot(a, b):
    return jnp.dot(a, b, preferred_element_type=F32)


def _dot_nt(a, b):
    return lax.dot_general(a, b, NT_DIMS, preferred_element_type=F32)


def _dot_tn(a, b):
    return lax.dot_general(a, b, TN_DIMS, preferred_element_type=F32)


def _split3(x):
    hi = x.astype(BF16)
    r1 = x - hi.astype(F32)
    mid = r1.astype(BF16)
    lo = (r1 - mid.astype(F32)).astype(BF16)
    return hi, mid, lo


def _dot_f32_lhs(x, w_exact):
    hi, mid, lo = _split3(x)
    return _dot(hi, w_exact) + _dot(mid, w_exact) + _dot(lo, w_exact)


def _dot_f32_rhs(w_exact, x):
    hi, mid, lo = _split3(x)
    return _dot(w_exact, hi) + _dot(w_exact, mid) + _dot(w_exact, lo)


def _dot_f32(x, w):
    xh, xm, xl = _split3(x)
    wh, wm, wl = _split3(w)
    return (_dot(xh, wh) + _dot(xm, wh) + _dot(xh, wm)
            + _dot(xl, wh) + _dot(xm, wm) + _dot(xh, wl))


def _sigmoid(x):
    return 1.0 / (1.0 + jnp.exp(-x))


def _rms_norm(x, g):
    return x * lax.rsqrt(jnp.mean(x * x, axis=-1, keepdims=True) + NORM_EPS) * g


def _swiglu_residual(x, g, wg_ref, wu_ref, wd_ref):
    xn = _rms_norm(x, g).astype(BF16)
    gate = _dot(xn, wg_ref[...])
    up = _dot(xn, wu_ref[...])
    h = (gate * _sigmoid(gate) * up).astype(BF16)
    return x + 0.5 * _dot(h, wd_ref[...])


def _ffn_proj_kernel(l_ref, x_ref, g1_ref, wg_ref, wu_ref, wd_ref, gm_ref, win_ref,
                     x1_ref, ps_ref, qkv_ref, u_ref):
    del l_ref
    x1 = _swiglu_residual(x_ref[...], g1_ref[...], wg_ref, wu_ref, wd_ref)
    x1_ref[...] = x1
    h = _rms_norm(x1, gm_ref[...]).astype(BF16)
    proj = _dot(h, win_ref[...])
    d_shift = ps_ref.shape[-1]
    d_qkv = qkv_ref.shape[-1]
    ps_ref[...] = proj[:, :d_shift]
    qkv_ref[...] = proj[:, d_shift:d_shift + d_qkv]
    u_ref[...] = proj[:, d_shift + d_qkv:]


def _layer_spec(shape, buffered=True):
    nd = len(shape)
    kw = dict(pipeline_mode=pl.Buffered(1)) if buffered else {}
    return pl.BlockSpec((None,) + tuple(shape), lambda *a: (a[-1][0],) + (0,) * nd, **kw)


def _ffn_proj(lidx, x, g1, wg, wu, wd, gm, win, dims, tm):
    T, D = x.shape
    d_shift, d_qkv, d_u = dims
    row = lambda w: pl.BlockSpec((tm, w), lambda i, l: (i, 0))
    grid_spec = pltpu.PrefetchScalarGridSpec(
        num_scalar_prefetch=1, grid=(T // tm,),
        in_specs=[row(D), _layer_spec((1, D)), _layer_spec(wg.shape[1:]), _layer_spec(wu.shape[1:]),
                  _layer_spec(wd.shape[1:]), _layer_spec((1, D)), _layer_spec(win.shape[1:])],
        out_specs=[row(D), row(d_shift), row(d_qkv), row(d_u)])
    return pl.pallas_call(
        _ffn_proj_kernel, grid_spec=grid_spec, name="ffn_proj",
        out_shape=[jax.ShapeDtypeStruct((T, D), F32), jax.ShapeDtypeStruct((T, d_shift), F32),
                   jax.ShapeDtypeStruct((T, d_qkv), F32), jax.ShapeDtypeStruct((T, d_u), F32)],
        compiler_params=pltpu.CompilerParams(dimension_semantics=("arbitrary",),
                                             vmem_limit_bytes=V7X_VMEM_LIMIT),
    )(lidx, x, g1, wg, wu, wd, gm, win)


def _out_ffn_kernel(l_ref, x_ref, yr_ref, ya_ref, yc_ref, wo_ref, g2_ref, wg_ref, wu_ref, wd_ref,
                    o_ref):
    del l_ref
    d_r = yr_ref.shape[-1]
    d_a = ya_ref.shape[-1]
    x2 = (x_ref[...]
          + _dot(yr_ref[...].astype(BF16), wo_ref[0:d_r, :])
          + _dot(ya_ref[...].astype(BF16), wo_ref[d_r:d_r + d_a, :])
          + _dot(yc_ref[...].astype(BF16), wo_ref[d_r + d_a:, :]))
    o_ref[...] = _swiglu_residual(x2, g2_ref[...], wg_ref, wu_ref, wd_ref)


def _out_ffn(lidx, x, yr, ya, yc, wo, g2, wg, wu, wd, tm):
    T, D = x.shape
    row = lambda w: pl.BlockSpec((tm, w), lambda i, l: (i, 0))
    grid_spec = pltpu.PrefetchScalarGridSpec(
        num_scalar_prefetch=1, grid=(T // tm,),
        in_specs=[row(D), row(yr.shape[1]), row(ya.shape[1]), row(yc.shape[1]),
                  _layer_spec(wo.shape[1:]), _layer_spec((1, D)), _layer_spec(wg.shape[1:]),
                  _layer_spec(wu.shape[1:]), _layer_spec(wd.shape[1:])],
        out_specs=row(D))
    return pl.pallas_call(
        _out_ffn_kernel, grid_spec=grid_spec, name="out_ffn",
        out_shape=jax.ShapeDtypeStruct((T, D), F32),
        compiler_params=pltpu.CompilerParams(dimension_semantics=("arbitrary",),
                                             vmem_limit_bytes=V7X_VMEM_LIMIT),
    )(lidx, x, yr, ya, yc, wo, g2, wg, wu, wd)


def _rwkv_kernel(l_ref, p_ref, mu_ref, w0_ref, w2_ref, a0_ref, a2_ref, g2_ref, kk_ref, ka_ref,
                 rk_ref, lnw_ref, lnb_ref, bd_ref, tri_ref, ones_ref, o_ref,
                 carry_ref, state_ref, at_s, rt_s, bt_s, kt_s, bp_s, kp_s, v_s, pc_s, y_s):
    del l_ref
    tb = p_ref.shape[1]
    d_r = o_ref.shape[-1]
    n_heads = d_r // HEAD_DIM

    @pl.when(pl.program_id(1) == 0)
    def _():
        carry_ref[...] = jnp.zeros_like(carry_ref)
        state_ref[...] = jnp.zeros_like(state_ref)

    p = p_ref[0]
    row = lax.broadcasted_iota(jnp.int32, (tb, 1), 0)
    prev = jnp.where(row == 0, carry_ref[...], pltpu.roll(p, 1, 0))
    carry_ref[...] = p[tb - 1:tb, :]
    ps = p + (prev - p) * mu_ref[...]

    r = ps[:, 0:d_r]
    k = ps[:, d_r:2 * d_r]
    v = ps[:, 2 * d_r:3 * d_r]
    xs = ps[:, 3 * d_r:3 * d_r + LORA_SLAB]

    bd = bd_ref[...]
    seg_sum = lambda t: _dot_f32_lhs(t, bd)

    lw = w0_ref[...] + _dot_f32(jnp.tanh(xs), w2_ref[...])
    ld = -EXP_NEG_HALF * _sigmoid(lw)
    a = _sigmoid(a0_ref[...] + _dot_f32(xs, a2_ref[...]))
    g = _dot_f32(_sigmoid(xs), g2_ref[...])
    kk = k * kk_ref[...]
    kk = kk * lax.rsqrt(seg_sum(kk * kk) + KK_EPS)
    k2 = k * (1.0 + (a - 1.0) * ka_ref[...])
    b = kk * a

    lcum = _dot_f32_rhs(tri_ref[...], ld)
    ltot = _dot_f32_rhs(ones_ref[...], ld)
    at_s[...] = (-kk * jnp.exp(lcum - ld)).astype(BF16)
    rt_s[...] = (r * jnp.exp(lcum)).astype(BF16)
    inv_p = jnp.exp(-lcum)
    bt_s[...] = (b * inv_p).astype(BF16)
    kt_s[...] = (k2 * inv_p).astype(BF16)
    rest = jnp.exp(ltot - lcum)
    bp_s[...] = (b * rest).astype(BF16)
    kp_s[...] = (k2 * rest).astype(BF16)
    v_s[...] = v.astype(BF16)
    pc_s[...] = jnp.exp(ltot)

    ti = lax.broadcasted_iota(jnp.int32, (CHUNK, CHUNK), 0)
    si = lax.broadcasted_iota(jnp.int32, (CHUNK, CHUNK), 1)
    strict = ti > si
    incl = ti >= si

    def chunk_body(c, carry):
        r0 = pl.multiple_of(c * CHUNK, CHUNK)
        rows = pl.ds(r0, CHUNK)
        for h in range(n_heads):
            ls = slice(h * HEAD_DIM, (h + 1) * HEAD_DIM)
            at, rt, bt, kt = at_s[rows, ls], rt_s[rows, ls], bt_s[rows, ls], kt_s[rows, ls]
            bp, kp, vc = bp_s[rows, ls], kp_s[rows, ls], v_s[rows, ls]
            a_ab = jnp.where(strict, _dot_nt(at, bt), 0.0)
            a_ak = jnp.where(strict, _dot_nt(at, kt), 0.0).astype(BF16)
            a_rb = jnp.where(incl, _dot_nt(rt, bt), 0.0).astype(BF16)
            a_rk = jnp.where(incl, _dot_nt(rt, kt), 0.0).astype(BF16)
            s0 = state_ref[h]
            s0b = s0.astype(BF16)
            x = _dot_nt(at, s0b) + _dot(a_ak, vc)
            ak = a_ab
            for it in range(NEUMANN_STEPS):
                akb = ak.astype(BF16)
                x = x + _dot(akb, x.astype(BF16))
                if it + 1 < NEUMANN_STEPS:
                    ak = _dot(akb, akb)
            ub = x.astype(BF16)
            y = _dot_nt(rt, s0b) + _dot(a_rb, ub) + _dot(a_rk, vc)
            pc = pc_s[pl.ds(r0, 1), ls]
            state_ref[h] = s0 * pc + _dot_tn(ub, bp) + _dot_tn(vc, kp)
            y_s[rows, ls] = y
        return carry

    lax.fori_loop(0, tb // CHUNK, chunk_body, 0)

    y = y_s[...]
    inv_n = 1.0 / HEAD_DIM
    mean = seg_sum(y) * inv_n
    d = y - mean
    var = seg_sum(d * d) * inv_n
    yn = d * lax.rsqrt(var + RWKV_GN_EPS) * lnw_ref[...] + lnb_ref[...]
    bonus = seg_sum(r * k2 * rk_ref[...]) * v
    o_ref[0] = (yn + bonus) * g


def _rwkv(lidx, ps3, mu, w0, w2p, a0, a2p, g2p, k_k, k_a, r_k, ln_w, ln_b, d_r, tb):
    B, S, d_shift = ps3.shape
    n_heads = d_r // HEAD_DIM
    hid = jnp.arange(d_r) // HEAD_DIM
    bd = (hid[:, None] == hid[None, :]).astype(BF16)
    ci = jnp.arange(tb)
    same = (ci[:, None] // CHUNK) == (ci[None, :] // CHUNK)
    tri = (same & (ci[:, None] >= ci[None, :])).astype(BF16)
    ones = same.astype(BF16)
    const = lambda a: pl.BlockSpec(a.shape, lambda b, t, l: (0,) * a.ndim)
    vec = _layer_spec((1, d_r), buffered=False)
    lora = _layer_spec((LORA_SLAB, d_r), buffered=False)
    grid_spec = pltpu.PrefetchScalarGridSpec(
        num_scalar_prefetch=1, grid=(B, S // tb),
        in_specs=[pl.BlockSpec((1, tb, d_shift), lambda b, t, l: (b, t, 0)),
                  _layer_spec((1, d_shift), buffered=False),
                  vec, lora, vec, lora, lora, vec, vec, vec, vec, vec,
                  const(bd), const(tri), const(ones)],
        out_specs=pl.BlockSpec((1, tb, d_r), lambda b, t, l: (b, t, 0)),
        scratch_shapes=[pltpu.VMEM((1, d_shift), F32),
                        pltpu.VMEM((n_heads, HEAD_DIM, HEAD_DIM), F32)]
        + [pltpu.VMEM((tb, d_r), BF16)] * 7
        + [pltpu.VMEM((tb, d_r), F32)] * 2)
    return pl.pallas_call(
        _rwkv_kernel, grid_spec=grid_spec, name="rwkv7",
        out_shape=jax.ShapeDtypeStruct((B, S, d_r), F32),
        compiler_params=pltpu.CompilerParams(dimension_semantics=("arbitrary", "arbitrary"),
                                             vmem_limit_bytes=V7X_VMEM_LIMIT),
    )(lidx, ps3, mu, w0, w2p, a0, a2p, g2p, k_k, k_a, r_k, ln_w, ln_b, bd, tri, ones)


def _attn_kernel(l_ref, q_ref, k_ref, v_ref, qn_ref, kn_ref, bd_ref, bias_ref, o_ref,
                 q_s, k_s, acc_s, m_s, l_s):
    del l_ref
    S = q_ref.shape[1]
    bd = bd_ref[...]
    inv_n = 1.0 / HEAD_DIM

    def head_rms(t, gain):
        ms = _dot_f32_lhs(t * t, bd) * inv_n
        return t * lax.rsqrt(ms + NORM_EPS) * gain

    q_s[...] = head_rms(q_ref[0], qn_ref[...]) * (HEAD_DIM ** -0.5)
    k_s[...] = head_rms(k_ref[0], kn_ref[...])

    for pi, (window, dil) in enumerate(DILATED_PATTERNS):
        n_sub = S // dil
        n_blk = n_sub // ATTN_BLOCK

        def block(q_start, k_start, n_keys, key_off):
            if dil > 1:
                q_rows = pl.ds(q_start, ATTN_BLOCK, stride=dil)
                k_rows = pl.ds(k_start, n_keys, stride=dil)
            else:
                q_rows = pl.ds(pl.multiple_of(q_start, ATTN_BLOCK), ATTN_BLOCK)
                k_rows = pl.ds(pl.multiple_of(k_start, ATTN_BLOCK), n_keys)
            q2 = q_s[q_rows, :].astype(BF16)
            k2 = k_s[k_rows, :].astype(BF16)
            v2 = v_ref[0, k_rows, :].astype(BF16)
            o_parts, m_parts, l_parts = [], [], []
            for h in range(2):
                ls = slice(h * HEAD_DIM, (h + 1) * HEAD_DIM)
                s = _dot_nt(q2[:, ls], k2[:, ls]) + bias_ref[pi, h, :, key_off:]
                m_blk = jnp.max(s, axis=-1, keepdims=True)
                e = jnp.exp(s - m_blk)
                l_blk = jnp.sum(e, axis=-1, keepdims=True)
                o_parts.append(_dot(e.astype(BF16), v2[:, ls]))
                m_parts.append(jnp.broadcast_to(m_blk, (ATTN_BLOCK, HEAD_DIM)))
                l_parts.append(jnp.broadcast_to(l_blk, (ATTN_BLOCK, HEAD_DIM)))
            o_blk = jnp.concatenate(o_parts, axis=-1)
            m_blk = jnp.concatenate(m_parts, axis=-1)
            l_blk = jnp.concatenate(l_parts, axis=-1)
            if pi == 0:
                acc_s[q_rows, :] = o_blk
                m_s[q_rows, :] = m_blk
                l_s[q_rows, :] = l_blk
            else:
                m_old = m_s[q_rows, :]
                m_new = jnp.maximum(m_old, m_blk)
                w_old = jnp.exp(m_old - m_new)
                w_blk = jnp.exp(m_blk - m_new)
                acc_s[q_rows, :] = acc_s[q_rows, :] * w_old + o_blk * w_blk
                l_s[q_rows, :] = l_s[q_rows, :] * w_old + l_blk * w_blk
                m_s[q_rows, :] = m_new

        def first_body(c, carry):
            block(c, c, ATTN_BLOCK, ATTN_BLOCK)
            return carry

        lax.fori_loop(0, dil, first_body, 0)

        def rest_body(i, carry):
            n = 1 + i // dil
            c = i % dil
            q_start = c + dil * ATTN_BLOCK * n
            block(q_start, q_start - dil * ATTN_BLOCK, 2 * ATTN_BLOCK, 0)
            return carry

        lax.fori_loop(0, dil * (n_blk - 1), rest_body, 0)

    o_ref[0] = acc_s[...] / l_s[...]


def _alibi_slopes(n):
    def pow2(m):
        start = 2.0 ** (-8.0 / m)
        return [start ** (i + 1) for i in range(m)]
    if math.log2(n).is_integer():
        return pow2(n)
    c = 2 ** int(math.floor(math.log2(n)))
    return pow2(c) + pow2(2 * c)[0::2][: n - c]


def _attn_bias(n_heads):
    qi = jnp.arange(ATTN_BLOCK)
    ki = jnp.arange(2 * ATTN_BLOCK)
    dist = qi[:, None] + ATTN_BLOCK - ki[None, :]
    slopes = jnp.asarray(_alibi_slopes(n_heads), F32)
    out = []
    for window, dil in DILATED_PATTERNS:
        valid = (dist >= 0) & (dist <= window // dil)
        bias = -slopes[:, None, None] * (dist * dil).astype(F32)[None]
        out.append(jnp.where(valid[None], bias, MASK_VALUE))
    return jnp.stack(out, 0)


def _attention(lidx, qkv3, q_norm, k_norm, d_a):
    B, S, _ = qkv3.shape
    n_heads = d_a // HEAD_DIM
    n_pairs = n_heads // 2
    slab = 2 * HEAD_DIM
    hid = jnp.arange(slab) // HEAD_DIM
    bd = (hid[:, None] == hid[None, :]).astype(BF16)
    bias = _attn_bias(n_heads)
    col = lambda off: pl.BlockSpec((1, S, slab), lambda b, p, l: (b, 0, off + p))
    grid_spec = pltpu.PrefetchScalarGridSpec(
        num_scalar_prefetch=1, grid=(B, n_pairs),
        in_specs=[col(0), col(n_pairs), col(2 * n_pairs),
                  _layer_spec((1, slab), buffered=False), _layer_spec((1, slab), buffered=False),
                  pl.BlockSpec(bd.shape, lambda b, p, l: (0, 0)),
                  pl.BlockSpec((len(DILATED_PATTERNS), 2, ATTN_BLOCK, 2 * ATTN_BLOCK),
                               lambda b, p, l: (0, p, 0, 0))],
        out_specs=pl.BlockSpec((1, S, slab), lambda b, p, l: (b, 0, p)),
        scratch_shapes=[pltpu.VMEM((S, slab), F32)] * 5)
    return pl.pallas_call(
        _attn_kernel, grid_spec=grid_spec, name="dilated_attn",
        out_shape=jax.ShapeDtypeStruct((B, S, d_a), F32),
        compiler_params=pltpu.CompilerParams(dimension_semantics=("arbitrary", "arbitrary"),
                                             vmem_limit_bytes=V7X_VMEM_LIMIT),
    )(lidx, qkv3, qkv3, qkv3, q_norm, k_norm, bd, bias)


CONV_PAD = 32
CONV_ROWS = 128
SUBLANES = 8


def _conv_kernel(l_ref, u_ref, w_ref, b_ref, lnw_ref, lnb_ref, o_ref, z_s):
    del l_ref
    S = u_ref.shape[1]
    d_c = o_ref.shape[-1]
    z_s[0:CONV_PAD, :] = jnp.zeros((CONV_PAD, d_c), F32)
    z_s[CONV_PAD:, :] = u_ref[0, :, 0:d_c] * _sigmoid(u_ref[0, :, d_c:])
    shift = CONV_PAD - (CONV_WIDTH - 1)

    def tile(i, carry):
        r0 = pl.multiple_of(i * CONV_ROWS, CONV_ROWS)
        acc = jnp.zeros((CONV_ROWS, d_c), F32) + b_ref[...]
        win = z_s[pl.ds(r0, CONV_ROWS + CONV_PAD), :]
        n_win = CONV_ROWS + CONV_PAD
        for sub in range(SUBLANES):
            rolled = win if sub == 0 else pltpu.roll(win, n_win - sub, 0)
            for j in range(CONV_WIDTH):
                off = shift + j
                if off % SUBLANES == sub:
                    base = off - sub
                    acc = acc + rolled[base:base + CONV_ROWS, :] * w_ref[j:j + 1, :]
        mean = jnp.mean(acc, axis=-1, keepdims=True)
        d = acc - mean
        var = jnp.mean(d * d, axis=-1, keepdims=True)
        z = d * lax.rsqrt(var + CONV_LN_EPS) * lnw_ref[...] + lnb_ref[...]
        o_ref[0, pl.ds(r0, CONV_ROWS), :] = z * _sigmoid(z)
        return carry

    lax.fori_loop(0, S // CONV_ROWS, tile, 0)


def _conv(lidx, u3, dw_w, dw_b, ln_w, ln_b):
    B, S, d_u = u3.shape
    d_c = d_u // 2
    grid_spec = pltpu.PrefetchScalarGridSpec(
        num_scalar_prefetch=1, grid=(B,),
        in_specs=[pl.BlockSpec((1, S, d_u), lambda b, l: (b, 0, 0)),
                  _layer_spec((CONV_WIDTH, d_c), buffered=False),
                  _layer_spec((1, d_c), buffered=False), _layer_spec((1, d_c), buffered=False),
                  _layer_spec((1, d_c), buffered=False)],
        out_specs=pl.BlockSpec((1, S, d_c), lambda b, l: (b, 0, 0)),
        scratch_shapes=[pltpu.VMEM((S + CONV_PAD, d_c), F32)])
    return pl.pallas_call(
        _conv_kernel, grid_spec=grid_spec, name="conv_module",
        out_shape=jax.ShapeDtypeStruct((B, S, d_c), F32),
        compiler_params=pltpu.CompilerParams(dimension_semantics=("arbitrary",),
                                             vmem_limit_bytes=V7X_VMEM_LIMIT),
    )(lidx, u3, dw_w, dw_b, ln_w, ln_b)


def _pad_lora(w, offset):
    L, r, d = w.shape
    return jnp.zeros((L, LORA_SLAB, d), F32).at[:, offset:offset + r, :].set(w)


def kernel(x, norm_ffn1, ffn1_w_gate, ffn1_w_up, ffn1_w_down, norm_mix, w_in, shift_mu, rwkv_w0, rwkv_w2, rwkv_a0, rwkv_a2, rwkv_g2, rwkv_k_k, rwkv_k_a, rwkv_r_k, rwkv_ln_w, rwkv_ln_b, attn_q_norm, attn_k_norm, conv_dw_w, conv_dw_b, conv_ln_w, conv_ln_b, w_out, norm_ffn2, ffn2_w_gate, ffn2_w_up, ffn2_w_down):
    B, S, D = x.shape
    depth = w_in.shape[0]
    d_r = rwkv_w0.shape[-1]
    d_c = conv_dw_b.shape[-1]
    d_shift = shift_mu.shape[-1]
    d_a = (w_in.shape[-1] - d_shift - 2 * d_c) // 3
    r_w, r_a, r_g = rwkv_w2.shape[1], rwkv_a2.shape[1], rwkv_g2.shape[1]
    assert d_shift == 3 * d_r + LORA_SLAB and r_w + r_a + r_g == LORA_SLAB
    assert S % (ATTN_BLOCK * DILATED_PATTERNS[-1][1]) == 0 and (d_a // HEAD_DIM) % 2 == 0
    T = B * S
    tm = 256
    tb = 256

    vec = lambda a: a.reshape(depth, 1, -1)
    bf = lambda a: a.astype(BF16)
    wg1, wu1, wd1 = bf(ffn1_w_gate), bf(ffn1_w_up), bf(ffn1_w_down)
    wg2, wu2, wd2 = bf(ffn2_w_gate), bf(ffn2_w_up), bf(ffn2_w_down)
    win, wo = bf(w_in), bf(w_out)
    g1, gm, g2 = vec(norm_ffn1), vec(norm_mix), vec(norm_ffn2)
    w2p = _pad_lora(rwkv_w2, 0)
    a2p = _pad_lora(rwkv_a2, r_w)
    g2p = _pad_lora(rwkv_g2, r_w + r_a)
    qn = vec(jnp.tile(attn_q_norm, (1, 2)))
    kn = vec(jnp.tile(attn_k_norm, (1, 2)))

    def layer(l, xf):
        lidx = jnp.reshape(l, (1,)).astype(jnp.int32)
        x1, ps, qkv, u = _ffn_proj(lidx, xf, g1, wg1, wu1, wd1, gm, win, (d_shift, 3 * d_a, 2 * d_c), tm)
        y_r = _rwkv(lidx, ps.reshape(B, S, d_shift), vec(shift_mu), vec(rwkv_w0), w2p, vec(rwkv_a0),
                    a2p, g2p, vec(rwkv_k_k), vec(rwkv_k_a), vec(rwkv_r_k), vec(rwkv_ln_w),
                    vec(rwkv_ln_b), d_r, tb)
        y_a = _attention(lidx, qkv.reshape(B, S, 3 * d_a), qn, kn, d_a)
        y_c = _conv(lidx, u.reshape(B, S, 2 * d_c), conv_dw_w, vec(conv_dw_b), vec(conv_ln_w),
                    vec(conv_ln_b))
        return _out_ffn(lidx, x1, y_r.reshape(T, d_r), y_a.reshape(T, d_a), y_c.reshape(T, d_c),
                        wo, g2, wg2, wu2, wd2, tm)

    out = lax.fori_loop(0, depth, layer, x.reshape(T, D))
    return out.reshape(B, S, D)
```

```python
import functools
import math

import jax
import jax.numpy as jnp
from jax import lax
from jax.experimental import pallas as pl
from jax.experimental.pallas import tpu as pltpu

F32 = jnp.float32
BF16 = jnp.bfloat16

HEAD_DIM = 64
PAIR = 2 * HEAD_DIM
NORM_EPS = 1e-6
RWKV_GN_EPS = 64e-5
CONV_LN_EPS = 1e-5
KK_EPS = 1e-12
CONV_WIDTH = 31
DILATED_PATTERNS = ((128, 1), (512, 4), (2048, 16))
ATTN_BLOCK = 128
LORA_SLAB = 128
CHUNK = 64
NEUMANN_STEPS = 6
MASK_VALUE = -1e30
EXP_NEG_HALF = math.exp(-0.5)
V7X_VMEM_LIMIT = 56 * 1024 * 1024

NT_DIMS = (((1,), (1,)), ((), ()))
TN_DIMS = (((0,), (0,)), ((), ()))


def _dot(a, b):
    return jnp.dot(a, b, preferred_element_type=F32)


def _dot_nt(a, b):
    return lax.dot_general(a, b, NT_DIMS, preferred_element_type=F32)


def _dot_tn(a, b):
    return lax.dot_general(a, b, TN_DIMS, preferred_element_type=F32)


def _split3(x):
    hi = x.astype(BF16)
    r1 = x - hi.astype(F32)
    mid = r1.astype(BF16)
    lo = (r1 - mid.astype(F32)).astype(BF16)
    return hi, mid, lo


def _dot_f32_lhs(x, w_exact):
    hi, mid, lo = _split3(x)
    return _dot(hi, w_exact) + _dot(mid, w_exact) + _dot(lo, w_exact)


def _dot_f32_rhs(w_exact, x):
    hi, mid, lo = _split3(x)
    return _dot(w_exact, hi) + _dot(w_exact, mid) + _dot(w_exact, lo)


def _dot_f32(x, w):
    xh, xm, xl = _split3(x)
    wh, wm, wl = _split3(w)
    return (_dot(xh, wh) + _dot(xm, wh) + _dot(xh, wm)
            + _dot(xl, wh) + _dot(xm, wm) + _dot(xh, wl))


def _sigmoid(x):
    return 1.0 / (1.0 + jnp.exp(-x))


def _rms_norm(x, g):
    return x * lax.rsqrt(jnp.mean(x * x, axis=-1, keepdims=True) + NORM_EPS) * g


def _swiglu_residual(x, g, wg_ref, wu_ref, wd_ref):
    xn = _rms_norm(x, g).astype(BF16)
    gate = _dot(xn, wg_ref[...])
    up = _dot(xn, wu_ref[...])
    h = (gate * _sigmoid(gate) * up).astype(BF16)
    return x + 0.5 * _dot(h, wd_ref[...])


def _ffn_proj_kernel(l_ref, x_ref, g1_ref, wg_ref, wu_ref, wd_ref, gm_ref, win_ref,
                     x1_ref, ps_ref, qkv_ref, u_ref):
    del l_ref
    x1 = _swiglu_residual(x_ref[...], g1_ref[...], wg_ref, wu_ref, wd_ref)
    x1_ref[...] = x1
    h = _rms_norm(x1, gm_ref[...]).astype(BF16)
    proj = _dot(h, win_ref[...])
    d_shift = ps_ref.shape[-1]
    d_qkv = qkv_ref.shape[-1]
    ps_ref[...] = proj[:, :d_shift]
    qkv_ref[...] = proj[:, d_shift:d_shift + d_qkv]
    u_ref[...] = proj[:, d_shift + d_qkv:]


def _layer_spec(shape, buffered=True):
    nd = len(shape)
    kw = dict(pipeline_mode=pl.Buffered(1)) if buffered else {}
    return pl.BlockSpec((None,) + tuple(shape), lambda *a: (a[-1][0],) + (0,) * nd, **kw)


def _ffn_proj(lidx, x, g1, wg, wu, wd, gm, win, dims, tm):
    T, D = x.shape
    d_shift, d_qkv, d_u = dims
    row = lambda w: pl.BlockSpec((tm, w), lambda i, l: (i, 0))
    grid_spec = pltpu.PrefetchScalarGridSpec(
        num_scalar_prefetch=1, grid=(T // tm,),
        in_specs=[row(D), _layer_spec((1, D)), _layer_spec(wg.shape[1:]), _layer_spec(wu.shape[1:]),
                  _layer_spec(wd.shape[1:]), _layer_spec((1, D)), _layer_spec(win.shape[1:])],
        out_specs=[row(D), row(d_shift), row(d_qkv), row(d_u)])
    return pl.pallas_call(
        _ffn_proj_kernel, grid_spec=grid_spec, name="ffn_proj",
        out_shape=[jax.ShapeDtypeStruct((T, D), F32), jax.ShapeDtypeStruct((T, d_shift), F32),
                   jax.ShapeDtypeStruct((T, d_qkv), F32), jax.ShapeDtypeStruct((T, d_u), F32)],
        compiler_params=pltpu.CompilerParams(dimension_semantics=("arbitrary",),
                                             vmem_limit_bytes=V7X_VMEM_LIMIT),
    )(lidx, x, g1, wg, wu, wd, gm, win)


def _out_ffn_kernel(l_ref, x_ref, yr_ref, ya_ref, yc_ref, wo_ref, g2_ref, wg_ref, wu_ref, wd_ref,
                    o_ref):
    del l_ref
    d_r = yr_ref.shape[-1]
    d_a = ya_ref.shape[-1]
    x2 = (x_ref[...]
          + _dot(yr_ref[...].astype(BF16), wo_ref[0:d_r, :])
          + _dot(ya_ref[...].astype(BF16), wo_ref[d_r:d_r + d_a, :])
          + _dot(yc_ref[...].astype(BF16), wo_ref[d_r + d_a:, :]))
    o_ref[...] = _swiglu_residual(x2, g2_ref[...], wg_ref, wu_ref, wd_ref)


def _out_ffn(lidx, x, yr, ya, yc, wo, g2, wg, wu, wd, tm):
    T, D = x.shape
    row = lambda w: pl.BlockSpec((tm, w), lambda i, l: (i, 0))
    grid_spec = pltpu.PrefetchScalarGridSpec(
        num_scalar_prefetch=1, grid=(T // tm,),
        in_specs=[row(D), row(yr.shape[1]), row(ya.shape[1]), row(yc.shape[1]),
                  _layer_spec(wo.shape[1:]), _layer_spec((1, D)), _layer_spec(wg.shape[1:]),
                  _layer_spec(wu.shape[1:]), _layer_spec(wd.shape[1:])],
        out_specs=row(D))
    return pl.pallas_call(
        _out_ffn_kernel, grid_spec=grid_spec, name="out_ffn",
        out_shape=jax.ShapeDtypeStruct((T, D), F32),
        compiler_params=pltpu.CompilerParams(dimension_semantics=("arbitrary",),
                                             vmem_limit_bytes=V7X_VMEM_LIMIT),
    )(lidx, x, yr, ya, yc, wo, g2, wg, wu, wd)


def _rwkv_kernel(l_ref, p_ref, mu_ref, w0_ref, w2_ref, a0_ref, a2_ref, g2_ref, kk_ref, ka_ref,
                 rk_ref, lnw_ref, lnb_ref, bd_ref, tri_ref, ones_ref, o_ref,
                 carry_ref, state_ref):
    del l_ref
    tb = p_ref.shape[1]
    d_r = o_ref.shape[-1]

    @pl.when(pl.program_id(1) == 0)
    def _():
        carry_ref[...] = jnp.zeros_like(carry_ref)
        state_ref[...] = jnp.zeros_like(state_ref)

    p = p_ref[0]
    row = lax.broadcasted_iota(jnp.int32, (tb, 1), 0)
    prev = jnp.where(row == 0, carry_ref[...], pltpu.roll(p, 1, 0))
    carry_ref[...] = p[tb - 1:tb, :]
    ps = p + (prev - p) * mu_ref[...]

    r = ps[:, 0:d_r]
    k = ps[:, d_r:2 * d_r]
    v = ps[:, 2 * d_r:3 * d_r]
    xs = ps[:, 3 * d_r:3 * d_r + LORA_SLAB]

    bd = bd_ref[...]
    seg_sum = lambda t: _dot_f32_lhs(t, bd)

    lw = w0_ref[...] + _dot_f32(jnp.tanh(xs), w2_ref[...])
    ld = -EXP_NEG_HALF * _sigmoid(lw)
    a = _sigmoid(a0_ref[...] + _dot_f32(xs, a2_ref[...]))
    g = _dot_f32(_sigmoid(xs), g2_ref[...])
    kk = k * kk_ref[...]
    kk = kk * lax.rsqrt(seg_sum(kk * kk) + KK_EPS)
    k2 = k * (1.0 + (a - 1.0) * ka_ref[...])
    b = kk * a

    lcum = _dot_f32_rhs(tri_ref[...], ld)
    ltot = _dot_f32_rhs(ones_ref[...], ld)
    at_all = (-kk * jnp.exp(lcum - ld)).astype(BF16)
    rt_all = r * jnp.exp(lcum)
    inv_p = jnp.exp(-lcum)
    bt_all = (b * inv_p).astype(BF16)
    kt_all = (k2 * inv_p).astype(BF16)
    rest = jnp.exp(ltot - lcum)
    bp_all = (b * rest).astype(BF16)
    kp_all = (k2 * rest).astype(BF16)
    v_all = v.astype(BF16)
    pc_all = jnp.exp(ltot)

    lane = lax.broadcasted_iota(jnp.int32, (CHUNK, PAIR), 1)
    trow = lax.broadcasted_iota(jnp.int32, (CHUNK, PAIR), 0)
    head0 = lane < HEAD_DIM
    scol = lane & (HEAD_DIM - 1)
    strict = trow > scol
    incl = trow >= scol
    eye2 = jnp.where(trow == scol, 1.0, 0.0)
    r2 = lax.broadcasted_iota(jnp.int32, (PAIR, PAIR), 0)
    c2 = lax.broadcasted_iota(jnp.int32, (PAIR, PAIR), 1)
    same_head = (r2 < HEAD_DIM) == (c2 < HEAD_DIM)

    def block_diag(xb):
        zero = jnp.zeros_like(xb)
        return jnp.concatenate([jnp.where(head0, xb, zero), jnp.where(head0, zero, xb)], axis=0)

    n_chunks = tb // CHUNK
    n_pairs = d_r // PAIR
    units = [(c, p) for c in range(n_chunks) for p in range(n_pairs)]
    tile = lambda t, c, p: t[c * CHUNK:(c + 1) * CHUNK, p * PAIR:(p + 1) * PAIR]

    a_ab, a_ak, a_rb, a_rk = {}, {}, {}, {}
    for u in units:
        ar = jnp.concatenate([tile(at_all, *u), tile(rt_all, *u).astype(BF16)], axis=0)
        gb = _dot_nt(ar, block_diag(tile(bt_all, *u)))
        gk = _dot_nt(ar, block_diag(tile(kt_all, *u)))
        a_ab[u] = jnp.where(strict, gb[:CHUNK], 0.0)
        a_rb[u] = jnp.where(incl, gb[CHUNK:], 0.0).astype(BF16)
        a_ak[u] = jnp.where(strict, gk[:CHUNK], 0.0).astype(BF16)
        a_rk[u] = jnp.where(incl, gk[CHUNK:], 0.0).astype(BF16)
    tinv = {u: eye2 + a_ab[u] for u in units}
    apow = dict(a_ab)
    for _ in range(NEUMANN_STEPS - 1):
        for u in units:
            ab = apow[u].astype(BF16)
            apow[u] = _dot(ab, block_diag(ab))
        for u in units:
            tinv[u] = tinv[u] + _dot(tinv[u].astype(BF16), block_diag(apow[u].astype(BF16)))
    av, w1, u1, rq, y1, mx, gx = {}, {}, {}, {}, {}, {}, {}
    for u in units:
        av[u] = _dot(a_ak[u], block_diag(tile(v_all, *u)))
    for u in units:
        tb16 = tinv[u].astype(BF16)
        w1[u] = _dot(tb16, block_diag(tile(at_all, *u))).astype(BF16)
        u1[u] = _dot(tb16, block_diag(av[u].astype(BF16))).astype(BF16)
    for u in units:
        vc, bp, kp = tile(v_all, *u), tile(bp_all, *u), tile(kp_all, *u)
        rq[u] = (tile(rt_all, *u) + _dot(a_rb[u], block_diag(w1[u]))).astype(BF16)
        y1[u] = _dot(a_rb[u], block_diag(u1[u])) + _dot(a_rk[u], block_diag(vc))
        mx[u] = jnp.where(same_head, _dot_tn(w1[u], bp), 0.0).astype(BF16)
        gx[u] = jnp.where(same_head, _dot_tn(jnp.concatenate([u1[u], vc], axis=0),
                                             jnp.concatenate([bp, kp], axis=0)), 0.0)

    y_rows = []
    for c in range(n_chunks):
        y_parts = []
        for p in range(n_pairs):
            u = (c, p)
            s0 = state_ref[p]
            s0b = s0.astype(BF16)
            y_parts.append(_dot_nt(rq[u], s0b) + y1[u])
            pc = pc_all[c * CHUNK:c * CHUNK + 1, p * PAIR:(p + 1) * PAIR]
            state_ref[p] = s0 * pc + _dot(s0b, mx[u]) + gx[u]
        y_rows.append(jnp.concatenate(y_parts, axis=1))
    y = jnp.concatenate(y_rows, axis=0)
    inv_n = 1.0 / HEAD_DIM
    mean = seg_sum(y) * inv_n
    d = y - mean
    var = seg_sum(d * d) * inv_n
    yn = d * lax.rsqrt(var + RWKV_GN_EPS) * lnw_ref[...] + lnb_ref[...]
    bonus = seg_sum(r * k2 * rk_ref[...]) * v
    o_ref[0] = (yn + bonus) * g


def _rwkv(lidx, ps3, mu, w0, w2p, a0, a2p, g2p, k_k, k_a, r_k, ln_w, ln_b, d_r, tb):
    B, S, d_shift = ps3.shape
    hid = jnp.arange(d_r) // HEAD_DIM
    bd = (hid[:, None] == hid[None, :]).astype(BF16)
    ci = jnp.arange(tb)
    same = (ci[:, None] // CHUNK) == (ci[None, :] // CHUNK)
    tri = (same & (ci[:, None] >= ci[None, :])).astype(BF16)
    ones = same.astype(BF16)
    const = lambda a: pl.BlockSpec(a.shape, lambda b, t, l: (0,) * a.ndim)
    vec = _layer_spec((1, d_r), buffered=False)
    lora = _layer_spec((LORA_SLAB, d_r), buffered=False)
    grid_spec = pltpu.PrefetchScalarGridSpec(
        num_scalar_prefetch=1, grid=(B, S // tb),
        in_specs=[pl.BlockSpec((1, tb, d_shift), lambda b, t, l: (b, t, 0)),
                  _layer_spec((1, d_shift), buffered=False),
                  vec, lora, vec, lora, lora, vec, vec, vec, vec, vec,
                  const(bd), const(tri), const(ones)],
        out_specs=pl.BlockSpec((1, tb, d_r), lambda b, t, l: (b, t, 0)),
        scratch_shapes=[pltpu.VMEM((1, d_shift), F32),
                        pltpu.VMEM((d_r // PAIR, PAIR, PAIR), F32)])
    return pl.pallas_call(
        _rwkv_kernel, grid_spec=grid_spec, name="rwkv7",
        out_shape=jax.ShapeDtypeStruct((B, S, d_r), F32),
        compiler_params=pltpu.CompilerParams(dimension_semantics=("arbitrary", "arbitrary"),
                                             vmem_limit_bytes=V7X_VMEM_LIMIT),
    )(lidx, ps3, mu, w0, w2p, a0, a2p, g2p, k_k, k_a, r_k, ln_w, ln_b, bd, tri, ones)


def _attn_kernel(l_ref, q_ref, k_ref, v_ref, qn_ref, kn_ref, bd_ref, bias_ref, o_ref,
                 q_s, k_s, acc_s, m_s, l_s):
    del l_ref
    S = q_ref.shape[1]
    bd = bd_ref[...]
    inv_n = 1.0 / HEAD_DIM

    def head_rms(t, gain):
        ms = _dot_f32_lhs(t * t, bd) * inv_n
        return t * lax.rsqrt(ms + NORM_EPS) * gain

    q_s[...] = head_rms(q_ref[0], qn_ref[...]) * (HEAD_DIM ** -0.5)
    k_s[...] = head_rms(k_ref[0], kn_ref[...])

    for pi, (window, dil) in enumerate(DILATED_PATTERNS):
        n_sub = S // dil
        n_blk = n_sub // ATTN_BLOCK

        def block(q_start, k_start, n_keys, key_off):
            if dil > 1:
                q_rows = pl.ds(q_start, ATTN_BLOCK, stride=dil)
                k_rows = pl.ds(k_start, n_keys, stride=dil)
            else:
                q_rows = pl.ds(pl.multiple_of(q_start, ATTN_BLOCK), ATTN_BLOCK)
                k_rows = pl.ds(pl.multiple_of(k_start, ATTN_BLOCK), n_keys)
            q2 = q_s[q_rows, :].astype(BF16)
            k2 = k_s[k_rows, :].astype(BF16)
            v2 = v_ref[0, k_rows, :].astype(BF16)
            o_parts, m_parts, l_parts = [], [], []
            for h in range(2):
                ls = slice(h * HEAD_DIM, (h + 1) * HEAD_DIM)
                s = _dot_nt(q2[:, ls], k2[:, ls]) + bias_ref[pi, h, :, key_off:]
                m_blk = jnp.max(s, axis=-1, keepdims=True)
                e = jnp.exp(s - m_blk)
                l_blk = jnp.sum(e, axis=-1, keepdims=True)
                o_parts.append(_dot(e.astype(BF16), v2[:, ls]))
                m_parts.append(jnp.broadcast_to(m_blk, (ATTN_BLOCK, HEAD_DIM)))
                l_parts.append(jnp.broadcast_to(l_blk, (ATTN_BLOCK, HEAD_DIM)))
            o_blk = jnp.concatenate(o_parts, axis=-1)
            m_blk = jnp.concatenate(m_parts, axis=-1)
            l_blk = jnp.concatenate(l_parts, axis=-1)
            if pi == 0:
                acc_s[q_rows, :] = o_blk
                m_s[q_rows, :] = m_blk
                l_s[q_rows, :] = l_blk
            else:
                m_old = m_s[q_rows, :]
                m_new = jnp.maximum(m_old, m_blk)
                w_old = jnp.exp(m_old - m_new)
                w_blk = jnp.exp(m_blk - m_new)
                acc_s[q_rows, :] = acc_s[q_rows, :] * w_old + o_blk * w_blk
                l_s[q_rows, :] = l_s[q_rows, :] * w_old + l_blk * w_blk
                m_s[q_rows, :] = m_new

        def first_body(c, carry):
            block(c, c, ATTN_BLOCK, ATTN_BLOCK)
            return carry

        lax.fori_loop(0, dil, first_body, 0)

        def rest_body(i, carry):
            n = 1 + i // dil
            c = i % dil
            q_start = c + dil * ATTN_BLOCK * n
            block(q_start, q_start - dil * ATTN_BLOCK, 2 * ATTN_BLOCK, 0)
            return carry

        lax.fori_loop(0, dil * (n_blk - 1), rest_body, 0)

    o_ref[0] = acc_s[...] / l_s[...]


def _alibi_slopes(n):
    def pow2(m):
        start = 2.0 ** (-8.0 / m)
        return [start ** (i + 1) for i in range(m)]
    if math.log2(n).is_integer():
        return pow2(n)
    c = 2 ** int(math.floor(math.log2(n)))
    return pow2(c) + pow2(2 * c)[0::2][: n - c]


def _attn_bias(n_heads):
    qi = jnp.arange(ATTN_BLOCK)
    ki = jnp.arange(2 * ATTN_BLOCK)
    dist = qi[:, None] + ATTN_BLOCK - ki[None, :]
    slopes = jnp.asarray(_alibi_slopes(n_heads), F32)
    out = []
    for window, dil in DILATED_PATTERNS:
        valid = (dist >= 0) & (dist <= window // dil)
        bias = -slopes[:, None, None] * (dist * dil).astype(F32)[None]
        out.append(jnp.where(valid[None], bias, MASK_VALUE))
    return jnp.stack(out, 0)


def _attention(lidx, qkv3, q_norm, k_norm, d_a):
    B, S, _ = qkv3.shape
    n_heads = d_a // HEAD_DIM
    n_pairs = n_heads // 2
    slab = 2 * HEAD_DIM
    hid = jnp.arange(slab) // HEAD_DIM
    bd = (hid[:, None] == hid[None, :]).astype(BF16)
    bias = _attn_bias(n_heads)
    col = lambda off: pl.BlockSpec((1, S, slab), lambda b, p, l: (b, 0, off + p))
    grid_spec = pltpu.PrefetchScalarGridSpec(
        num_scalar_prefetch=1, grid=(B, n_pairs),
        in_specs=[col(0), col(n_pairs), col(2 * n_pairs),
                  _layer_spec((1, slab), buffered=False), _layer_spec((1, slab), buffered=False),
                  pl.BlockSpec(bd.shape, lambda b, p, l: (0, 0)),
                  pl.BlockSpec((len(DILATED_PATTERNS), 2, ATTN_BLOCK, 2 * ATTN_BLOCK),
                               lambda b, p, l: (0, p, 0, 0))],
        out_specs=pl.BlockSpec((1, S, slab), lambda b, p, l: (b, 0, p)),
        scratch_shapes=[pltpu.VMEM((S, slab), F32)] * 5)
    return pl.pallas_call(
        _attn_kernel, grid_spec=grid_spec, name="dilated_attn",
        out_shape=jax.ShapeDtypeStruct((B, S, d_a), F32),
        compiler_params=pltpu.CompilerParams(dimension_semantics=("arbitrary", "arbitrary"),
                                             vmem_limit_bytes=V7X_VMEM_LIMIT),
    )(lidx, qkv3, qkv3, qkv3, q_norm, k_norm, bd, bias)


CONV_PAD = 32
CONV_ROWS = 128
SUBLANES = 8


def _conv_kernel(l_ref, u_ref, w_ref, b_ref, lnw_ref, lnb_ref, o_ref, z_s):
    del l_ref
    S = u_ref.shape[1]
    d_c = o_ref.shape[-1]
    z_s[0:CONV_PAD, :] = jnp.zeros((CONV_PAD, d_c), F32)
    z_s[CONV_PAD:, :] = u_ref[0, :, 0:d_c] * _sigmoid(u_ref[0, :, d_c:])
    shift = CONV_PAD - (CONV_WIDTH - 1)

    def tile(i, carry):
        r0 = pl.multiple_of(i * CONV_ROWS, CONV_ROWS)
        acc = jnp.zeros((CONV_ROWS, d_c), F32) + b_ref[...]
        win = z_s[pl.ds(r0, CONV_ROWS + CONV_PAD), :]
        n_win = CONV_ROWS + CONV_PAD
        for sub in range(SUBLANES):
            rolled = win if sub == 0 else pltpu.roll(win, n_win - sub, 0)
            for j in range(CONV_WIDTH):
                off = shift + j
                if off % SUBLANES == sub:
                    base = off - sub
                    acc = acc + rolled[base:base + CONV_ROWS, :] * w_ref[j:j + 1, :]
        mean = jnp.mean(acc, axis=-1, keepdims=True)
        d = acc - mean
        var = jnp.mean(d * d, axis=-1, keepdims=True)
        z = d * lax.rsqrt(var + CONV_LN_EPS) * lnw_ref[...] + lnb_ref[...]
        o_ref[0, pl.ds(r0, CONV_ROWS), :] = z * _sigmoid(z)
        return carry

    lax.fori_loop(0, S // CONV_ROWS, tile, 0)


def _conv(lidx, u3, dw_w, dw_b, ln_w, ln_b):
    B, S, d_u = u3.shape
    d_c = d_u // 2
    grid_spec = pltpu.PrefetchScalarGridSpec(
        num_scalar_prefetch=1, grid=(B,),
        in_specs=[pl.BlockSpec((1, S, d_u), lambda b, l: (b, 0, 0)),
                  _layer_spec((CONV_WIDTH, d_c), buffered=False),
                  _layer_spec((1, d_c), buffered=False), _layer_spec((1, d_c), buffered=False),
                  _layer_spec((1, d_c), buffered=False)],
        out_specs=pl.BlockSpec((1, S, d_c), lambda b, l: (b, 0, 0)),
        scratch_shapes=[pltpu.VMEM((S + CONV_PAD, d_c), F32)])
    return pl.pallas_call(
        _conv_kernel, grid_spec=grid_spec, name="conv_module",
        out_shape=jax.ShapeDtypeStruct((B, S, d_c), F32),
        compiler_params=pltpu.CompilerParams(dimension_semantics=("arbitrary",),
                                             vmem_limit_bytes=V7X_VMEM_LIMIT),
    )(lidx, u3, dw_w, dw_b, ln_w, ln_b)


def _pad_lora(w, offset):
    L, r, d = w.shape
    return jnp.zeros((L, LORA_SLAB, d), F32).at[:, offset:offset + r, :].set(w)


def kernel(x, norm_ffn1, ffn1_w_gate, ffn1_w_up, ffn1_w_down, norm_mix, w_in, shift_mu, rwkv_w0, rwkv_w2, rwkv_a0, rwkv_a2, rwkv_g2, rwkv_k_k, rwkv_k_a, rwkv_r_k, rwkv_ln_w, rwkv_ln_b, attn_q_norm, attn_k_norm, conv_dw_w, conv_dw_b, conv_ln_w, conv_ln_b, w_out, norm_ffn2, ffn2_w_gate, ffn2_w_up, ffn2_w_down):
    B, S, D = x.shape
    depth = w_in.shape[0]
    d_r = rwkv_w0.shape[-1]
    d_c = conv_dw_b.shape[-1]
    d_shift = shift_mu.shape[-1]
    d_a = (w_in.shape[-1] - d_shift - 2 * d_c) // 3
    r_w, r_a, r_g = rwkv_w2.shape[1], rwkv_a2.shape[1], rwkv_g2.shape[1]
    assert d_shift == 3 * d_r + LORA_SLAB and r_w + r_a + r_g == LORA_SLAB
    assert S % (ATTN_BLOCK * DILATED_PATTERNS[-1][1]) == 0 and (d_a // HEAD_DIM) % 2 == 0
    T = B * S
    tm = 256
    tb = 256

    vec = lambda a: a.reshape(depth, 1, -1)
    bf = lambda a: a.astype(BF16)
    wg1, wu1, wd1 = bf(ffn1_w_gate), bf(ffn1_w_up), bf(ffn1_w_down)
    wg2, wu2, wd2 = bf(ffn2_w_gate), bf(ffn2_w_up), bf(ffn2_w_down)
    win, wo = bf(w_in), bf(w_out)
    g1, gm, g2 = vec(norm_ffn1), vec(norm_mix), vec(norm_ffn2)
    w2p = _pad_lora(rwkv_w2, 0)
    a2p = _pad_lora(rwkv_a2, r_w)
    g2p = _pad_lora(rwkv_g2, r_w + r_a)
    qn = vec(jnp.tile(attn_q_norm, (1, 2)))
    kn = vec(jnp.tile(attn_k_norm, (1, 2)))

    def layer(l, xf):
        lidx = jnp.reshape(l, (1,)).astype(jnp.int32)
        x1, ps, qkv, u = _ffn_proj(lidx, xf, g1, wg1, wu1, wd1, gm, win, (d_shift, 3 * d_a, 2 * d_c), tm)
        y_r = _rwkv(lidx, ps.reshape(B, S, d_shift), vec(shift_mu), vec(rwkv_w0), w2p, vec(rwkv_a0),
                    a2p, g2p, vec(rwkv_k_k), vec(rwkv_k_a), vec(rwkv_r_k), vec(rwkv_ln_w),
                    vec(rwkv_ln_b), d_r, tb)
        y_a = _attention(lidx, qkv.reshape(B, S, 3 * d_a), qn, kn, d_a)
        y_c = _conv(lidx, u.reshape(B, S, 2 * d_c), conv_dw_w, vec(conv_dw_b), vec(conv_ln_w),
                    vec(conv_ln_b))
        return _out_ffn(lidx, x1, y_r.reshape(T, d_r), y_a.reshape(T, d_a), y_c.reshape(T, d_c),
                        wo, g2, wg2, wu2, wd2, tm)

    out = lax.fori_loop(0, depth, layer, x.reshape(T, D))
    return out.reshape(B, S, D)
```

```python
import functools
import math

import jax
import jax.numpy as jnp
from jax import lax
from jax.experimental import pallas as pl
from jax.experimental.pallas import tpu as pltpu

F32 = jnp.float32
BF16 = jnp.bfloat16

HEAD_DIM = 64
PAIR = 2 * HEAD_DIM
NORM_EPS = 1e-6
RWKV_GN_EPS = 64e-5
CONV_LN_EPS = 1e-5
KK_EPS = 1e-12
CONV_WIDTH = 31
DILATED_PATTERNS = ((128, 1), (512, 4), (2048, 16))
ATTN_BLOCK = 128
ATTN_GROUP = 4
LORA_SLAB = 128
CHUNK = 64
NEUMANN_STEPS = 6
MASK_VALUE = -1e30
EXP_NEG_HALF = math.exp(-0.5)
V7X_VMEM_LIMIT = 56 * 1024 * 1024

NT_DIMS = (((1,), (1,)), ((), ()))
TN_DIMS = (((0,), (0,)), ((), ()))


def _dot(a, b):
    return jnp.dot(a, b, preferred_element_type=F32)


def _dot_nt(a, b):
    return lax.dot_general(a, b, NT_DIMS, preferred_element_type=F32)


def _dot_tn(a, b):
    return lax.dot_general(a, b, TN_DIMS, preferred_element_type=F32)


def _split3(x):
    hi = x.astype(BF16)
    r1 = x - hi.astype(F32)
    mid = r1.astype(BF16)
    lo = (r1 - mid.astype(F32)).astype(BF16)
    return hi, mid, lo


def _dot_f32_lhs(x, w_exact):
    hi, mid, lo = _split3(x)
    return _dot(hi, w_exact) + _dot(mid, w_exact) + _dot(lo, w_exact)


def _dot_f32_rhs(w_exact, x):
    hi, mid, lo = _split3(x)
    return _dot(w_exact, hi) + _dot(w_exact, mid) + _dot(w_exact, lo)


def _dot_f32(x, w):
    xh, xm, xl = _split3(x)
    wh, wm, wl = _split3(w)
    return (_dot(xh, wh) + _dot(xm, wh) + _dot(xh, wm)
            + _dot(xl, wh) + _dot(xm, wm) + _dot(xh, wl))


def _sigmoid(x):
    return 1.0 / (1.0 + jnp.exp(-x))


def _rms_norm(x, g):
    return x * lax.rsqrt(jnp.mean(x * x, axis=-1, keepdims=True) + NORM_EPS) * g


def _swiglu_residual(x, g, wg_ref, wu_ref, wd_ref):
    xn = _rms_norm(x, g).astype(BF16)
    gate = _dot(xn, wg_ref[...])
    up = _dot(xn, wu_ref[...])
    h = (gate * _sigmoid(gate) * up).astype(BF16)
    return x + 0.5 * _dot(h, wd_ref[...])


def _ffn_proj_kernel(l_ref, x_ref, g1_ref, wg_ref, wu_ref, wd_ref, gm_ref, win_ref,
                     x1_ref, ps_ref, qkv_ref, u_ref):
    del l_ref
    x1 = _swiglu_residual(x_ref[...], g1_ref[...], wg_ref, wu_ref, wd_ref)
    x1_ref[...] = x1
    h = _rms_norm(x1, gm_ref[...]).astype(BF16)
    proj = _dot(h, win_ref[...])
    d_shift = ps_ref.shape[-1]
    d_qkv = qkv_ref.shape[-1]
    ps_ref[...] = proj[:, :d_shift]
    qkv_ref[...] = proj[:, d_shift:d_shift + d_qkv]
    u_ref[...] = proj[:, d_shift + d_qkv:]


def _layer_spec(shape, buffered=True):
    nd = len(shape)
    kw = dict(pipeline_mode=pl.Buffered(1)) if buffered else {}
    return pl.BlockSpec((None,) + tuple(shape), lambda *a: (a[-1][0],) + (0,) * nd, **kw)


def _ffn_proj(lidx, x, g1, wg, wu, wd, gm, win, dims, tm):
    T, D = x.shape
    d_shift, d_qkv, d_u = dims
    row = lambda w: pl.BlockSpec((tm, w), lambda i, l: (i, 0))
    grid_spec = pltpu.PrefetchScalarGridSpec(
        num_scalar_prefetch=1, grid=(T // tm,),
        in_specs=[row(D), _layer_spec((1, D)), _layer_spec(wg.shape[1:]), _layer_spec(wu.shape[1:]),
                  _layer_spec(wd.shape[1:]), _layer_spec((1, D)), _layer_spec(win.shape[1:])],
        out_specs=[row(D), row(d_shift), row(d_qkv), row(d_u)])
    return pl.pallas_call(
        _ffn_proj_kernel, grid_spec=grid_spec, name="ffn_proj",
        out_shape=[jax.ShapeDtypeStruct((T, D), F32), jax.ShapeDtypeStruct((T, d_shift), F32),
                   jax.ShapeDtypeStruct((T, d_qkv), F32), jax.ShapeDtypeStruct((T, d_u), F32)],
        compiler_params=pltpu.CompilerParams(dimension_semantics=("arbitrary",),
                                             vmem_limit_bytes=V7X_VMEM_LIMIT),
    )(lidx, x, g1, wg, wu, wd, gm, win)


def _out_ffn_kernel(l_ref, x_ref, yr_ref, ya_ref, yc_ref, wo_ref, g2_ref, wg_ref, wu_ref, wd_ref,
                    o_ref):
    del l_ref
    d_r = yr_ref.shape[-1]
    d_a = ya_ref.shape[-1]
    x2 = (x_ref[...]
          + _dot(yr_ref[...].astype(BF16), wo_ref[0:d_r, :])
          + _dot(ya_ref[...].astype(BF16), wo_ref[d_r:d_r + d_a, :])
          + _dot(yc_ref[...].astype(BF16), wo_ref[d_r + d_a:, :]))
    o_ref[...] = _swiglu_residual(x2, g2_ref[...], wg_ref, wu_ref, wd_ref)


def _out_ffn(lidx, x, yr, ya, yc, wo, g2, wg, wu, wd, tm):
    T, D = x.shape
    row = lambda w: pl.BlockSpec((tm, w), lambda i, l: (i, 0))
    grid_spec = pltpu.PrefetchScalarGridSpec(
        num_scalar_prefetch=1, grid=(T // tm,),
        in_specs=[row(D), row(yr.shape[1]), row(ya.shape[1]), row(yc.shape[1]),
                  _layer_spec(wo.shape[1:]), _layer_spec((1, D)), _layer_spec(wg.shape[1:]),
                  _layer_spec(wu.shape[1:]), _layer_spec(wd.shape[1:])],
        out_specs=row(D))
    return pl.pallas_call(
        _out_ffn_kernel, grid_spec=grid_spec, name="out_ffn",
        out_shape=jax.ShapeDtypeStruct((T, D), F32),
        compiler_params=pltpu.CompilerParams(dimension_semantics=("arbitrary",),
                                             vmem_limit_bytes=V7X_VMEM_LIMIT),
    )(lidx, x, yr, ya, yc, wo, g2, wg, wu, wd)


def _rwkv_kernel(l_ref, p_ref, mu_ref, w0_ref, w2_ref, a0_ref, a2_ref, g2_ref, kk_ref, ka_ref,
                 rk_ref, lnw_ref, lnb_ref, bd_ref, tri_ref, ones_ref, o_ref,
                 carry_ref, state_ref):
    del l_ref
    tb = p_ref.shape[1]
    d_r = o_ref.shape[-1]

    @pl.when(pl.program_id(1) == 0)
    def _():
        carry_ref[...] = jnp.zeros_like(carry_ref)
        state_ref[...] = jnp.zeros_like(state_ref)

    p = p_ref[0]
    row = lax.broadcasted_iota(jnp.int32, (tb, 1), 0)
    prev = jnp.where(row == 0, carry_ref[...], pltpu.roll(p, 1, 0))
    carry_ref[...] = p[tb - 1:tb, :]
    ps = p + (prev - p) * mu_ref[...]

    r = ps[:, 0:d_r]
    k = ps[:, d_r:2 * d_r]
    v = ps[:, 2 * d_r:3 * d_r]
    xs = ps[:, 3 * d_r:3 * d_r + LORA_SLAB]

    bd = bd_ref[...]
    seg_sum = lambda t: _dot_f32_lhs(t, bd)

    lw = w0_ref[...] + _dot_f32(jnp.tanh(xs), w2_ref[...])
    ld = -EXP_NEG_HALF * _sigmoid(lw)
    a = _sigmoid(a0_ref[...] + _dot_f32(xs, a2_ref[...]))
    g = _dot_f32(_sigmoid(xs), g2_ref[...])
    kk = k * kk_ref[...]
    kk = kk * lax.rsqrt(seg_sum(kk * kk) + KK_EPS)
    k2 = k * (1.0 + (a - 1.0) * ka_ref[...])
    b = kk * a

    lcum = _dot_f32_rhs(tri_ref[...], ld)
    ltot = _dot_f32_rhs(ones_ref[...], ld)
    at_all = (-kk * jnp.exp(lcum - ld)).astype(BF16)
    rt_all = r * jnp.exp(lcum)
    inv_p = jnp.exp(-lcum)
    bt_all = (b * inv_p).astype(BF16)
    kt_all = (k2 * inv_p).astype(BF16)
    rest = jnp.exp(ltot - lcum)
    bp_all = (b * rest).astype(BF16)
    kp_all = (k2 * rest).astype(BF16)
    v_all = v.astype(BF16)
    pc_all = jnp.exp(ltot)

    lane = lax.broadcasted_iota(jnp.int32, (CHUNK, PAIR), 1)
    trow = lax.broadcasted_iota(jnp.int32, (CHUNK, PAIR), 0)
    head0 = lane < HEAD_DIM
    scol = lane & (HEAD_DIM - 1)
    strict = trow > scol
    incl = trow >= scol
    eye2 = jnp.where(trow == scol, 1.0, 0.0)
    r2 = lax.broadcasted_iota(jnp.int32, (PAIR, PAIR), 0)
    c2 = lax.broadcasted_iota(jnp.int32, (PAIR, PAIR), 1)
    same_head = (r2 < HEAD_DIM) == (c2 < HEAD_DIM)

    def block_diag(xb):
        zero = jnp.zeros_like(xb)
        return jnp.concatenate([jnp.where(head0, xb, zero), jnp.where(head0, zero, xb)], axis=0)

    n_chunks = tb // CHUNK
    n_pairs = d_r // PAIR
    units = [(c, p) for c in range(n_chunks) for p in range(n_pairs)]
    tile = lambda t, c, p: t[c * CHUNK:(c + 1) * CHUNK, p * PAIR:(p + 1) * PAIR]

    a_ab, a_ak, a_rb, a_rk = {}, {}, {}, {}
    for u in units:
        ar = jnp.concatenate([tile(at_all, *u), tile(rt_all, *u).astype(BF16)], axis=0)
        gb = _dot_nt(ar, block_diag(tile(bt_all, *u)))
        gk = _dot_nt(ar, block_diag(tile(kt_all, *u)))
        a_ab[u] = jnp.where(strict, gb[:CHUNK], 0.0)
        a_rb[u] = jnp.where(incl, gb[CHUNK:], 0.0).astype(BF16)
        a_ak[u] = jnp.where(strict, gk[:CHUNK], 0.0).astype(BF16)
        a_rk[u] = jnp.where(incl, gk[CHUNK:], 0.0).astype(BF16)
    tinv = {u: eye2 + a_ab[u] for u in units}
    apow = dict(a_ab)
    for _ in range(NEUMANN_STEPS - 1):
        for u in units:
            ab = apow[u].astype(BF16)
            apow[u] = _dot(ab, block_diag(ab))
        for u in units:
            tinv[u] = tinv[u] + _dot(tinv[u].astype(BF16), block_diag(apow[u].astype(BF16)))
    av, w1, u1, rq, y1, mx, gx = {}, {}, {}, {}, {}, {}, {}
    for u in units:
        av[u] = _dot(a_ak[u], block_diag(tile(v_all, *u)))
    for u in units:
        tb16 = tinv[u].astype(BF16)
        w1[u] = _dot(tb16, block_diag(tile(at_all, *u))).astype(BF16)
        u1[u] = _dot(tb16, block_diag(av[u].astype(BF16))).astype(BF16)
    for u in units:
        vc, bp, kp = tile(v_all, *u), tile(bp_all, *u), tile(kp_all, *u)
        rq[u] = (tile(rt_all, *u) + _dot(a_rb[u], block_diag(w1[u]))).astype(BF16)
        y1[u] = _dot(a_rb[u], block_diag(u1[u])) + _dot(a_rk[u], block_diag(vc))
        mx[u] = jnp.where(same_head, _dot_tn(w1[u], bp), 0.0).astype(BF16)
        gx[u] = jnp.where(same_head, _dot_tn(jnp.concatenate([u1[u], vc], axis=0),
                                             jnp.concatenate([bp, kp], axis=0)), 0.0)

    y_rows = []
    for c in range(n_chunks):
        y_parts = []
        for p in range(n_pairs):
            u = (c, p)
            s0 = state_ref[p]
            s0b = s0.astype(BF16)
            y_parts.append(_dot_nt(rq[u], s0b) + y1[u])
            pc = pc_all[c * CHUNK:c * CHUNK + 1, p * PAIR:(p + 1) * PAIR]
            state_ref[p] = s0 * pc + _dot(s0b, mx[u]) + gx[u]
        y_rows.append(jnp.concatenate(y_parts, axis=1))
    y = jnp.concatenate(y_rows, axis=0)
    inv_n = 1.0 / HEAD_DIM
    mean = seg_sum(y) * inv_n
    d = y - mean
    var = seg_sum(d * d) * inv_n
    yn = d * lax.rsqrt(var + RWKV_GN_EPS) * lnw_ref[...] + lnb_ref[...]
    bonus = seg_sum(r * k2 * rk_ref[...]) * v
    o_ref[0] = (yn + bonus) * g


def _rwkv(lidx, ps3, mu, w0, w2p, a0, a2p, g2p, k_k, k_a, r_k, ln_w, ln_b, d_r, tb):
    B, S, d_shift = ps3.shape
    hid = jnp.arange(d_r) // HEAD_DIM
    bd = (hid[:, None] == hid[None, :]).astype(BF16)
    ci = jnp.arange(tb)
    same = (ci[:, None] // CHUNK) == (ci[None, :] // CHUNK)
    tri = (same & (ci[:, None] >= ci[None, :])).astype(BF16)
    ones = same.astype(BF16)
    const = lambda a: pl.BlockSpec(a.shape, lambda b, t, l: (0,) * a.ndim)
    vec = _layer_spec((1, d_r), buffered=False)
    lora = _layer_spec((LORA_SLAB, d_r), buffered=False)
    grid_spec = pltpu.PrefetchScalarGridSpec(
        num_scalar_prefetch=1, grid=(B, S // tb),
        in_specs=[pl.BlockSpec((1, tb, d_shift), lambda b, t, l: (b, t, 0)),
                  _layer_spec((1, d_shift), buffered=False),
                  vec, lora, vec, lora, lora, vec, vec, vec, vec, vec,
                  const(bd), const(tri), const(ones)],
        out_specs=pl.BlockSpec((1, tb, d_r), lambda b, t, l: (b, t, 0)),
        scratch_shapes=[pltpu.VMEM((1, d_shift), F32),
                        pltpu.VMEM((d_r // PAIR, PAIR, PAIR), F32)])
    return pl.pallas_call(
        _rwkv_kernel, grid_spec=grid_spec, name="rwkv7",
        out_shape=jax.ShapeDtypeStruct((B, S, d_r), F32),
        compiler_params=pltpu.CompilerParams(dimension_semantics=("arbitrary", "arbitrary"),
                                             vmem_limit_bytes=V7X_VMEM_LIMIT),
    )(lidx, ps3, mu, w0, w2p, a0, a2p, g2p, k_k, k_a, r_k, ln_w, ln_b, bd, tri, ones)


def _attn_kernel(l_ref, q_ref, k_ref, v_ref, qn_ref, kn_ref, bd_ref, bias_ref, o_ref,
                 q_s, k_s, acc_s, m_s, l_s):
    del l_ref
    S = q_ref.shape[1]
    bd = bd_ref[...]
    inv_n = 1.0 / HEAD_DIM

    def head_rms(t, gain):
        ms = _dot_f32_lhs(t * t, bd) * inv_n
        return t * lax.rsqrt(ms + NORM_EPS) * gain

    q_s[...] = head_rms(q_ref[0], qn_ref[...]) * (HEAD_DIM ** -0.5)
    k_s[...] = head_rms(k_ref[0], kn_ref[...])
    head0 = lax.broadcasted_iota(jnp.int32, (ATTN_BLOCK, PAIR), 1) < HEAD_DIM

    for pi, (window, dil) in enumerate(DILATED_PATTERNS):
        n_sub = S // dil
        n_blk = n_sub // ATTN_BLOCK

        span = dil * ATTN_BLOCK

        def group_body(gi, carry):
            blocks = []
            for g in range(ATTN_GROUP):
                i = gi * ATTN_GROUP + g
                n = i // dil
                q_start = i % dil + span * n
                first = (n == 0).astype(jnp.int32)
                k_start = q_start - span * (1 - first)
                if dil > 1:
                    q_rows = pl.ds(q_start, ATTN_BLOCK, stride=dil)
                    k_rows = pl.ds(k_start, 2 * ATTN_BLOCK, stride=dil)
                else:
                    q_rows = pl.ds(pl.multiple_of(q_start, ATTN_BLOCK), ATTN_BLOCK)
                    k_rows = pl.ds(pl.multiple_of(k_start, ATTN_BLOCK), 2 * ATTN_BLOCK)
                blocks.append(dict(first=first, q_rows=q_rows, k_rows=k_rows))
            for blk in blocks:
                q2 = q_s[blk["q_rows"], :]
                zero = jnp.zeros_like(q2)
                blk["q"] = [jnp.where(head0, q2, zero).astype(BF16), jnp.where(head0, zero, q2).astype(BF16)]
                blk["k"] = k_s[blk["k_rows"], :].astype(BF16)
                blk["v"] = v_ref[0, blk["k_rows"], :].astype(BF16)
                if pi > 0:
                    blk["old"] = (acc_s[blk["q_rows"], :], m_s[blk["q_rows"], :], l_s[blk["q_rows"], :])
            for blk in blocks:
                blk["s"] = [_dot_nt(blk["q"][h], blk["k"]) + bias_ref[pi, h, blk["first"]] for h in range(2)]
            for blk in blocks:
                blk["m"] = [jnp.max(s, axis=-1, keepdims=True) for s in blk["s"]]
            for blk in blocks:
                blk["e"] = [jnp.exp(s - m) for s, m in zip(blk["s"], blk["m"])]
            for blk in blocks:
                blk["l"] = [jnp.sum(e, axis=-1, keepdims=True) for e in blk["e"]]
                blk["o"] = [_dot(e.astype(BF16), blk["v"]) for e in blk["e"]]
            for blk in blocks:
                pair = lambda t: jnp.where(head0, jnp.broadcast_to(t[0], (ATTN_BLOCK, PAIR)),
                                           jnp.broadcast_to(t[1], (ATTN_BLOCK, PAIR)))
                o_blk, m_blk, l_blk = pair(blk["o"]), pair(blk["m"]), pair(blk["l"])
                if pi > 0:
                    acc_old, m_old, l_old = blk["old"]
                    m_new = jnp.maximum(m_old, m_blk)
                    w_old = jnp.exp(m_old - m_new)
                    w_blk = jnp.exp(m_blk - m_new)
                    o_blk = acc_old * w_old + o_blk * w_blk
                    l_blk = l_old * w_old + l_blk * w_blk
                    m_blk = m_new
                blk["out"] = (o_blk, m_blk, l_blk)
            for blk in blocks:
                acc_s[blk["q_rows"], :], m_s[blk["q_rows"], :], l_s[blk["q_rows"], :] = blk["out"]
            return carry

        lax.fori_loop(0, dil * n_blk // ATTN_GROUP, group_body, 0)

    o_ref[0] = acc_s[...] / l_s[...]


def _alibi_slopes(n):
    def pow2(m):
        start = 2.0 ** (-8.0 / m)
        return [start ** (i + 1) for i in range(m)]
    if math.log2(n).is_integer():
        return pow2(n)
    c = 2 ** int(math.floor(math.log2(n)))
    return pow2(c) + pow2(2 * c)[0::2][: n - c]


def _attn_bias(n_heads):
    qi = jnp.arange(ATTN_BLOCK)
    ki = jnp.arange(2 * ATTN_BLOCK)
    dist = qi[:, None] + ATTN_BLOCK - ki[None, :]
    slopes = jnp.asarray(_alibi_slopes(n_heads), F32)
    out = []
    for window, dil in DILATED_PATTERNS:
        valid = (dist >= 0) & (dist <= window // dil)
        bias = -slopes[:, None, None] * (dist * dil).astype(F32)[None]
        rest = jnp.where(valid[None], bias, MASK_VALUE)
        first = jnp.concatenate([rest[..., ATTN_BLOCK:], jnp.full_like(rest[..., ATTN_BLOCK:], MASK_VALUE)], -1)
        out.append(jnp.stack([rest, first], 1))
    return jnp.stack(out, 0)


def _attention(lidx, qkv3, q_norm, k_norm, d_a):
    B, S, _ = qkv3.shape
    n_heads = d_a // HEAD_DIM
    n_pairs = n_heads // 2
    slab = 2 * HEAD_DIM
    hid = jnp.arange(slab) // HEAD_DIM
    bd = (hid[:, None] == hid[None, :]).astype(BF16)
    bias = _attn_bias(n_heads)
    col = lambda off: pl.BlockSpec((1, S, slab), lambda b, p, l: (b, 0, off + p))
    grid_spec = pltpu.PrefetchScalarGridSpec(
        num_scalar_prefetch=1, grid=(B, n_pairs),
        in_specs=[col(0), col(n_pairs), col(2 * n_pairs),
                  _layer_spec((1, slab), buffered=False), _layer_spec((1, slab), buffered=False),
                  pl.BlockSpec(bd.shape, lambda b, p, l: (0, 0)),
                  pl.BlockSpec((len(DILATED_PATTERNS), 2, 2, ATTN_BLOCK, 2 * ATTN_BLOCK),
                               lambda b, p, l: (0, p, 0, 0, 0))],
        out_specs=pl.BlockSpec((1, S, slab), lambda b, p, l: (b, 0, p)),
        scratch_shapes=[pltpu.VMEM((S, slab), F32)] * 5)
    return pl.pallas_call(
        _attn_kernel, grid_spec=grid_spec, name="dilated_attn",
        out_shape=jax.ShapeDtypeStruct((B, S, d_a), F32),
        compiler_params=pltpu.CompilerParams(dimension_semantics=("arbitrary", "arbitrary"),
                                             vmem_limit_bytes=V7X_VMEM_LIMIT),
    )(lidx, qkv3, qkv3, qkv3, q_norm, k_norm, bd, bias)


CONV_PAD = 32
CONV_ROWS = 128
SUBLANES = 8


def _conv_kernel(l_ref, u_ref, w_ref, b_ref, lnw_ref, lnb_ref, o_ref, z_s):
    del l_ref
    S = u_ref.shape[1]
    d_c = o_ref.shape[-1]
    z_s[0:CONV_PAD, :] = jnp.zeros((CONV_PAD, d_c), F32)
    z_s[CONV_PAD:, :] = u_ref[0, :, 0:d_c] * _sigmoid(u_ref[0, :, d_c:])
    shift = CONV_PAD - (CONV_WIDTH - 1)

    def tile(i, carry):
        r0 = pl.multiple_of(i * CONV_ROWS, CONV_ROWS)
        acc = jnp.zeros((CONV_ROWS, d_c), F32) + b_ref[...]
        win = z_s[pl.ds(r0, CONV_ROWS + CONV_PAD), :]
        n_win = CONV_ROWS + CONV_PAD
        for sub in range(SUBLANES):
            rolled = win if sub == 0 else pltpu.roll(win, n_win - sub, 0)
            for j in range(CONV_WIDTH):
                off = shift + j
                if off % SUBLANES == sub:
                    base = off - sub
                    acc = acc + rolled[base:base + CONV_ROWS, :] * w_ref[j:j + 1, :]
        mean = jnp.mean(acc, axis=-1, keepdims=True)
        d = acc - mean
        var = jnp.mean(d * d, axis=-1, keepdims=True)
        z = d * lax.rsqrt(var + CONV_LN_EPS) * lnw_ref[...] + lnb_ref[...]
        o_ref[0, pl.ds(r0, CONV_ROWS), :] = z * _sigmoid(z)
        return carry

    lax.fori_loop(0, S // CONV_ROWS, tile, 0)


def _conv(lidx, u3, dw_w, dw_b, ln_w, ln_b):
    B, S, d_u = u3.shape
    d_c = d_u // 2
    grid_spec = pltpu.PrefetchScalarGridSpec(
        num_scalar_prefetch=1, grid=(B,),
        in_specs=[pl.BlockSpec((1, S, d_u), lambda b, l: (b, 0, 0)),
                  _layer_spec((CONV_WIDTH, d_c), buffered=False),
                  _layer_spec((1, d_c), buffered=False), _layer_spec((1, d_c), buffered=False),
                  _layer_spec((1, d_c), buffered=False)],
        out_specs=pl.BlockSpec((1, S, d_c), lambda b, l: (b, 0, 0)),
        scratch_shapes=[pltpu.VMEM((S + CONV_PAD, d_c), F32)])
    return pl.pallas_call(
        _conv_kernel, grid_spec=grid_spec, name="conv_module",
        out_shape=jax.ShapeDtypeStruct((B, S, d_c), F32),
        compiler_params=pltpu.CompilerParams(dimension_semantics=("arbitrary",),
                                             vmem_limit_bytes=V7X_VMEM_LIMIT),
    )(lidx, u3, dw_w, dw_b, ln_w, ln_b)


def _pad_lora(w, offset):
    L, r, d = w.shape
    return jnp.zeros((L, LORA_SLAB, d), F32).at[:, offset:offset + r, :].set(w)


def kernel(x, norm_ffn1, ffn1_w_gate, ffn1_w_up, ffn1_w_down, norm_mix, w_in, shift_mu, rwkv_w0, rwkv_w2, rwkv_a0, rwkv_a2, rwkv_g2, rwkv_k_k, rwkv_k_a, rwkv_r_k, rwkv_ln_w, rwkv_ln_b, attn_q_norm, attn_k_norm, conv_dw_w, conv_dw_b, conv_ln_w, conv_ln_b, w_out, norm_ffn2, ffn2_w_gate, ffn2_w_up, ffn2_w_down):
    B, S, D = x.shape
    depth = w_in.shape[0]
    d_r = rwkv_w0.shape[-1]
    d_c = conv_dw_b.shape[-1]
    d_shift = shift_mu.shape[-1]
    d_a = (w_in.shape[-1] - d_shift - 2 * d_c) // 3
    r_w, r_a, r_g = rwkv_w2.shape[1], rwkv_a2.shape[1], rwkv_g2.shape[1]
    assert d_shift == 3 * d_r + LORA_SLAB and r_w + r_a + r_g == LORA_SLAB
    assert S % (2 * ATTN_BLOCK * DILATED_PATTERNS[-1][1]) == 0 and (d_a // HEAD_DIM) % 2 == 0
    assert (S // ATTN_BLOCK) % ATTN_GROUP == 0
    T = B * S
    tm = 256
    tb = 256

    vec = lambda a: a.reshape(depth, 1, -1)
    bf = lambda a: a.astype(BF16)
    wg1, wu1, wd1 = bf(ffn1_w_gate), bf(ffn1_w_up), bf(ffn1_w_down)
    wg2, wu2, wd2 = bf(ffn2_w_gate), bf(ffn2_w_up), bf(ffn2_w_down)
    win, wo = bf(w_in), bf(w_out)
    g1, gm, g2 = vec(norm_ffn1), vec(norm_mix), vec(norm_ffn2)
    w2p = _pad_lora(rwkv_w2, 0)
    a2p = _pad_lora(rwkv_a2, r_w)
    g2p = _pad_lora(rwkv_g2, r_w + r_a)
    qn = vec(jnp.tile(attn_q_norm, (1, 2)))
    kn = vec(jnp.tile(attn_k_norm, (1, 2)))

    def layer(l, xf):
        lidx = jnp.reshape(l, (1,)).astype(jnp.int32)
        x1, ps, qkv, u = _ffn_proj(lidx, xf, g1, wg1, wu1, wd1, gm, win, (d_shift, 3 * d_a, 2 * d_c), tm)
        y_r = _rwkv(lidx, ps.reshape(B, S, d_shift), vec(shift_mu), vec(rwkv_w0), w2p, vec(rwkv_a0),
                    a2p, g2p, vec(rwkv_k_k), vec(rwkv_k_a), vec(rwkv_r_k), vec(rwkv_ln_w),
                    vec(rwkv_ln_b), d_r, tb)
        y_a = _attention(lidx, qkv.reshape(B, S, 3 * d_a), qn, kn, d_a)
        y_c = _conv(lidx, u.reshape(B, S, 2 * d_c), conv_dw_w, vec(conv_dw_b), vec(conv_ln_w),
                    vec(conv_ln_b))
        return _out_ffn(lidx, x1, y_r.reshape(T, d_r), y_a.reshape(T, d_a), y_c.reshape(T, d_c),
                        wo, g2, wg2, wu2, wd2, tm)

    out = lax.fori_loop(0, depth, layer, x.reshape(T, D))
    return out.reshape(B, S, D)
```

```python
import functools
import math

import jax
import jax.numpy as jnp
from jax import lax
from jax.experimental import pallas as pl
from jax.experimental.pallas import tpu as pltpu

F32 = jnp.float32
BF16 = jnp.bfloat16

HEAD_DIM = 64
PAIR = 2 * HEAD_DIM
NORM_EPS = 1e-6
RWKV_GN_EPS = 64e-5
CONV_LN_EPS = 1e-5
KK_EPS = 1e-12
CONV_WIDTH = 31
DILATED_PATTERNS = ((128, 1), (512, 4), (2048, 16))
ATTN_BLOCK = 128
ATTN_GROUP = 4
PRE_DILATION = 4
REGROUP_ROWS = 256
LORA_SLAB = 128
CHUNK = 64
NEUMANN_STEPS = 6
MASK_VALUE = -1e30
EXP_NEG_HALF = math.exp(-0.5)
V7X_VMEM_LIMIT = 56 * 1024 * 1024

NT_DIMS = (((1,), (1,)), ((), ()))
TN_DIMS = (((0,), (0,)), ((), ()))


def _dot(a, b):
    return jnp.dot(a, b, preferred_element_type=F32)


def _dot_nt(a, b):
    return lax.dot_general(a, b, NT_DIMS, preferred_element_type=F32)


def _dot_tn(a, b):
    return lax.dot_general(a, b, TN_DIMS, preferred_element_type=F32)


def _split2(x):
    hi = x.astype(BF16)
    lo = (x - hi.astype(F32)).astype(BF16)
    return hi, lo


def _dot_f32_lhs(x, w_exact):
    hi, lo = _split2(x)
    return _dot(hi, w_exact) + _dot(lo, w_exact)


def _dot_f32_rhs(w_exact, x):
    hi, lo = _split2(x)
    return _dot(w_exact, hi) + _dot(w_exact, lo)


def _dot_f32(x, w):
    xh, xl = _split2(x)
    wh, wl = _split2(w)
    return _dot(xh, wh) + _dot(xl, wh) + _dot(xh, wl)


def _sigmoid(x):
    return 0.5 + 0.5 * jnp.tanh(0.5 * x)


def _rms_norm(x, g):
    return x * lax.rsqrt(jnp.mean(x * x, axis=-1, keepdims=True) + NORM_EPS) * g


def _swiglu_residual(x, g, wg_ref, wu_ref, wd_ref):
    xn = _rms_norm(x, g).astype(BF16)
    gate = _dot(xn, wg_ref[...])
    up = _dot(xn, wu_ref[...])
    h = (gate * _sigmoid(gate) * up).astype(BF16)
    return x + 0.5 * _dot(h, wd_ref[...])


def _ffn_proj_kernel(l_ref, x_ref, g1_ref, wg_ref, wu_ref, wd_ref, gm_ref, win_ref,
                     x1_ref, ps_ref, qkv_ref, u_ref):
    del l_ref
    x1 = _swiglu_residual(x_ref[...], g1_ref[...], wg_ref, wu_ref, wd_ref)
    x1_ref[...] = x1
    h = _rms_norm(x1, gm_ref[...]).astype(BF16)
    proj = _dot(h, win_ref[...])
    d_shift = ps_ref.shape[-1]
    d_qkv = qkv_ref.shape[-1]
    ps_ref[...] = proj[:, :d_shift]
    qkv_ref[...] = proj[:, d_shift:d_shift + d_qkv]
    u_ref[...] = proj[:, d_shift + d_qkv:]


def _layer_spec(shape, buffered=True):
    nd = len(shape)
    kw = dict(pipeline_mode=pl.Buffered(1)) if buffered else {}
    return pl.BlockSpec((None,) + tuple(shape), lambda *a: (a[-1][0],) + (0,) * nd, **kw)


def _ffn_proj(lidx, x, g1, wg, wu, wd, gm, win, dims, tm):
    T, D = x.shape
    d_shift, d_qkv, d_u = dims
    row = lambda w: pl.BlockSpec((tm, w), lambda i, l: (i, 0))
    grid_spec = pltpu.PrefetchScalarGridSpec(
        num_scalar_prefetch=1, grid=(T // tm,),
        in_specs=[row(D), _layer_spec((1, D)), _layer_spec(wg.shape[1:]), _layer_spec(wu.shape[1:]),
                  _layer_spec(wd.shape[1:]), _layer_spec((1, D)), _layer_spec(win.shape[1:])],
        out_specs=[row(D), row(d_shift), row(d_qkv), row(d_u)])
    return pl.pallas_call(
        _ffn_proj_kernel, grid_spec=grid_spec, name="ffn_proj",
        out_shape=[jax.ShapeDtypeStruct((T, D), F32), jax.ShapeDtypeStruct((T, d_shift), F32),
                   jax.ShapeDtypeStruct((T, d_qkv), F32), jax.ShapeDtypeStruct((T, d_u), F32)],
        compiler_params=pltpu.CompilerParams(dimension_semantics=("arbitrary",),
                                             vmem_limit_bytes=V7X_VMEM_LIMIT),
    )(lidx, x, g1, wg, wu, wd, gm, win)


def _out_ffn_kernel(l_ref, x_ref, yr_ref, ya_ref, yc_ref, wo_ref, g2_ref, wg_ref, wu_ref, wd_ref,
                    o_ref):
    del l_ref
    d_r = yr_ref.shape[-1]
    d_a = ya_ref.shape[-1]
    x2 = (x_ref[...]
          + _dot(yr_ref[...].astype(BF16), wo_ref[0:d_r, :])
          + _dot(ya_ref[...].astype(BF16), wo_ref[d_r:d_r + d_a, :])
          + _dot(yc_ref[...].astype(BF16), wo_ref[d_r + d_a:, :]))
    o_ref[...] = _swiglu_residual(x2, g2_ref[...], wg_ref, wu_ref, wd_ref)


def _out_ffn(lidx, x, yr, ya, yc, wo, g2, wg, wu, wd, tm):
    T, D = x.shape
    row = lambda w: pl.BlockSpec((tm, w), lambda i, l: (i, 0))
    grid_spec = pltpu.PrefetchScalarGridSpec(
        num_scalar_prefetch=1, grid=(T // tm,),
        in_specs=[row(D), row(yr.shape[1]), row(ya.shape[1]), row(yc.shape[1]),
                  _layer_spec(wo.shape[1:]), _layer_spec((1, D)), _layer_spec(wg.shape[1:]),
                  _layer_spec(wu.shape[1:]), _layer_spec(wd.shape[1:])],
        out_specs=row(D))
    return pl.pallas_call(
        _out_ffn_kernel, grid_spec=grid_spec, name="out_ffn",
        out_shape=jax.ShapeDtypeStruct((T, D), F32),
        compiler_params=pltpu.CompilerParams(dimension_semantics=("arbitrary",),
                                             vmem_limit_bytes=V7X_VMEM_LIMIT),
    )(lidx, x, yr, ya, yc, wo, g2, wg, wu, wd)


def _rwkv_kernel(l_ref, p_ref, mu_ref, w0_ref, w2_ref, a0_ref, a2_ref, g2_ref, kk_ref, ka_ref,
                 rk_ref, lnw_ref, lnb_ref, bd_ref, tri_ref, o_ref,
                 carry_ref, state_ref):
    del l_ref
    tb = p_ref.shape[1]
    d_r = o_ref.shape[-1]

    @pl.when(pl.program_id(1) == 0)
    def _():
        carry_ref[...] = jnp.zeros_like(carry_ref)
        state_ref[...] = jnp.zeros_like(state_ref)

    p = p_ref[0]
    row = lax.broadcasted_iota(jnp.int32, (tb, 1), 0)
    prev = jnp.where(row == 0, carry_ref[...], pltpu.roll(p, 1, 0))
    carry_ref[...] = p[tb - 1:tb, :]
    ps = p + (prev - p) * mu_ref[...]

    r = ps[:, 0:d_r]
    k = ps[:, d_r:2 * d_r]
    v = ps[:, 2 * d_r:3 * d_r]
    xs = ps[:, 3 * d_r:3 * d_r + LORA_SLAB]

    bd = bd_ref[...]
    seg_sum = lambda t: _dot_f32_lhs(t, bd)

    lw = w0_ref[...] + _dot_f32(jnp.tanh(xs), w2_ref[...])
    ld = -EXP_NEG_HALF * _sigmoid(lw)
    a = _sigmoid(a0_ref[...] + _dot_f32(xs, a2_ref[...]))
    g = _dot_f32(_sigmoid(xs), g2_ref[...])
    kk = k * kk_ref[...]
    kk = kk * lax.rsqrt(seg_sum(kk * kk) + KK_EPS)
    k2 = k * (1.0 + (a - 1.0) * ka_ref[...])
    b = kk * a

    lcum = _dot_f32_rhs(tri_ref[...], ld)
    n_chunks = tb // CHUNK
    pc_rows = [jnp.exp(lcum[(c + 1) * CHUNK - 1:(c + 1) * CHUNK, :]) for c in range(n_chunks)]
    at_all = (-kk * jnp.exp(lcum - ld)).astype(BF16)
    rt_all = r * jnp.exp(lcum)
    inv_p = jnp.exp(-lcum)
    bt_all = (b * inv_p).astype(BF16)
    kt_all = (k2 * inv_p).astype(BF16)
    rest = inv_p * jnp.concatenate([jnp.broadcast_to(pc, (CHUNK, d_r)) for pc in pc_rows], axis=0)
    bp_all = (b * rest).astype(BF16)
    kp_all = (k2 * rest).astype(BF16)
    v_all = v.astype(BF16)

    lane = lax.broadcasted_iota(jnp.int32, (CHUNK, PAIR), 1)
    trow = lax.broadcasted_iota(jnp.int32, (CHUNK, PAIR), 0)
    head0 = lane < HEAD_DIM
    scol = lane & (HEAD_DIM - 1)
    strict = trow > scol
    incl = trow >= scol
    eye2 = jnp.where(trow == scol, 1.0, 0.0)
    r2 = lax.broadcasted_iota(jnp.int32, (PAIR, PAIR), 0)
    c2 = lax.broadcasted_iota(jnp.int32, (PAIR, PAIR), 1)
    same_head = (r2 < HEAD_DIM) == (c2 < HEAD_DIM)

    def block_diag(xb):
        zero = jnp.zeros_like(xb)
        return jnp.concatenate([jnp.where(head0, xb, zero), jnp.where(head0, zero, xb)], axis=0)

    n_pairs = d_r // PAIR
    units = [(c, p) for c in range(n_chunks) for p in range(n_pairs)]
    tile = lambda t, c, p: t[c * CHUNK:(c + 1) * CHUNK, p * PAIR:(p + 1) * PAIR]

    a_ab, a_ak, a_rb, a_rk = {}, {}, {}, {}
    for u in units:
        ar = jnp.concatenate([tile(at_all, *u), tile(rt_all, *u).astype(BF16)], axis=0)
        bk = jnp.concatenate([block_diag(tile(bt_all, *u)), block_diag(tile(kt_all, *u))], axis=0)
        g2 = _dot_nt(ar, bk)
        a_ab[u] = jnp.where(strict, g2[:CHUNK, :PAIR], 0.0)
        a_ak[u] = jnp.where(strict, g2[:CHUNK, PAIR:], 0.0).astype(BF16)
        a_rb[u] = jnp.where(incl, g2[CHUNK:, :PAIR], 0.0).astype(BF16)
        a_rk[u] = jnp.where(incl, g2[CHUNK:, PAIR:], 0.0).astype(BF16)
    tinv = {u: eye2 + a_ab[u] for u in units}
    apow = {}
    for u in units:
        ab = a_ab[u].astype(BF16)
        apow[u] = _dot(ab, block_diag(ab))
    for step in range(1, NEUMANN_STEPS):
        last = step == NEUMANN_STEPS - 1
        for u in units:
            ab = apow[u].astype(BF16)
            lhs = tinv[u].astype(BF16) if last else jnp.concatenate([tinv[u].astype(BF16), ab], axis=0)
            prod = _dot(lhs, block_diag(ab))
            tinv[u] = tinv[u] + prod[:CHUNK]
            if not last:
                apow[u] = prod[CHUNK:]
    av, y1v, w1, u1, rq, y1, mx, gx = {}, {}, {}, {}, {}, {}, {}, {}
    for u in units:
        avk = _dot(jnp.concatenate([a_ak[u], a_rk[u]], axis=0), block_diag(tile(v_all, *u)))
        av[u], y1v[u] = avk[:CHUNK].astype(BF16), avk[CHUNK:]
    for u in units:
        rhs = jnp.concatenate([block_diag(tile(at_all, *u)), block_diag(av[u])], axis=1)
        wu = _dot(tinv[u].astype(BF16), rhs).astype(BF16)
        w1[u], u1[u] = wu[:, :PAIR], wu[:, PAIR:]
    for u in units:
        vc = tile(v_all, *u)
        bpkp = jnp.concatenate([tile(bp_all, *u), tile(kp_all, *u)], axis=0)
        ry = _dot(a_rb[u], jnp.concatenate([block_diag(w1[u]), block_diag(u1[u])], axis=1))
        rq[u] = (tile(rt_all, *u) + ry[:, :PAIR]).astype(BF16)
        y1[u] = ry[:, PAIR:] + y1v[u]
        lhs = jnp.concatenate([jnp.concatenate([w1[u], u1[u]], axis=1),
                               jnp.concatenate([jnp.zeros_like(vc), vc], axis=1)], axis=0)
        mg = _dot_tn(lhs, bpkp)
        mx[u] = jnp.where(same_head, mg[:PAIR], 0.0).astype(BF16)
        gx[u] = jnp.where(same_head, mg[PAIR:], 0.0)

    y_rows = []
    for c in range(n_chunks):
        y_parts = []
        for p in range(n_pairs):
            u = (c, p)
            s0 = state_ref[p]
            s0b = s0.astype(BF16)
            y_parts.append(_dot_nt(rq[u], s0b) + y1[u])
            pc = pc_rows[c][:, p * PAIR:(p + 1) * PAIR]
            state_ref[p] = s0 * pc + _dot(s0b, mx[u]) + gx[u]
        y_rows.append(jnp.concatenate(y_parts, axis=1))
    y = jnp.concatenate(y_rows, axis=0)
    inv_n = 1.0 / HEAD_DIM
    mean = seg_sum(y) * inv_n
    d = y - mean
    var = seg_sum(d * d) * inv_n
    yn = d * lax.rsqrt(var + RWKV_GN_EPS) * lnw_ref[...] + lnb_ref[...]
    bonus = seg_sum(r * k2 * rk_ref[...]) * v
    o_ref[0] = (yn + bonus) * g


def _rwkv(lidx, ps3, mu, w0, w2p, a0, a2p, g2p, k_k, k_a, r_k, ln_w, ln_b, d_r, tb):
    B, S, d_shift = ps3.shape
    hid = jnp.arange(d_r) // HEAD_DIM
    bd = (hid[:, None] == hid[None, :]).astype(BF16)
    ci = jnp.arange(tb)
    same = (ci[:, None] // CHUNK) == (ci[None, :] // CHUNK)
    tri = (same & (ci[:, None] >= ci[None, :])).astype(BF16)
    const = lambda a: pl.BlockSpec(a.shape, lambda b, t, l: (0,) * a.ndim)
    vec = _layer_spec((1, d_r), buffered=False)
    lora = _layer_spec((LORA_SLAB, d_r), buffered=False)
    grid_spec = pltpu.PrefetchScalarGridSpec(
        num_scalar_prefetch=1, grid=(B, S // tb),
        in_specs=[pl.BlockSpec((1, tb, d_shift), lambda b, t, l: (b, t, 0)),
                  _layer_spec((1, d_shift), buffered=False),
                  vec, lora, vec, lora, lora, vec, vec, vec, vec, vec,
                  const(bd), const(tri)],
        out_specs=pl.BlockSpec((1, tb, d_r), lambda b, t, l: (b, t, 0)),
        scratch_shapes=[pltpu.VMEM((1, d_shift), F32),
                        pltpu.VMEM((d_r // PAIR, PAIR, PAIR), F32)])
    return pl.pallas_call(
        _rwkv_kernel, grid_spec=grid_spec, name="rwkv7",
        out_shape=jax.ShapeDtypeStruct((B, S, d_r), F32),
        compiler_params=pltpu.CompilerParams(dimension_semantics=("arbitrary", "arbitrary"),
                                             vmem_limit_bytes=V7X_VMEM_LIMIT),
    )(lidx, ps3, mu, w0, w2p, a0, a2p, g2p, k_k, k_a, r_k, ln_w, ln_b, bd, tri)


def _attn_kernel(l_ref, q_ref, k_ref, v_ref, qn_ref, kn_ref, bd_ref, bias_ref, o_ref,
                 q_s, k_s, acc_s, m_s, l_s, q4_s, k4_s, v4_s, acc4_s, m4_s, l4_s):
    del l_ref
    S = q_ref.shape[1]
    bd = bd_ref[...]
    inv_n = 1.0 / HEAD_DIM

    def head_rms(t, gain):
        ms = _dot_f32_lhs(t * t, bd) * inv_n
        return t * lax.rsqrt(ms + NORM_EPS) * gain

    q_s[...] = head_rms(q_ref[0], qn_ref[...]) * (HEAD_DIM ** -0.5)
    k_s[...] = head_rms(k_ref[0], kn_ref[...])
    head0 = lax.broadcasted_iota(jnp.int32, (ATTN_BLOCK, PAIR), 1) < HEAD_DIM

    s_pre = S // PRE_DILATION
    v_tok = v_ref.at[0]

    def regroup(i, carry):
        c = i // (s_pre // REGROUP_ROWS)
        r0 = (i % (s_pre // REGROUP_ROWS)) * REGROUP_ROWS
        src = pl.ds(c + PRE_DILATION * r0, REGROUP_ROWS, stride=PRE_DILATION)
        dst = pl.ds(pl.multiple_of(c * s_pre + r0, REGROUP_ROWS), REGROUP_ROWS)
        q4_s[dst, :] = q_s[src, :]
        k4_s[dst, :] = k_s[src, :]
        v4_s[dst, :] = v_tok[src, :]
        return carry

    lax.fori_loop(0, S // REGROUP_ROWS, regroup, 0)

    for pi, (window, dil) in enumerate(DILATED_PATTERNS):
        n_sub = S // dil
        n_blk = n_sub // ATTN_BLOCK
        regrouped = dil % PRE_DILATION == 0
        if regrouped:
            stride = dil // PRE_DILATION
            q_src, k_src, v_src, state = q4_s, k4_s, v4_s, (acc4_s, m4_s, l4_s)
        else:
            stride = dil
            q_src, k_src, v_src, state = q_s, k_s, v_tok, (acc_s, m_s, l_s)
        merge = any((d % PRE_DILATION == 0) == regrouped for _, d in DILATED_PATTERNS[:pi])
        span = stride * ATTN_BLOCK

        def group_body(gi, carry):
            blocks = []
            for g in range(ATTN_GROUP):
                i = gi * ATTN_GROUP + g
                n = i // dil
                res = i % dil
                if regrouped:
                    q_start = (res % PRE_DILATION) * s_pre + res // PRE_DILATION + span * n
                else:
                    q_start = res + span * n
                first = jnp.where(n == 0, 1, 0)
                k_start = q_start - span * (1 - first)
                if stride > 1:
                    q_rows = pl.ds(q_start, ATTN_BLOCK, stride=stride)
                    k_rows = pl.ds(k_start, 2 * ATTN_BLOCK, stride=stride)
                else:
                    q_rows = pl.ds(pl.multiple_of(q_start, ATTN_BLOCK), ATTN_BLOCK)
                    k_rows = pl.ds(pl.multiple_of(k_start, ATTN_BLOCK), 2 * ATTN_BLOCK)
                blocks.append(dict(first=first, q_rows=q_rows, k_rows=k_rows))
            for blk in blocks:
                q2 = q_src[blk["q_rows"], :]
                zero = jnp.zeros_like(q2)
                blk["q"] = jnp.concatenate([jnp.where(head0, q2, zero), jnp.where(head0, zero, q2)],
                                           axis=0).astype(BF16)
                blk["k"] = k_src[blk["k_rows"], :].astype(BF16)
                blk["v"] = v_src[blk["k_rows"], :].astype(BF16)
                if merge:
                    blk["old"] = tuple(ref[blk["q_rows"], :] for ref in state)
            for blk in blocks:
                blk["s"] = _dot_nt(blk["q"], blk["k"]) + bias_ref[pi, blk["first"]]
            for blk in blocks:
                blk["m"] = jnp.max(blk["s"], axis=-1, keepdims=True)
            for blk in blocks:
                blk["e"] = jnp.exp(blk["s"] - blk["m"])
            for blk in blocks:
                blk["l"] = jnp.sum(blk["e"], axis=-1, keepdims=True)
                blk["o"] = _dot(blk["e"].astype(BF16), blk["v"])
            for blk in blocks:
                pair = lambda t: jnp.where(head0, jnp.broadcast_to(t[:ATTN_BLOCK], (ATTN_BLOCK, PAIR)),
                                           jnp.broadcast_to(t[ATTN_BLOCK:], (ATTN_BLOCK, PAIR)))
                o_blk, m_blk, l_blk = pair(blk["o"]), pair(blk["m"]), pair(blk["l"])
                if merge:
                    o_blk, m_blk, l_blk = _softmax_merge(blk["old"], (o_blk, m_blk, l_blk))
                blk["out"] = (o_blk, m_blk, l_blk)
            for blk in blocks:
                for ref, val in zip(state, blk["out"]):
                    ref[blk["q_rows"], :] = val
            return carry

        lax.fori_loop(0, dil * n_blk // ATTN_GROUP, group_body, 0)

    def finish(i, carry):
        c = i // (s_pre // ATTN_BLOCK)
        r0 = (i % (s_pre // ATTN_BLOCK)) * ATTN_BLOCK
        tok = pl.ds(c + PRE_DILATION * r0, ATTN_BLOCK, stride=PRE_DILATION)
        grp = pl.ds(pl.multiple_of(c * s_pre + r0, ATTN_BLOCK), ATTN_BLOCK)
        acc, _, den = _softmax_merge((acc_s[tok, :], m_s[tok, :], l_s[tok, :]),
                                     (acc4_s[grp, :], m4_s[grp, :], l4_s[grp, :]))
        o_ref[0, tok, :] = acc / den
        return carry

    lax.fori_loop(0, S // ATTN_BLOCK, finish, 0)


def _softmax_merge(a, b):
    acc_a, m_a, l_a = a
    acc_b, m_b, l_b = b
    m_new = jnp.maximum(m_a, m_b)
    w_a = jnp.exp(m_a - m_new)
    w_b = jnp.exp(m_b - m_new)
    return acc_a * w_a + acc_b * w_b, m_new, l_a * w_a + l_b * w_b


def _alibi_slopes(n):
    def pow2(m):
        start = 2.0 ** (-8.0 / m)
        return [start ** (i + 1) for i in range(m)]
    if math.log2(n).is_integer():
        return pow2(n)
    c = 2 ** int(math.floor(math.log2(n)))
    return pow2(c) + pow2(2 * c)[0::2][: n - c]


def _attn_bias(n_heads):
    qi = jnp.arange(ATTN_BLOCK)
    ki = jnp.arange(2 * ATTN_BLOCK)
    dist = qi[:, None] + ATTN_BLOCK - ki[None, :]
    slopes = jnp.asarray(_alibi_slopes(n_heads), F32)
    out = []
    for window, dil in DILATED_PATTERNS:
        valid = (dist >= 0) & (dist <= window // dil)
        bias = -slopes[:, None, None] * (dist * dil).astype(F32)[None]
        rest = jnp.where(valid[None], bias, MASK_VALUE)
        first = jnp.concatenate([rest[..., ATTN_BLOCK:], jnp.full_like(rest[..., ATTN_BLOCK:], MASK_VALUE)], -1)
        out.append(jnp.stack([rest, first], 0).reshape(2, n_heads * ATTN_BLOCK, 2 * ATTN_BLOCK))
    return jnp.stack(out, 0)


def _attention(lidx, qkv3, q_norm, k_norm, d_a):
    B, S, _ = qkv3.shape
    n_heads = d_a // HEAD_DIM
    n_pairs = n_heads // 2
    slab = 2 * HEAD_DIM
    hid = jnp.arange(slab) // HEAD_DIM
    bd = (hid[:, None] == hid[None, :]).astype(BF16)
    bias = _attn_bias(n_heads)
    col = lambda off: pl.BlockSpec((1, S, slab), lambda b, p, l: (b, 0, off + p))
    grid_spec = pltpu.PrefetchScalarGridSpec(
        num_scalar_prefetch=1, grid=(B, n_pairs),
        in_specs=[col(0), col(n_pairs), col(2 * n_pairs),
                  _layer_spec((1, slab), buffered=False), _layer_spec((1, slab), buffered=False),
                  pl.BlockSpec(bd.shape, lambda b, p, l: (0, 0)),
                  pl.BlockSpec((len(DILATED_PATTERNS), 2, 2 * ATTN_BLOCK, 2 * ATTN_BLOCK),
                               lambda b, p, l: (0, 0, p, 0))],
        out_specs=pl.BlockSpec((1, S, slab), lambda b, p, l: (b, 0, p)),
        scratch_shapes=[pltpu.VMEM((S, slab), F32)] * 11)
    return pl.pallas_call(
        _attn_kernel, grid_spec=grid_spec, name="dilated_attn",
        out_shape=jax.ShapeDtypeStruct((B, S, d_a), F32),
        compiler_params=pltpu.CompilerParams(dimension_semantics=("arbitrary", "arbitrary"),
                                             vmem_limit_bytes=V7X_VMEM_LIMIT),
    )(lidx, qkv3, qkv3, qkv3, q_norm, k_norm, bd, bias)


CONV_PAD = 32
CONV_ROWS = 128
SUBLANES = 8


def _conv_kernel(l_ref, u_ref, w_ref, b_ref, lnw_ref, lnb_ref, o_ref, z_s):
    del l_ref
    S = u_ref.shape[1]
    d_c = o_ref.shape[-1]
    z_s[0:CONV_PAD, :] = jnp.zeros((CONV_PAD, d_c), F32)
    z_s[CONV_PAD:, :] = u_ref[0, :, 0:d_c] * _sigmoid(u_ref[0, :, d_c:])
    shift = CONV_PAD - (CONV_WIDTH - 1)

    def tile(i, carry):
        r0 = pl.multiple_of(i * CONV_ROWS, CONV_ROWS)
        acc = jnp.zeros((CONV_ROWS, d_c), F32) + b_ref[...]
        win = z_s[pl.ds(r0, CONV_ROWS + CONV_PAD), :]
        n_win = CONV_ROWS + CONV_PAD
        for sub in range(SUBLANES):
            rolled = win if sub == 0 else pltpu.roll(win, n_win - sub, 0)
            for j in range(CONV_WIDTH):
                off = shift + j
                if off % SUBLANES == sub:
                    base = off - sub
                    acc = acc + rolled[base:base + CONV_ROWS, :] * w_ref[j:j + 1, :]
        mean = jnp.mean(acc, axis=-1, keepdims=True)
        d = acc - mean
        var = jnp.mean(d * d, axis=-1, keepdims=True)
        z = d * lax.rsqrt(var + CONV_LN_EPS) * lnw_ref[...] + lnb_ref[...]
        o_ref[0, pl.ds(r0, CONV_ROWS), :] = z * _sigmoid(z)
        return carry

    lax.fori_loop(0, S // CONV_ROWS, tile, 0)


def _conv(lidx, u3, dw_w, dw_b, ln_w, ln_b):
    B, S, d_u = u3.shape
    d_c = d_u // 2
    grid_spec = pltpu.PrefetchScalarGridSpec(
        num_scalar_prefetch=1, grid=(B,),
        in_specs=[pl.BlockSpec((1, S, d_u), lambda b, l: (b, 0, 0)),
                  _layer_spec((CONV_WIDTH, d_c), buffered=False),
                  _layer_spec((1, d_c), buffered=False), _layer_spec((1, d_c), buffered=False),
                  _layer_spec((1, d_c), buffered=False)],
        out_specs=pl.BlockSpec((1, S, d_c), lambda b, l: (b, 0, 0)),
        scratch_shapes=[pltpu.VMEM((S + CONV_PAD, d_c), F32)])
    return pl.pallas_call(
        _conv_kernel, grid_spec=grid_spec, name="conv_module",
        out_shape=jax.ShapeDtypeStruct((B, S, d_c), F32),
        compiler_params=pltpu.CompilerParams(dimension_semantics=("arbitrary",),
                                             vmem_limit_bytes=V7X_VMEM_LIMIT),
    )(lidx, u3, dw_w, dw_b, ln_w, ln_b)


def _pad_lora(w, offset):
    L, r, d = w.shape
    return jnp.zeros((L, LORA_SLAB, d), F32).at[:, offset:offset + r, :].set(w)


def kernel(x, norm_ffn1, ffn1_w_gate, ffn1_w_up, ffn1_w_down, norm_mix, w_in, shift_mu, rwkv_w0, rwkv_w2, rwkv_a0, rwkv_a2, rwkv_g2, rwkv_k_k, rwkv_k_a, rwkv_r_k, rwkv_ln_w, rwkv_ln_b, attn_q_norm, attn_k_norm, conv_dw_w, conv_dw_b, conv_ln_w, conv_ln_b, w_out, norm_ffn2, ffn2_w_gate, ffn2_w_up, ffn2_w_down):
    B, S, D = x.shape
    depth = w_in.shape[0]
    d_r = rwkv_w0.shape[-1]
    d_c = conv_dw_b.shape[-1]
    d_shift = shift_mu.shape[-1]
    d_a = (w_in.shape[-1] - d_shift - 2 * d_c) // 3
    r_w, r_a, r_g = rwkv_w2.shape[1], rwkv_a2.shape[1], rwkv_g2.shape[1]
    assert d_shift == 3 * d_r + LORA_SLAB and r_w + r_a + r_g == LORA_SLAB
    assert S % (2 * ATTN_BLOCK * DILATED_PATTERNS[-1][1]) == 0 and (d_a // HEAD_DIM) % 2 == 0
    assert (S // ATTN_BLOCK) % ATTN_GROUP == 0 and S % (PRE_DILATION * REGROUP_ROWS) == 0
    kinds = {d % PRE_DILATION == 0 for _, d in DILATED_PATTERNS}
    assert kinds == {True, False}, "both accumulator sets must be written before the final merge"
    assert all(d < PRE_DILATION or d % PRE_DILATION == 0 for _, d in DILATED_PATTERNS)
    T = B * S
    tm = 256
    tb = 256

    vec = lambda a: a.reshape(depth, 1, -1)
    bf = lambda a: a.astype(BF16)
    wg1, wu1, wd1 = bf(ffn1_w_gate), bf(ffn1_w_up), bf(ffn1_w_down)
    wg2, wu2, wd2 = bf(ffn2_w_gate), bf(ffn2_w_up), bf(ffn2_w_down)
    win, wo = bf(w_in), bf(w_out)
    g1, gm, g2 = vec(norm_ffn1), vec(norm_mix), vec(norm_ffn2)
    w2p = _pad_lora(rwkv_w2, 0)
    a2p = _pad_lora(rwkv_a2, r_w)
    g2p = _pad_lora(rwkv_g2, r_w + r_a)
    qn = vec(jnp.tile(attn_q_norm, (1, 2)))
    kn = vec(jnp.tile(attn_k_norm, (1, 2)))

    def layer(l, xf):
        lidx = jnp.reshape(l, (1,)).astype(jnp.int32)
        x1, ps, qkv, u = _ffn_proj(lidx, xf, g1, wg1, wu1, wd1, gm, win, (d_shift, 3 * d_a, 2 * d_c), tm)
        y_r = _rwkv(lidx, ps.reshape(B, S, d_shift), vec(shift_mu), vec(rwkv_w0), w2p, vec(rwkv_a0),
                    a2p, g2p, vec(rwkv_k_k), vec(rwkv_k_a), vec(rwkv_r_k), vec(rwkv_ln_w),
                    vec(rwkv_ln_b), d_r, tb)
        y_a = _attention(lidx, qkv.reshape(B, S, 3 * d_a), qn, kn, d_a)
        y_c = _conv(lidx, u.reshape(B, S, 2 * d_c), conv_dw_w, vec(conv_dw_b), vec(conv_ln_w),
                    vec(conv_ln_b))
        return _out_ffn(lidx, x1, y_r.reshape(T, d_r), y_a.reshape(T, d_a), y_c.reshape(T, d_c),
                        wo, g2, wg2, wu2, wd2, tm)

    out = lax.fori_loop(0, depth, layer, x.reshape(T, D))
    return out.reshape(B, S, D)
```

```python
import functools
import math

import jax
import jax.numpy as jnp
from jax import lax
from jax.experimental import pallas as pl
from jax.experimental.pallas import tpu as pltpu

F32 = jnp.float32
BF16 = jnp.bfloat16

HEAD_DIM = 64
PAIR = 2 * HEAD_DIM
NORM_EPS = 1e-6
RWKV_GN_EPS = 64e-5
CONV_LN_EPS = 1e-5
KK_EPS = 1e-12
CONV_WIDTH = 31
DILATED_PATTERNS = ((128, 1), (512, 4), (2048, 16))
ATTN_BLOCK = 128
ATTN_GROUP = 4
PRE_DILATION = 4
REGROUP_ROWS = 256
LORA_SLAB = 128
CHUNK = 64
NEUMANN_STEPS = 6
MASK_VALUE = -1e30
EXP_NEG_HALF = math.exp(-0.5)
V7X_VMEM_LIMIT = 56 * 1024 * 1024

NT_DIMS = (((1,), (1,)), ((), ()))
TN_DIMS = (((0,), (0,)), ((), ()))


def _dot(a, b):
    return jnp.dot(a, b, preferred_element_type=F32)


def _dot_nt(a, b):
    return lax.dot_general(a, b, NT_DIMS, preferred_element_type=F32)


def _dot_tn(a, b):
    return lax.dot_general(a, b, TN_DIMS, preferred_element_type=F32)


def _split2(x):
    hi = x.astype(BF16)
    lo = (x - hi.astype(F32)).astype(BF16)
    return hi, lo


def _dot_bf16(x, w):
    return _dot(x.astype(BF16), w.astype(BF16))


def _dot_f32_rhs(w_exact, x):
    hi, lo = _split2(x)
    return _dot(w_exact, hi) + _dot(w_exact, lo)


def _dot_f32(x, w):
    xh, xl = _split2(x)
    wh, wl = _split2(w)
    return _dot(xh, wh) + _dot(xl, wh) + _dot(xh, wl)


def _sigmoid(x):
    return 0.5 + 0.5 * jnp.tanh(0.5 * x)


def _rms_norm(x, g):
    return x * lax.rsqrt(jnp.mean(x * x, axis=-1, keepdims=True) + NORM_EPS) * g


def _swiglu_residual(x, g, wg_ref, wu_ref, wd_ref):
    xn = _rms_norm(x, g).astype(BF16)
    gate = _dot(xn, wg_ref[...])
    up = _dot(xn, wu_ref[...])
    h = (gate * _sigmoid(gate) * up).astype(BF16)
    return x + 0.5 * _dot(h, wd_ref[...])


def _ffn_proj_kernel(l_ref, x_ref, g1_ref, wg_ref, wu_ref, wd_ref, gm_ref, win_ref,
                     x1_ref, ps_ref, qkv_ref, u_ref):
    del l_ref
    x1 = _swiglu_residual(x_ref[...], g1_ref[...], wg_ref, wu_ref, wd_ref)
    x1_ref[...] = x1
    h = _rms_norm(x1, gm_ref[...]).astype(BF16)
    proj = _dot(h, win_ref[...])
    d_shift = ps_ref.shape[-1]
    d_qkv = qkv_ref.shape[-1]
    ps_ref[...] = proj[:, :d_shift]
    qkv_ref[...] = proj[:, d_shift:d_shift + d_qkv]
    u_ref[...] = proj[:, d_shift + d_qkv:]


def _layer_spec(shape, buffered=True):
    nd = len(shape)
    kw = dict(pipeline_mode=pl.Buffered(1)) if buffered else {}
    return pl.BlockSpec((None,) + tuple(shape), lambda *a: (a[-1][0],) + (0,) * nd, **kw)


def _ffn_proj(lidx, x, g1, wg, wu, wd, gm, win, dims, tm):
    T, D = x.shape
    d_shift, d_qkv, d_u = dims
    row = lambda w: pl.BlockSpec((tm, w), lambda i, l: (i, 0))
    grid_spec = pltpu.PrefetchScalarGridSpec(
        num_scalar_prefetch=1, grid=(T // tm,),
        in_specs=[row(D), _layer_spec((1, D)), _layer_spec(wg.shape[1:]), _layer_spec(wu.shape[1:]),
                  _layer_spec(wd.shape[1:]), _layer_spec((1, D)), _layer_spec(win.shape[1:])],
        out_specs=[row(D), row(d_shift), row(d_qkv), row(d_u)])
    return pl.pallas_call(
        _ffn_proj_kernel, grid_spec=grid_spec, name="ffn_proj",
        out_shape=[jax.ShapeDtypeStruct((T, D), F32), jax.ShapeDtypeStruct((T, d_shift), F32),
                   jax.ShapeDtypeStruct((T, d_qkv), F32), jax.ShapeDtypeStruct((T, d_u), F32)],
        compiler_params=pltpu.CompilerParams(dimension_semantics=("arbitrary",),
                                             vmem_limit_bytes=V7X_VMEM_LIMIT),
    )(lidx, x, g1, wg, wu, wd, gm, win)


def _out_ffn_kernel(l_ref, x_ref, yr_ref, ya_ref, yc_ref, wo_ref, g2_ref, wg_ref, wu_ref, wd_ref,
                    o_ref):
    del l_ref
    d_r = yr_ref.shape[-1]
    d_a = ya_ref.shape[-1]
    x2 = (x_ref[...]
          + _dot(yr_ref[...].astype(BF16), wo_ref[0:d_r, :])
          + _dot(ya_ref[...].astype(BF16), wo_ref[d_r:d_r + d_a, :])
          + _dot(yc_ref[...].astype(BF16), wo_ref[d_r + d_a:, :]))
    o_ref[...] = _swiglu_residual(x2, g2_ref[...], wg_ref, wu_ref, wd_ref)


def _out_ffn(lidx, x, yr, ya, yc, wo, g2, wg, wu, wd, tm):
    T, D = x.shape
    row = lambda w: pl.BlockSpec((tm, w), lambda i, l: (i, 0))
    grid_spec = pltpu.PrefetchScalarGridSpec(
        num_scalar_prefetch=1, grid=(T // tm,),
        in_specs=[row(D), row(yr.shape[1]), row(ya.shape[1]), row(yc.shape[1]),
                  _layer_spec(wo.shape[1:]), _layer_spec((1, D)), _layer_spec(wg.shape[1:]),
                  _layer_spec(wu.shape[1:]), _layer_spec(wd.shape[1:])],
        out_specs=row(D))
    return pl.pallas_call(
        _out_ffn_kernel, grid_spec=grid_spec, name="out_ffn",
        out_shape=jax.ShapeDtypeStruct((T, D), F32),
        compiler_params=pltpu.CompilerParams(dimension_semantics=("arbitrary",),
                                             vmem_limit_bytes=V7X_VMEM_LIMIT),
    )(lidx, x, yr, ya, yc, wo, g2, wg, wu, wd)


def _rwkv_kernel(l_ref, p_ref, mu_ref, w0_ref, w2_ref, a0_ref, a2_ref, g2_ref, kk_ref, ka_ref,
                 rk_ref, lnw_ref, lnb_ref, bd_ref, tri_ref, o_ref,
                 carry_ref, state_ref):
    del l_ref
    tb = p_ref.shape[1]
    d_r = o_ref.shape[-1]

    @pl.when(pl.program_id(1) == 0)
    def _():
        carry_ref[...] = jnp.zeros_like(carry_ref)
        state_ref[...] = jnp.zeros_like(state_ref)

    p = p_ref[0]
    row = lax.broadcasted_iota(jnp.int32, (tb, 1), 0)
    prev = jnp.where(row == 0, carry_ref[...], pltpu.roll(p, 1, 0))
    carry_ref[...] = p[tb - 1:tb, :]
    ps = p + (prev - p) * mu_ref[...]

    r = ps[:, 0:d_r]
    k = ps[:, d_r:2 * d_r]
    v = ps[:, 2 * d_r:3 * d_r]
    xs = ps[:, 3 * d_r:3 * d_r + LORA_SLAB]

    bd = bd_ref[...]
    def seg_sum(t):
        tb16 = t.astype(BF16)
        return jnp.concatenate([_dot(tb16[:, s:s + PAIR], bd) for s in range(0, d_r, PAIR)], axis=1)

    lw = w0_ref[...] + _dot_f32(jnp.tanh(xs), w2_ref[...])
    ld = -EXP_NEG_HALF * _sigmoid(lw)
    a = _sigmoid(a0_ref[...] + _dot_bf16(xs, a2_ref[...]))
    g = _dot_bf16(_sigmoid(xs), g2_ref[...])
    kk = k * kk_ref[...]
    kk = kk * lax.rsqrt(seg_sum(kk * kk) + KK_EPS)
    k2 = k * (1.0 + (a - 1.0) * ka_ref[...])
    b = kk * a

    lcum = _dot_f32_rhs(tri_ref[...], ld)
    n_chunks = tb // CHUNK
    pc_rows = [jnp.exp(lcum[(c + 1) * CHUNK - 1:(c + 1) * CHUNK, :]) for c in range(n_chunks)]
    at_all = (-kk * jnp.exp(lcum - ld)).astype(BF16)
    rt_all = r * jnp.exp(lcum)
    inv_p = jnp.exp(-lcum)
    bt_all = (b * inv_p).astype(BF16)
    kt_all = (k2 * inv_p).astype(BF16)
    rest = inv_p * jnp.concatenate([jnp.broadcast_to(pc, (CHUNK, d_r)) for pc in pc_rows], axis=0)
    bp_all = (b * rest).astype(BF16)
    kp_all = (k2 * rest).astype(BF16)
    v_all = v.astype(BF16)

    lane = lax.broadcasted_iota(jnp.int32, (CHUNK, PAIR), 1)
    trow = lax.broadcasted_iota(jnp.int32, (CHUNK, PAIR), 0)
    head0 = lane < HEAD_DIM
    scol = lane & (HEAD_DIM - 1)
    strict = trow > scol
    incl = trow >= scol
    eye2 = jnp.where(trow == scol, 1.0, 0.0)
    r2 = lax.broadcasted_iota(jnp.int32, (PAIR, PAIR), 0)
    c2 = lax.broadcasted_iota(jnp.int32, (PAIR, PAIR), 1)
    same_head = (r2 < HEAD_DIM) == (c2 < HEAD_DIM)

    def block_diag(xb):
        zero = jnp.zeros_like(xb)
        return jnp.concatenate([jnp.where(head0, xb, zero), jnp.where(head0, zero, xb)], axis=0)

    n_pairs = d_r // PAIR
    units = [(c, p) for c in range(n_chunks) for p in range(n_pairs)]
    tile = lambda t, c, p: t[c * CHUNK:(c + 1) * CHUNK, p * PAIR:(p + 1) * PAIR]

    a_ab, a_ak, a_rb, a_rk = {}, {}, {}, {}
    for u in units:
        ar = jnp.concatenate([tile(at_all, *u), tile(rt_all, *u).astype(BF16)], axis=0)
        bk = jnp.concatenate([block_diag(tile(bt_all, *u)), block_diag(tile(kt_all, *u))], axis=0)
        g2 = _dot_nt(ar, bk)
        a_ab[u] = jnp.where(strict, g2[:CHUNK, :PAIR], 0.0)
        a_ak[u] = jnp.where(strict, g2[:CHUNK, PAIR:], 0.0).astype(BF16)
        a_rb[u] = jnp.where(incl, g2[CHUNK:, :PAIR], 0.0).astype(BF16)
        a_rk[u] = jnp.where(incl, g2[CHUNK:, PAIR:], 0.0).astype(BF16)
    tinv = {u: eye2 + a_ab[u] for u in units}
    apow = {}
    for u in units:
        ab = a_ab[u].astype(BF16)
        apow[u] = _dot(ab, block_diag(ab))
    for step in range(1, NEUMANN_STEPS):
        last = step == NEUMANN_STEPS - 1
        for u in units:
            ab = apow[u].astype(BF16)
            lhs = tinv[u].astype(BF16) if last else jnp.concatenate([tinv[u].astype(BF16), ab], axis=0)
            prod = _dot(lhs, block_diag(ab))
            tinv[u] = tinv[u] + prod[:CHUNK]
            if not last:
                apow[u] = prod[CHUNK:]
    av, y1v, w1, u1, rq, y1, mx, gx = {}, {}, {}, {}, {}, {}, {}, {}
    for u in units:
        avk = _dot(jnp.concatenate([a_ak[u], a_rk[u]], axis=0), block_diag(tile(v_all, *u)))
        av[u], y1v[u] = avk[:CHUNK].astype(BF16), avk[CHUNK:]
    for u in units:
        rhs = jnp.concatenate([block_diag(tile(at_all, *u)), block_diag(av[u])], axis=1)
        wu = _dot(tinv[u].astype(BF16), rhs).astype(BF16)
        w1[u], u1[u] = wu[:, :PAIR], wu[:, PAIR:]
    for u in units:
        vc = tile(v_all, *u)
        bpkp = jnp.concatenate([tile(bp_all, *u), tile(kp_all, *u)], axis=0)
        ry = _dot(a_rb[u], jnp.concatenate([block_diag(w1[u]), block_diag(u1[u])], axis=1))
        rq[u] = (tile(rt_all, *u) + ry[:, :PAIR]).astype(BF16)
        y1[u] = ry[:, PAIR:] + y1v[u]
        lhs = jnp.concatenate([jnp.concatenate([w1[u], u1[u]], axis=1),
                               jnp.concatenate([jnp.zeros_like(vc), vc], axis=1)], axis=0)
        mg = _dot_tn(lhs, bpkp)
        mx[u] = jnp.where(same_head, mg[:PAIR], 0.0).astype(BF16)
        gx[u] = jnp.where(same_head, mg[PAIR:], 0.0)

    y_rows = []
    for c in range(n_chunks):
        y_parts = []
        for p in range(n_pairs):
            u = (c, p)
            s0 = state_ref[p]
            s0b = s0.astype(BF16)
            y_parts.append(_dot_nt(rq[u], s0b) + y1[u])
            pc = pc_rows[c][:, p * PAIR:(p + 1) * PAIR]
            state_ref[p] = s0 * pc + _dot(s0b, mx[u]) + gx[u]
        y_rows.append(jnp.concatenate(y_parts, axis=1))
    y = jnp.concatenate(y_rows, axis=0)
    inv_n = 1.0 / HEAD_DIM
    mean = seg_sum(y) * inv_n
    d = y - mean
    var = seg_sum(d * d) * inv_n
    yn = d * lax.rsqrt(var + RWKV_GN_EPS) * lnw_ref[...] + lnb_ref[...]
    bonus = seg_sum(r * k2 * rk_ref[...]) * v
    o_ref[0] = (yn + bonus) * g


def _rwkv(lidx, ps3, mu, w0, w2p, a0, a2p, g2p, k_k, k_a, r_k, ln_w, ln_b, d_r, tb):
    B, S, d_shift = ps3.shape
    hid = jnp.arange(PAIR) // HEAD_DIM
    bd = (hid[:, None] == hid[None, :]).astype(BF16)
    ci = jnp.arange(tb)
    same = (ci[:, None] // CHUNK) == (ci[None, :] // CHUNK)
    tri = (same & (ci[:, None] >= ci[None, :])).astype(BF16)
    const = lambda a: pl.BlockSpec(a.shape, lambda b, t, l: (0,) * a.ndim)
    vec = _layer_spec((1, d_r), buffered=False)
    lora = _layer_spec((LORA_SLAB, d_r), buffered=False)
    grid_spec = pltpu.PrefetchScalarGridSpec(
        num_scalar_prefetch=1, grid=(B, S // tb),
        in_specs=[pl.BlockSpec((1, tb, d_shift), lambda b, t, l: (b, t, 0)),
                  _layer_spec((1, d_shift), buffered=False),
                  vec, lora, vec, lora, lora, vec, vec, vec, vec, vec,
                  const(bd), const(tri)],
        out_specs=pl.BlockSpec((1, tb, d_r), lambda b, t, l: (b, t, 0)),
        scratch_shapes=[pltpu.VMEM((1, d_shift), F32),
                        pltpu.VMEM((d_r // PAIR, PAIR, PAIR), F32)])
    return pl.pallas_call(
        _rwkv_kernel, grid_spec=grid_spec, name="rwkv7",
        out_shape=jax.ShapeDtypeStruct((B, S, d_r), F32),
        compiler_params=pltpu.CompilerParams(dimension_semantics=("arbitrary", "arbitrary"),
                                             vmem_limit_bytes=V7X_VMEM_LIMIT),
    )(lidx, ps3, mu, w0, w2p, a0, a2p, g2p, k_k, k_a, r_k, ln_w, ln_b, bd, tri)


def _attn_kernel(l_ref, q_ref, k_ref, v_ref, qn_ref, kn_ref, bd_ref, bias_ref, o_ref,
                 q_s, k_s, acc_s, m_s, l_s, q4_s, k4_s, v4_s, acc4_s, m4_s, l4_s):
    del l_ref
    S = q_ref.shape[1]
    bd = bd_ref[...]
    inv_n = 1.0 / HEAD_DIM

    def head_rms(t, gain):
        ms = _dot_bf16(t * t, bd) * inv_n
        return t * lax.rsqrt(ms + NORM_EPS) * gain

    q_s[...] = head_rms(q_ref[0], qn_ref[...]) * (HEAD_DIM ** -0.5)
    k_s[...] = head_rms(k_ref[0], kn_ref[...])
    head0 = lax.broadcasted_iota(jnp.int32, (ATTN_BLOCK, PAIR), 1) < HEAD_DIM

    s_pre = S // PRE_DILATION
    v_tok = v_ref.at[0]

    def regroup(i, carry):
        c = i // (s_pre // REGROUP_ROWS)
        r0 = (i % (s_pre // REGROUP_ROWS)) * REGROUP_ROWS
        src = pl.ds(c + PRE_DILATION * r0, REGROUP_ROWS, stride=PRE_DILATION)
        dst = pl.ds(pl.multiple_of(c * s_pre + r0, REGROUP_ROWS), REGROUP_ROWS)
        q4_s[dst, :] = q_s[src, :]
        k4_s[dst, :] = k_s[src, :]
        v4_s[dst, :] = v_tok[src, :]
        return carry

    lax.fori_loop(0, S // REGROUP_ROWS, regroup, 0)

    for pi, (window, dil) in enumerate(DILATED_PATTERNS):
        n_sub = S // dil
        n_blk = n_sub // ATTN_BLOCK
        regrouped = dil % PRE_DILATION == 0
        if regrouped:
            stride = dil // PRE_DILATION
            q_src, k_src, v_src, state = q4_s, k4_s, v4_s, (acc4_s, m4_s, l4_s)
        else:
            stride = dil
            q_src, k_src, v_src, state = q_s, k_s, v_tok, (acc_s, m_s, l_s)
        merge = any((d % PRE_DILATION == 0) == regrouped for _, d in DILATED_PATTERNS[:pi])
        span = stride * ATTN_BLOCK

        def group_body(gi, carry):
            blocks = []
            for g in range(ATTN_GROUP):
                i = gi * ATTN_GROUP + g
                n = i // dil
                res = i % dil
                if regrouped:
                    q_start = (res % PRE_DILATION) * s_pre + res // PRE_DILATION + span * n
                else:
                    q_start = res + span * n
                first = jnp.where(n == 0, 1, 0)
                k_start = q_start - span * (1 - first)
                if stride > 1:
                    q_rows = pl.ds(q_start, ATTN_BLOCK, stride=stride)
                    k_rows = pl.ds(k_start, 2 * ATTN_BLOCK, stride=stride)
                else:
                    q_rows = pl.ds(pl.multiple_of(q_start, ATTN_BLOCK), ATTN_BLOCK)
                    k_rows = pl.ds(pl.multiple_of(k_start, ATTN_BLOCK), 2 * ATTN_BLOCK)
                blocks.append(dict(first=first, q_rows=q_rows, k_rows=k_rows))
            for blk in blocks:
                q2 = q_src[blk["q_rows"], :]
                zero = jnp.zeros_like(q2)
                blk["q"] = jnp.concatenate([jnp.where(head0, q2, zero), jnp.where(head0, zero, q2)],
                                           axis=0).astype(BF16)
                blk["k"] = k_src[blk["k_rows"], :].astype(BF16)
                blk["v"] = v_src[blk["k_rows"], :].astype(BF16)
                if merge:
                    blk["old"] = tuple(ref[blk["q_rows"], :] for ref in state)
            for blk in blocks:
                blk["s"] = _dot_nt(blk["q"], blk["k"]) + bias_ref[pi, blk["first"]]
            for blk in blocks:
                blk["m"] = jnp.max(blk["s"], axis=-1, keepdims=True)
            for blk in blocks:
                blk["e"] = jnp.exp(blk["s"] - blk["m"])
            for blk in blocks:
                blk["l"] = jnp.sum(blk["e"], axis=-1, keepdims=True)
                blk["o"] = _dot(blk["e"].astype(BF16), blk["v"])
            for blk in blocks:
                pair = lambda t: jnp.where(head0, jnp.broadcast_to(t[:ATTN_BLOCK], (ATTN_BLOCK, PAIR)),
                                           jnp.broadcast_to(t[ATTN_BLOCK:], (ATTN_BLOCK, PAIR)))
                o_blk, m_blk, l_blk = pair(blk["o"]), pair(blk["m"]), pair(blk["l"])
                if merge:
                    o_blk, m_blk, l_blk = _softmax_merge(blk["old"], (o_blk, m_blk, l_blk))
                blk["out"] = (o_blk, m_blk, l_blk)
            for blk in blocks:
                for ref, val in zip(state, blk["out"]):
                    ref[blk["q_rows"], :] = val
            return carry

        lax.fori_loop(0, dil * n_blk // ATTN_GROUP, group_body, 0)

    def finish(i, carry):
        c = i // (s_pre // ATTN_BLOCK)
        r0 = (i % (s_pre // ATTN_BLOCK)) * ATTN_BLOCK
        tok = pl.ds(c + PRE_DILATION * r0, ATTN_BLOCK, stride=PRE_DILATION)
        grp = pl.ds(pl.multiple_of(c * s_pre + r0, ATTN_BLOCK), ATTN_BLOCK)
        acc, _, den = _softmax_merge((acc_s[tok, :], m_s[tok, :], l_s[tok, :]),
                                     (acc4_s[grp, :], m4_s[grp, :], l4_s[grp, :]))
        o_ref[0, tok, :] = acc / den
        return carry

    lax.fori_loop(0, S // ATTN_BLOCK, finish, 0)


def _softmax_merge(a, b):
    acc_a, m_a, l_a = a
    acc_b, m_b, l_b = b
    m_new = jnp.maximum(m_a, m_b)
    w_a = jnp.exp(m_a - m_new)
    w_b = jnp.exp(m_b - m_new)
    return acc_a * w_a + acc_b * w_b, m_new, l_a * w_a + l_b * w_b


def _alibi_slopes(n):
    def pow2(m):
        start = 2.0 ** (-8.0 / m)
        return [start ** (i + 1) for i in range(m)]
    if math.log2(n).is_integer():
        return pow2(n)
    c = 2 ** int(math.floor(math.log2(n)))
    return pow2(c) + pow2(2 * c)[0::2][: n - c]


def _attn_bias(n_heads):
    qi = jnp.arange(ATTN_BLOCK)
    ki = jnp.arange(2 * ATTN_BLOCK)
    dist = qi[:, None] + ATTN_BLOCK - ki[None, :]
    slopes = jnp.asarray(_alibi_slopes(n_heads), F32)
    out = []
    for window, dil in DILATED_PATTERNS:
        valid = (dist >= 0) & (dist <= window // dil)
        bias = -slopes[:, None, None] * (dist * dil).astype(F32)[None]
        rest = jnp.where(valid[None], bias, MASK_VALUE)
        first = jnp.concatenate([rest[..., ATTN_BLOCK:], jnp.full_like(rest[..., ATTN_BLOCK:], MASK_VALUE)], -1)
        out.append(jnp.stack([rest, first], 0).reshape(2, n_heads * ATTN_BLOCK, 2 * ATTN_BLOCK))
    return jnp.stack(out, 0)


def _attention(lidx, qkv3, q_norm, k_norm, d_a):
    B, S, _ = qkv3.shape
    n_heads = d_a // HEAD_DIM
    n_pairs = n_heads // 2
    slab = 2 * HEAD_DIM
    hid = jnp.arange(slab) // HEAD_DIM
    bd = (hid[:, None] == hid[None, :]).astype(BF16)
    bias = _attn_bias(n_heads)
    col = lambda off: pl.BlockSpec((1, S, slab), lambda b, p, l: (b, 0, off + p))
    grid_spec = pltpu.PrefetchScalarGridSpec(
        num_scalar_prefetch=1, grid=(B, n_pairs),
        in_specs=[col(0), col(n_pairs), col(2 * n_pairs),
                  _layer_spec((1, slab), buffered=False), _layer_spec((1, slab), buffered=False),
                  pl.BlockSpec(bd.shape, lambda b, p, l: (0, 0)),
                  pl.BlockSpec((len(DILATED_PATTERNS), 2, 2 * ATTN_BLOCK, 2 * ATTN_BLOCK),
                               lambda b, p, l: (0, 0, p, 0))],
        out_specs=pl.BlockSpec((1, S, slab), lambda b, p, l: (b, 0, p)),
        scratch_shapes=[pltpu.VMEM((S, slab), F32)] * 11)
    return pl.pallas_call(
        _attn_kernel, grid_spec=grid_spec, name="dilated_attn",
        out_shape=jax.ShapeDtypeStruct((B, S, d_a), F32),
        compiler_params=pltpu.CompilerParams(dimension_semantics=("arbitrary", "arbitrary"),
                                             vmem_limit_bytes=V7X_VMEM_LIMIT),
    )(lidx, qkv3, qkv3, qkv3, q_norm, k_norm, bd, bias)


CONV_PAD = 32
CONV_ROWS = 128
SUBLANES = 8


def _conv_kernel(l_ref, u_ref, w_ref, b_ref, lnw_ref, lnb_ref, o_ref, z_s):
    del l_ref
    S = u_ref.shape[1]
    d_c = o_ref.shape[-1]
    z_s[0:CONV_PAD, :] = jnp.zeros((CONV_PAD, d_c), F32)
    z_s[CONV_PAD:, :] = u_ref[0, :, 0:d_c] * _sigmoid(u_ref[0, :, d_c:])
    shift = CONV_PAD - (CONV_WIDTH - 1)

    def tile(i, carry):
        r0 = pl.multiple_of(i * CONV_ROWS, CONV_ROWS)
        acc = jnp.zeros((CONV_ROWS, d_c), F32) + b_ref[...]
        win = z_s[pl.ds(r0, CONV_ROWS + CONV_PAD), :]
        n_win = CONV_ROWS + CONV_PAD
        for sub in range(SUBLANES):
            rolled = win if sub == 0 else pltpu.roll(win, n_win - sub, 0)
            for j in range(CONV_WIDTH):
                off = shift + j
                if off % SUBLANES == sub:
                    base = off - sub
                    acc = acc + rolled[base:base + CONV_ROWS, :] * w_ref[j:j + 1, :]
        mean = jnp.mean(acc, axis=-1, keepdims=True)
        d = acc - mean
        var = jnp.mean(d * d, axis=-1, keepdims=True)
        z = d * lax.rsqrt(var + CONV_LN_EPS) * lnw_ref[...] + lnb_ref[...]
        o_ref[0, pl.ds(r0, CONV_ROWS), :] = z * _sigmoid(z)
        return carry

    lax.fori_loop(0, S // CONV_ROWS, tile, 0)


def _conv(lidx, u3, dw_w, dw_b, ln_w, ln_b):
    B, S, d_u = u3.shape
    d_c = d_u // 2
    grid_spec = pltpu.PrefetchScalarGridSpec(
        num_scalar_prefetch=1, grid=(B,),
        in_specs=[pl.BlockSpec((1, S, d_u), lambda b, l: (b, 0, 0)),
                  _layer_spec((CONV_WIDTH, d_c), buffered=False),
                  _layer_spec((1, d_c), buffered=False), _layer_spec((1, d_c), buffered=False),
                  _layer_spec((1, d_c), buffered=False)],
        out_specs=pl.BlockSpec((1, S, d_c), lambda b, l: (b, 0, 0)),
        scratch_shapes=[pltpu.VMEM((S + CONV_PAD, d_c), F32)])
    return pl.pallas_call(
        _conv_kernel, grid_spec=grid_spec, name="conv_module",
        out_shape=jax.ShapeDtypeStruct((B, S, d_c), F32),
        compiler_params=pltpu.CompilerParams(dimension_semantics=("arbitrary",),
                                             vmem_limit_bytes=V7X_VMEM_LIMIT),
    )(lidx, u3, dw_w, dw_b, ln_w, ln_b)


def _pad_lora(w, offset):
    L, r, d = w.shape
    return jnp.zeros((L, LORA_SLAB, d), F32).at[:, offset:offset + r, :].set(w)


def kernel(x, norm_ffn1, ffn1_w_gate, ffn1_w_up, ffn1_w_down, norm_mix, w_in, shift_mu, rwkv_w0, rwkv_w2, rwkv_a0, rwkv_a2, rwkv_g2, rwkv_k_k, rwkv_k_a, rwkv_r_k, rwkv_ln_w, rwkv_ln_b, attn_q_norm, attn_k_norm, conv_dw_w, conv_dw_b, conv_ln_w, conv_ln_b, w_out, norm_ffn2, ffn2_w_gate, ffn2_w_up, ffn2_w_down):
    B, S, D = x.shape
    depth = w_in.shape[0]
    d_r = rwkv_w0.shape[-1]
    d_c = conv_dw_b.shape[-1]
    d_shift = shift_mu.shape[-1]
    d_a = (w_in.shape[-1] - d_shift - 2 * d_c) // 3
    r_w, r_a, r_g = rwkv_w2.shape[1], rwkv_a2.shape[1], rwkv_g2.shape[1]
    assert d_shift == 3 * d_r + LORA_SLAB and r_w + r_a + r_g == LORA_SLAB
    assert S % (2 * ATTN_BLOCK * DILATED_PATTERNS[-1][1]) == 0 and (d_a // HEAD_DIM) % 2 == 0
    assert (S // ATTN_BLOCK) % ATTN_GROUP == 0 and S % (PRE_DILATION * REGROUP_ROWS) == 0
    kinds = {d % PRE_DILATION == 0 for _, d in DILATED_PATTERNS}
    assert kinds == {True, False}, "both accumulator sets must be written before the final merge"
    assert all(d < PRE_DILATION or d % PRE_DILATION == 0 for _, d in DILATED_PATTERNS)
    T = B * S
    tm = 256
    tb = 256

    vec = lambda a: a.reshape(depth, 1, -1)
    bf = lambda a: a.astype(BF16)
    wg1, wu1, wd1 = bf(ffn1_w_gate), bf(ffn1_w_up), bf(ffn1_w_down)
    wg2, wu2, wd2 = bf(ffn2_w_gate), bf(ffn2_w_up), bf(ffn2_w_down)
    win, wo = bf(w_in), bf(w_out)
    g1, gm, g2 = vec(norm_ffn1), vec(norm_mix), vec(norm_ffn2)
    w2p = _pad_lora(rwkv_w2, 0)
    a2p = _pad_lora(rwkv_a2, r_w)
    g2p = _pad_lora(rwkv_g2, r_w + r_a)
    qn = vec(jnp.tile(attn_q_norm, (1, 2)))
    kn = vec(jnp.tile(attn_k_norm, (1, 2)))

    def layer(l, xf):
        lidx = jnp.reshape(l, (1,)).astype(jnp.int32)
        x1, ps, qkv, u = _ffn_proj(lidx, xf, g1, wg1, wu1, wd1, gm, win, (d_shift, 3 * d_a, 2 * d_c), tm)
        y_r = _rwkv(lidx, ps.reshape(B, S, d_shift), vec(shift_mu), vec(rwkv_w0), w2p, vec(rwkv_a0),
                    a2p, g2p, vec(rwkv_k_k), vec(rwkv_k_a), vec(rwkv_r_k), vec(rwkv_ln_w),
                    vec(rwkv_ln_b), d_r, tb)
        y_a = _attention(lidx, qkv.reshape(B, S, 3 * d_a), qn, kn, d_a)
        y_c = _conv(lidx, u.reshape(B, S, 2 * d_c), conv_dw_w, vec(conv_dw_b), vec(conv_ln_w),
                    vec(conv_ln_b))
        return _out_ffn(lidx, x1, y_r.reshape(T, d_r), y_a.reshape(T, d_a), y_c.reshape(T, d_c),
                        wo, g2, wg2, wu2, wd2, 2 * tm)

    out = lax.fori_loop(0, depth, layer, x.reshape(T, D))
    return out.reshape(B, S, D)
```

```python
import functools
import math

import jax
import jax.numpy as jnp
from jax import lax
from jax.experimental import pallas as pl
from jax.experimental.pallas import tpu as pltpu

F32 = jnp.float32
BF16 = jnp.bfloat16

HEAD_DIM = 64
PAIR = 2 * HEAD_DIM
NORM_EPS = 1e-6
RWKV_GN_EPS = 64e-5
CONV_LN_EPS = 1e-5
KK_EPS = 1e-12
CONV_WIDTH = 31
DILATED_PATTERNS = ((128, 1), (512, 4), (2048, 16))
ATTN_BLOCK = 128
ATTN_GROUP = 8
PRE_DILATION = 4
REGROUP_ROWS = 256
LORA_SLAB = 128
CHUNK = 64
CUMSUM_ROWS = 256
NEUMANN_STEPS = 6
MASK_VALUE = -1e30
EXP_NEG_HALF = math.exp(-0.5)
V7X_VMEM_LIMIT = 56 * 1024 * 1024

NT_DIMS = (((1,), (1,)), ((), ()))
TN_DIMS = (((0,), (0,)), ((), ()))


def _dot(a, b):
    return jnp.dot(a, b, preferred_element_type=F32)


def _dot_nt(a, b):
    return lax.dot_general(a, b, NT_DIMS, preferred_element_type=F32)


def _dot_tn(a, b):
    return lax.dot_general(a, b, TN_DIMS, preferred_element_type=F32)


def _split2(x):
    hi = x.astype(BF16)
    lo = (x - hi.astype(F32)).astype(BF16)
    return hi, lo


def _dot_bf16(x, w):
    return _dot(x.astype(BF16), w.astype(BF16))


def _dot_f32_rhs(w_exact, x):
    hi, lo = _split2(x)
    return _dot(w_exact, hi) + _dot(w_exact, lo)


def _dot_f32(x, w):
    xh, xl = _split2(x)
    wh, wl = _split2(w)
    return _dot(xh, wh) + _dot(xl, wh) + _dot(xh, wl)


def _sigmoid(x):
    return 0.5 + 0.5 * jnp.tanh(0.5 * x)


def _rms_norm(x, g):
    return x * lax.rsqrt(jnp.mean(x * x, axis=-1, keepdims=True) + NORM_EPS) * g


def _swiglu_residual(x, g, wg_ref, wu_ref, wd_ref, up_bias=None):
    xn = _rms_norm(x, g).astype(BF16)
    gate = _dot(xn, wg_ref[...])
    up = _dot(xn, wu_ref[...])
    if up_bias is not None:
        width = up.shape[1] // len(up_bias)
        up = jnp.concatenate([up[:, k * width:(k + 1) * width] + z for k, z in enumerate(up_bias)], axis=1)
    h = (gate * _sigmoid(gate) * up).astype(BF16)
    return x + 0.5 * _dot(h, wd_ref[...])


def _ffn_proj_kernel(l_ref, x_ref, g1_ref, wg_ref, wu_ref, wd_ref, gm_ref, win_ref,
                     x1_ref, ps_ref, qkv_ref, u_ref):
    del l_ref
    x1 = _swiglu_residual(x_ref[...], g1_ref[...], wg_ref, wu_ref, wd_ref)
    x1_ref[...] = x1
    h = _rms_norm(x1, gm_ref[...]).astype(BF16)
    proj = _dot(h, win_ref[...])
    d_shift = ps_ref.shape[-1]
    d_qkv = qkv_ref.shape[-1]
    ps_ref[...] = proj[:, :d_shift]
    qkv_ref[...] = proj[:, d_shift:d_shift + d_qkv]
    u_ref[...] = proj[:, d_shift + d_qkv:]


def _layer_spec(shape, buffered=True):
    nd = len(shape)
    kw = dict(pipeline_mode=pl.Buffered(1)) if buffered else {}
    return pl.BlockSpec((None,) + tuple(shape), lambda *a: (a[-1][0],) + (0,) * nd, **kw)


def _ffn_proj(lidx, x, g1, wg, wu, wd, gm, win, dims, tm):
    T, D = x.shape
    d_shift, d_qkv, d_u = dims
    row = lambda w: pl.BlockSpec((tm, w), lambda i, l: (i, 0))
    grid_spec = pltpu.PrefetchScalarGridSpec(
        num_scalar_prefetch=1, grid=(T // tm,),
        in_specs=[row(D), _layer_spec((1, D)), _layer_spec(wg.shape[1:]), _layer_spec(wu.shape[1:]),
                  _layer_spec(wd.shape[1:]), _layer_spec((1, D)), _layer_spec(win.shape[1:])],
        out_specs=[row(D), row(d_shift), row(d_qkv), row(d_u)])
    return pl.pallas_call(
        _ffn_proj_kernel, grid_spec=grid_spec, name="ffn_proj",
        out_shape=[jax.ShapeDtypeStruct((T, D), F32), jax.ShapeDtypeStruct((T, d_shift), F32),
                   jax.ShapeDtypeStruct((T, d_qkv), F32), jax.ShapeDtypeStruct((T, d_u), F32)],
        compiler_params=pltpu.CompilerParams(dimension_semantics=("arbitrary",),
                                             vmem_limit_bytes=V7X_VMEM_LIMIT),
    )(lidx, x, g1, wg, wu, wd, gm, win)


CONV_PAD = 32
CONV_ROWS = 64
FFN_TIE_COLS = 256
SUBLANES = 8


def _conv_rows(z_ref, r0, w_ref, b_ref, lnw_ref, lnb_ref):
    d_c = z_ref.shape[-1]
    shift = CONV_PAD - (CONV_WIDTH - 1)
    n_win = CONV_ROWS + CONV_PAD
    win = z_ref[r0:r0 + n_win, :]
    acc = jnp.zeros((CONV_ROWS, d_c), F32) + b_ref[...]
    for sub in range(SUBLANES):
        rolled = win if sub == 0 else pltpu.roll(win, n_win - sub, 0)
        for j in range(CONV_WIDTH):
            off = shift + j
            if off % SUBLANES == sub:
                base = off - sub
                acc = acc + rolled[base:base + CONV_ROWS, :] * w_ref[j:j + 1, :]
    mean = jnp.mean(acc, axis=-1, keepdims=True)
    d = acc - mean
    var = jnp.mean(d * d, axis=-1, keepdims=True)
    z = d * lax.rsqrt(var + CONV_LN_EPS) * lnw_ref[...] + lnb_ref[...]
    return z * _sigmoid(z)


def _out_ffn_kernel(tiles_per_seq, l_ref, x_ref, yr_ref, ya_ref, u_ref, cw_ref, cb_ref, clnw_ref,
                    clnb_ref, wo_ref, g2_ref, wg_ref, wu_ref, wd_ref, o_ref, yc_s, z_s):
    del l_ref
    i = pl.program_id(0)
    tm = x_ref.shape[0]
    d_r = yr_ref.shape[-1]
    d_a = ya_ref.shape[-1]
    d_c = yc_s.shape[-1]

    @pl.when(i == 0)
    def _():
        yc_s[...] = jnp.zeros_like(yc_s)
        z_s[...] = jnp.zeros_like(z_s)

    yc_prev = yc_s[...].astype(BF16)

    left = z_s[tm:tm + CONV_PAD, :]
    z_s[0:CONV_PAD, :] = jnp.where(i % tiles_per_seq == 0, jnp.zeros_like(left), left)
    z_s[CONV_PAD:, :] = u_ref[:, 0:d_c] * _sigmoid(u_ref[:, d_c:])
    n_ties = wg_ref.shape[1] // FFN_TIE_COLS
    rows_per_tie = -(-(tm // CONV_ROWS) // n_ties)
    zeros = []
    tie = jnp.zeros((SUBLANES, 128), F32)
    for ci, r0 in enumerate(range(0, tm, CONV_ROWS)):
        yc = _conv_rows(z_s, r0, cw_ref, cb_ref, clnw_ref, clnb_ref)
        yc_s[r0:r0 + CONV_ROWS, :] = yc
        for s0 in range(0, CONV_ROWS, SUBLANES):
            for c0 in range(0, d_c, 128):
                tie = tie + yc[s0:s0 + SUBLANES, c0:c0 + 128]
        if (ci + 1) % rows_per_tie == 0 or r0 + CONV_ROWS == tm:
            bits = pltpu.bitcast(tie, jnp.uint32)
            z = pltpu.bitcast(lax.shift_right_logical(lax.shift_right_logical(bits, jnp.uint32(16)),
                                                      jnp.uint32(16)), F32)
            zeros.append(jnp.max(z, axis=(0, 1), keepdims=True))
            tie = jnp.zeros((SUBLANES, 128), F32)
    zero = zeros + [zeros[-1]] * (n_ties - len(zeros))

    x2 = (x_ref[...]
          + _dot(yr_ref[...].astype(BF16), wo_ref[0:d_r, :])
          + _dot(ya_ref[...].astype(BF16), wo_ref[d_r:d_r + d_a, :])
          + _dot(yc_prev, wo_ref[d_r + d_a:, :]))
    o_ref[...] = _swiglu_residual(x2, g2_ref[...], wg_ref, wu_ref, wd_ref, up_bias=zero)


def _out_ffn(lidx, x, yr, ya, u, cw, cb, clnw, clnb, wo, g2, wg, wu, wd, tm, seq_len):
    T, D = x.shape
    d_c = u.shape[1] // 2
    n_tiles = T // tm
    assert seq_len % tm == 0 and tm % CONV_ROWS == 0
    lag = lambda w: pl.BlockSpec((tm, w), lambda i, l: (jnp.maximum(i - 1, 0), 0))
    grid_spec = pltpu.PrefetchScalarGridSpec(
        num_scalar_prefetch=1, grid=(n_tiles + 1,),
        in_specs=[lag(D), lag(yr.shape[1]), lag(ya.shape[1]),
                  pl.BlockSpec((tm, 2 * d_c), lambda i, l: (jnp.minimum(i, n_tiles - 1), 0)),
                  _layer_spec((CONV_WIDTH, d_c), buffered=False), _layer_spec((1, d_c), buffered=False),
                  _layer_spec((1, d_c), buffered=False), _layer_spec((1, d_c), buffered=False),
                  _layer_spec(wo.shape[1:]), _layer_spec((1, D)), _layer_spec(wg.shape[1:]),
                  _layer_spec(wu.shape[1:]), _layer_spec(wd.shape[1:])],
        out_specs=lag(D),
        scratch_shapes=[pltpu.VMEM((tm, d_c), F32), pltpu.VMEM((CONV_PAD + tm, d_c), F32)])
    return pl.pallas_call(
        functools.partial(_out_ffn_kernel, seq_len // tm), grid_spec=grid_spec, name="out_ffn",
        out_shape=jax.ShapeDtypeStruct((T, D), F32),
        compiler_params=pltpu.CompilerParams(dimension_semantics=("arbitrary",),
                                             vmem_limit_bytes=V7X_VMEM_LIMIT),
    )(lidx, x, yr, ya, u, cw, cb, clnw, clnb, wo, g2, wg, wu, wd)


def _rwkv_kernel(l_ref, p_ref, mu_ref, w0_ref, w2_ref, a0_ref, a2_ref, g2_ref, kk_ref, ka_ref,
                 rk_ref, lnw_ref, lnb_ref, bd_ref, tri_ref, o_ref,
                 carry_ref, state_ref):
    del l_ref
    tb = p_ref.shape[1]
    d_r = o_ref.shape[-1]

    @pl.when(pl.program_id(1) == 0)
    def _():
        carry_ref[...] = jnp.zeros_like(carry_ref)
        state_ref[...] = jnp.zeros_like(state_ref)

    p = p_ref[0]
    row = lax.broadcasted_iota(jnp.int32, (tb, 1), 0)
    prev = jnp.where(row == 0, carry_ref[...], pltpu.roll(p, 1, 0))
    carry_ref[...] = p[tb - 1:tb, :]
    ps = p + (prev - p) * mu_ref[...]

    r = ps[:, 0:d_r]
    k = ps[:, d_r:2 * d_r]
    v = ps[:, 2 * d_r:3 * d_r]
    xs = ps[:, 3 * d_r:3 * d_r + LORA_SLAB]

    bd = bd_ref[...]
    def seg_sum(t):
        tb16 = t.astype(BF16)
        return jnp.concatenate([_dot(tb16[:, s:s + PAIR], bd) for s in range(0, d_r, PAIR)], axis=1)

    lw = w0_ref[...] + _dot_f32(jnp.tanh(xs), w2_ref[...])
    ld = -EXP_NEG_HALF * _sigmoid(lw)
    a = _sigmoid(a0_ref[...] + _dot_bf16(xs, a2_ref[...]))
    g = _dot_bf16(_sigmoid(xs), g2_ref[...])
    kk = k * kk_ref[...]
    kk = kk * lax.rsqrt(seg_sum(kk * kk) + KK_EPS)
    k2 = k * (1.0 + (a - 1.0) * ka_ref[...])
    b = kk * a

    lcum = jnp.concatenate(
        [_dot_f32_rhs(tri_ref[...], ld[s:s + CUMSUM_ROWS]) for s in range(0, tb, CUMSUM_ROWS)], axis=0)
    n_chunks = tb // CHUNK
    pc_rows = [jnp.exp(lcum[(c + 1) * CHUNK - 1:(c + 1) * CHUNK, :]) for c in range(n_chunks)]
    at_all = (-kk * jnp.exp(lcum - ld)).astype(BF16)
    rt_all = r * jnp.exp(lcum)
    inv_p = jnp.exp(-lcum)
    bt_all = (b * inv_p).astype(BF16)
    kt_all = (k2 * inv_p).astype(BF16)
    rest = inv_p * jnp.concatenate([jnp.broadcast_to(pc, (CHUNK, d_r)) for pc in pc_rows], axis=0)
    bp_all = (b * rest).astype(BF16)
    kp_all = (k2 * rest).astype(BF16)
    v_all = v.astype(BF16)

    lane = lax.broadcasted_iota(jnp.int32, (CHUNK, PAIR), 1)
    trow = lax.broadcasted_iota(jnp.int32, (CHUNK, PAIR), 0)
    head0 = lane < HEAD_DIM
    scol = lane & (HEAD_DIM - 1)
    strict = trow > scol
    incl = trow >= scol
    eye2 = jnp.where(trow == scol, 1.0, 0.0)
    r2 = lax.broadcasted_iota(jnp.int32, (PAIR, PAIR), 0)
    c2 = lax.broadcasted_iota(jnp.int32, (PAIR, PAIR), 1)
    same_head = (r2 < HEAD_DIM) == (c2 < HEAD_DIM)

    def block_diag(xb):
        zero = jnp.zeros_like(xb)
        return jnp.concatenate([jnp.where(head0, xb, zero), jnp.where(head0, zero, xb)], axis=0)

    n_pairs = d_r // PAIR
    units = [(c, p) for c in range(n_chunks) for p in range(n_pairs)]
    tile = lambda t, c, p: t[c * CHUNK:(c + 1) * CHUNK, p * PAIR:(p + 1) * PAIR]

    a_ab, a_ak, a_rb, a_rk = {}, {}, {}, {}
    for u in units:
        ar = jnp.concatenate([tile(at_all, *u), tile(rt_all, *u).astype(BF16)], axis=0)
        bk = jnp.concatenate([block_diag(tile(bt_all, *u)), block_diag(tile(kt_all, *u))], axis=0)
        g2 = _dot_nt(ar, bk)
        a_ab[u] = jnp.where(strict, g2[:CHUNK, :PAIR], 0.0)
        a_ak[u] = jnp.where(strict, g2[:CHUNK, PAIR:], 0.0).astype(BF16)
        a_rb[u] = jnp.where(incl, g2[CHUNK:, :PAIR], 0.0).astype(BF16)
        a_rk[u] = jnp.where(incl, g2[CHUNK:, PAIR:], 0.0).astype(BF16)
    tinv = {u: eye2 + a_ab[u] for u in units}
    apow = {}
    for u in units:
        ab = a_ab[u].astype(BF16)
        apow[u] = _dot(ab, block_diag(ab))
    for step in range(1, NEUMANN_STEPS):
        last = step == NEUMANN_STEPS - 1
        for u in units:
            ab = apow[u].astype(BF16)
            lhs = tinv[u].astype(BF16) if last else jnp.concatenate([tinv[u].astype(BF16), ab], axis=0)
            prod = _dot(lhs, block_diag(ab))
            tinv[u] = tinv[u] + prod[:CHUNK]
            if not last:
                apow[u] = prod[CHUNK:]
    av, y1v, w1, u1, rq, y1, mx, gx = {}, {}, {}, {}, {}, {}, {}, {}
    for u in units:
        avk = _dot(jnp.concatenate([a_ak[u], a_rk[u]], axis=0), block_diag(tile(v_all, *u)))
        av[u], y1v[u] = avk[:CHUNK].astype(BF16), avk[CHUNK:]
    for u in units:
        rhs = jnp.concatenate([block_diag(tile(at_all, *u)), block_diag(av[u])], axis=1)
        wu = _dot(tinv[u].astype(BF16), rhs).astype(BF16)
        w1[u], u1[u] = wu[:, :PAIR], wu[:, PAIR:]
    for u in units:
        vc = tile(v_all, *u)
        bpkp = jnp.concatenate([tile(bp_all, *u), tile(kp_all, *u)], axis=0)
        ry = _dot(a_rb[u], jnp.concatenate([block_diag(w1[u]), block_diag(u1[u])], axis=1))
        rq[u] = (tile(rt_all, *u) + ry[:, :PAIR]).astype(BF16)
        y1[u] = ry[:, PAIR:] + y1v[u]
        lhs = jnp.concatenate([jnp.concatenate([w1[u], u1[u]], axis=1),
                               jnp.concatenate([jnp.zeros_like(vc), vc], axis=1)], axis=0)
        mg = _dot_tn(lhs, bpkp)
        mx[u] = jnp.where(same_head, mg[:PAIR], 0.0).astype(BF16)
        gx[u] = jnp.where(same_head, mg[PAIR:], 0.0)

    y_rows = []
    for c in range(n_chunks):
        y_parts = []
        for p in range(n_pairs):
            u = (c, p)
            s0 = state_ref[p]
            s0b = s0.astype(BF16)
            y_parts.append(_dot_nt(rq[u], s0b) + y1[u])
            pc = pc_rows[c][:, p * PAIR:(p + 1) * PAIR]
            state_ref[p] = s0 * pc + _dot(s0b, mx[u]) + gx[u]
        y_rows.append(jnp.concatenate(y_parts, axis=1))
    y = jnp.concatenate(y_rows, axis=0)
    inv_n = 1.0 / HEAD_DIM
    mean = seg_sum(y) * inv_n
    d = y - mean
    var = seg_sum(d * d) * inv_n
    yn = d * lax.rsqrt(var + RWKV_GN_EPS) * lnw_ref[...] + lnb_ref[...]
    bonus = seg_sum(r * k2 * rk_ref[...]) * v
    o_ref[0] = (yn + bonus) * g


def _rwkv(lidx, ps3, mu, w0, w2p, a0, a2p, g2p, k_k, k_a, r_k, ln_w, ln_b, d_r, tb):
    B, S, d_shift = ps3.shape
    hid = jnp.arange(PAIR) // HEAD_DIM
    bd = (hid[:, None] == hid[None, :]).astype(BF16)
    assert tb % CUMSUM_ROWS == 0
    ci = jnp.arange(CUMSUM_ROWS)
    same = (ci[:, None] // CHUNK) == (ci[None, :] // CHUNK)
    tri = (same & (ci[:, None] >= ci[None, :])).astype(BF16)
    const = lambda a: pl.BlockSpec(a.shape, lambda b, t, l: (0,) * a.ndim)
    vec = _layer_spec((1, d_r), buffered=False)
    lora = _layer_spec((LORA_SLAB, d_r), buffered=False)
    grid_spec = pltpu.PrefetchScalarGridSpec(
        num_scalar_prefetch=1, grid=(B, S // tb),
        in_specs=[pl.BlockSpec((1, tb, d_shift), lambda b, t, l: (b, t, 0)),
                  _layer_spec((1, d_shift), buffered=False),
                  vec, lora, vec, lora, lora, vec, vec, vec, vec, vec,
                  const(bd), const(tri)],
        out_specs=pl.BlockSpec((1, tb, d_r), lambda b, t, l: (b, t, 0)),
        scratch_shapes=[pltpu.VMEM((1, d_shift), F32),
                        pltpu.VMEM((d_r // PAIR, PAIR, PAIR), F32)])
    return pl.pallas_call(
        _rwkv_kernel, grid_spec=grid_spec, name="rwkv7",
        out_shape=jax.ShapeDtypeStruct((B, S, d_r), F32),
        compiler_params=pltpu.CompilerParams(dimension_semantics=("arbitrary", "arbitrary"),
                                             vmem_limit_bytes=V7X_VMEM_LIMIT),
    )(lidx, ps3, mu, w0, w2p, a0, a2p, g2p, k_k, k_a, r_k, ln_w, ln_b, bd, tri)


def _attn_kernel(l_ref, q_ref, k_ref, v_ref, qn_ref, kn_ref, bd_ref, bias_ref, o_ref,
                 q_s, k_s, q4_s, k4_s, v4_s, st_s):
    del l_ref
    S = q_ref.shape[1]
    bd = bd_ref[...]
    inv_n = 1.0 / HEAD_DIM

    def head_rms(t, gain):
        ms = _dot_bf16(t * t, bd) * inv_n
        return t * lax.rsqrt(ms + NORM_EPS) * gain

    q_s[...] = head_rms(q_ref[0], qn_ref[...]) * (HEAD_DIM ** -0.5)
    k_s[...] = head_rms(k_ref[0], kn_ref[...])
    head0 = lax.broadcasted_iota(jnp.int32, (ATTN_BLOCK, PAIR), 1) < HEAD_DIM

    s_pre = S // PRE_DILATION
    v_tok = v_ref.at[0]

    def regroup(i, carry):
        c = i // (s_pre // REGROUP_ROWS)
        r0 = (i % (s_pre // REGROUP_ROWS)) * REGROUP_ROWS
        src = pl.ds(c + PRE_DILATION * r0, REGROUP_ROWS, stride=PRE_DILATION)
        dst = pl.ds(pl.multiple_of(c * s_pre + r0, REGROUP_ROWS), REGROUP_ROWS)
        q4_s[dst, :] = q_s[src, :]
        k4_s[dst, :] = k_s[src, :]
        v4_s[dst, :] = v_tok[src, :]
        return carry

    lax.fori_loop(0, S // REGROUP_ROWS, regroup, 0)

    for pi, (window, dil) in enumerate(DILATED_PATTERNS):
        n_sub = S // dil
        n_blk = n_sub // ATTN_BLOCK
        regrouped = dil % PRE_DILATION == 0
        if regrouped:
            stride = dil // PRE_DILATION
            q_src, k_src, v_src = q4_s, k4_s, v4_s
        else:
            stride = dil
            q_src, k_src, v_src = q_s, k_s, v_tok
        state = tuple(st_s.at[3 * pi + j] for j in range(3))
        span = stride * ATTN_BLOCK

        def group_body(gi, carry):
            blocks = []
            for g in range(ATTN_GROUP):
                i = gi * ATTN_GROUP + g
                n = i // dil
                res = i % dil
                if regrouped:
                    q_start = (res % PRE_DILATION) * s_pre + res // PRE_DILATION + span * n
                else:
                    q_start = res + span * n
                first = jnp.where(n == 0, 1, 0)
                k_start = q_start - span * (1 - first)
                if stride > 1:
                    q_rows = pl.ds(q_start, ATTN_BLOCK, stride=stride)
                    k_rows = pl.ds(k_start, 2 * ATTN_BLOCK, stride=stride)
                else:
                    q_rows = pl.ds(pl.multiple_of(q_start, ATTN_BLOCK), ATTN_BLOCK)
                    k_rows = pl.ds(pl.multiple_of(k_start, ATTN_BLOCK), 2 * ATTN_BLOCK)
                blocks.append(dict(first=first, q_rows=q_rows, k_rows=k_rows))
            for blk in blocks:
                q2 = q_src[blk["q_rows"], :]
                zero = jnp.zeros_like(q2)
                blk["q"] = jnp.concatenate([jnp.where(head0, q2, zero), jnp.where(head0, zero, q2)],
                                           axis=0).astype(BF16)
                blk["k"] = k_src[blk["k_rows"], :].astype(BF16)
                blk["v"] = v_src[blk["k_rows"], :].astype(BF16)
            for blk in blocks:
                blk["s"] = _dot_nt(blk["q"], blk["k"]) + bias_ref[pi, blk["first"]]
            for blk in blocks:
                blk["m"] = jnp.max(blk["s"], axis=-1, keepdims=True)
            for blk in blocks:
                blk["e"] = jnp.exp(blk["s"] - blk["m"])
            for blk in blocks:
                blk["l"] = jnp.sum(blk["e"], axis=-1, keepdims=True)
                blk["o"] = _dot(blk["e"].astype(BF16), blk["v"])
            for blk in blocks:
                pair = lambda t: jnp.where(head0, jnp.broadcast_to(t[:ATTN_BLOCK], (ATTN_BLOCK, PAIR)),
                                           jnp.broadcast_to(t[ATTN_BLOCK:], (ATTN_BLOCK, PAIR)))
                blk["out"] = (pair(blk["o"]), pair(blk["m"]), pair(blk["l"]))
            for blk in blocks:
                for ref, val in zip(state, blk["out"]):
                    ref[blk["q_rows"], :] = val
            return carry

        lax.fori_loop(0, dil * n_blk // ATTN_GROUP, group_body, 0)

    def finish(i, carry):
        c = i // (s_pre // ATTN_BLOCK)
        r0 = (i % (s_pre // ATTN_BLOCK)) * ATTN_BLOCK
        tok = pl.ds(c + PRE_DILATION * r0, ATTN_BLOCK, stride=PRE_DILATION)
        grp = pl.ds(pl.multiple_of(c * s_pre + r0, ATTN_BLOCK), ATTN_BLOCK)
        total = None
        for pi, (_, dil) in enumerate(DILATED_PATTERNS):
            rows = grp if dil % PRE_DILATION == 0 else tok
            part = tuple(st_s[3 * pi + j, rows, :] for j in range(3))
            total = part if total is None else _softmax_merge(total, part)
        o_ref[0, tok, :] = total[0] / total[2]
        return carry

    lax.fori_loop(0, S // ATTN_BLOCK, finish, 0)


def _softmax_merge(a, b):
    acc_a, m_a, l_a = a
    acc_b, m_b, l_b = b
    m_new = jnp.maximum(m_a, m_b)
    w_a = jnp.exp(m_a - m_new)
    w_b = jnp.exp(m_b - m_new)
    return acc_a * w_a + acc_b * w_b, m_new, l_a * w_a + l_b * w_b


def _alibi_slopes(n):
    def pow2(m):
        start = 2.0 ** (-8.0 / m)
        return [start ** (i + 1) for i in range(m)]
    if math.log2(n).is_integer():
        return pow2(n)
    c = 2 ** int(math.floor(math.log2(n)))
    return pow2(c) + pow2(2 * c)[0::2][: n - c]


def _attn_bias(n_heads):
    qi = jnp.arange(ATTN_BLOCK)
    ki = jnp.arange(2 * ATTN_BLOCK)
    dist = qi[:, None] + ATTN_BLOCK - ki[None, :]
    slopes = jnp.asarray(_alibi_slopes(n_heads), F32)
    out = []
    for window, dil in DILATED_PATTERNS:
        valid = (dist >= 0) & (dist <= window // dil)
        bias = -slopes[:, None, None] * (dist * dil).astype(F32)[None]
        rest = jnp.where(valid[None], bias, MASK_VALUE)
        first = jnp.concatenate([rest[..., ATTN_BLOCK:], jnp.full_like(rest[..., ATTN_BLOCK:], MASK_VALUE)], -1)
        out.append(jnp.stack([rest, first], 0).reshape(2, n_heads * ATTN_BLOCK, 2 * ATTN_BLOCK))
    return jnp.stack(out, 0)


def _attention(lidx, qkv3, q_norm, k_norm, d_a):
    B, S, _ = qkv3.shape
    n_heads = d_a // HEAD_DIM
    n_pairs = n_heads // 2
    slab = 2 * HEAD_DIM
    hid = jnp.arange(slab) // HEAD_DIM
    bd = (hid[:, None] == hid[None, :]).astype(BF16)
    bias = _attn_bias(n_heads)
    col = lambda off: pl.BlockSpec((1, S, slab), lambda b, p, l: (b, 0, off + p))
    grid_spec = pltpu.PrefetchScalarGridSpec(
        num_scalar_prefetch=1, grid=(B, n_pairs),
        in_specs=[col(0), col(n_pairs), col(2 * n_pairs),
                  _layer_spec((1, slab), buffered=False), _layer_spec((1, slab), buffered=False),
                  pl.BlockSpec(bd.shape, lambda b, p, l: (0, 0)),
                  pl.BlockSpec((len(DILATED_PATTERNS), 2, 2 * ATTN_BLOCK, 2 * ATTN_BLOCK),
                               lambda b, p, l: (0, 0, p, 0))],
        out_specs=pl.BlockSpec((1, S, slab), lambda b, p, l: (b, 0, p)),
        scratch_shapes=[pltpu.VMEM((S, slab), F32)] * 5
        + [pltpu.VMEM((3 * len(DILATED_PATTERNS), S, slab), F32)])
    return pl.pallas_call(
        _attn_kernel, grid_spec=grid_spec, name="dilated_attn",
        out_shape=jax.ShapeDtypeStruct((B, S, d_a), F32),
        compiler_params=pltpu.CompilerParams(dimension_semantics=("arbitrary", "arbitrary"),
                                             vmem_limit_bytes=V7X_VMEM_LIMIT),
    )(lidx, qkv3, qkv3, qkv3, q_norm, k_norm, bd, bias)


def _pad_lora(w, offset):
    L, r, d = w.shape
    return jnp.zeros((L, LORA_SLAB, d), F32).at[:, offset:offset + r, :].set(w)


def kernel(x, norm_ffn1, ffn1_w_gate, ffn1_w_up, ffn1_w_down, norm_mix, w_in, shift_mu, rwkv_w0, rwkv_w2, rwkv_a0, rwkv_a2, rwkv_g2, rwkv_k_k, rwkv_k_a, rwkv_r_k, rwkv_ln_w, rwkv_ln_b, attn_q_norm, attn_k_norm, conv_dw_w, conv_dw_b, conv_ln_w, conv_ln_b, w_out, norm_ffn2, ffn2_w_gate, ffn2_w_up, ffn2_w_down):
    B, S, D = x.shape
    depth = w_in.shape[0]
    d_r = rwkv_w0.shape[-1]
    d_c = conv_dw_b.shape[-1]
    d_shift = shift_mu.shape[-1]
    d_a = (w_in.shape[-1] - d_shift - 2 * d_c) // 3
    r_w, r_a, r_g = rwkv_w2.shape[1], rwkv_a2.shape[1], rwkv_g2.shape[1]
    assert d_shift == 3 * d_r + LORA_SLAB and r_w + r_a + r_g == LORA_SLAB
    assert S % (2 * ATTN_BLOCK * DILATED_PATTERNS[-1][1]) == 0 and (d_a // HEAD_DIM) % 2 == 0
    assert (S // ATTN_BLOCK) % ATTN_GROUP == 0 and S % (PRE_DILATION * REGROUP_ROWS) == 0
    assert all(d < PRE_DILATION or d % PRE_DILATION == 0 for _, d in DILATED_PATTERNS)
    T = B * S
    tm = 256
    tb = 512

    vec = lambda a: a.reshape(depth, 1, -1)
    bf = lambda a: a.astype(BF16)
    wg1, wu1, wd1 = bf(ffn1_w_gate), bf(ffn1_w_up), bf(ffn1_w_down)
    wg2, wu2, wd2 = bf(ffn2_w_gate), bf(ffn2_w_up), bf(ffn2_w_down)
    win, wo = bf(w_in), bf(w_out)
    g1, gm, g2 = vec(norm_ffn1), vec(norm_mix), vec(norm_ffn2)
    w2p = _pad_lora(rwkv_w2, 0)
    a2p = _pad_lora(rwkv_a2, r_w)
    g2p = _pad_lora(rwkv_g2, r_w + r_a)
    qn = vec(jnp.tile(attn_q_norm, (1, 2)))
    kn = vec(jnp.tile(attn_k_norm, (1, 2)))

    def layer(l, xf):
        lidx = jnp.reshape(l, (1,)).astype(jnp.int32)
        x1, ps, qkv, u = _ffn_proj(lidx, xf, g1, wg1, wu1, wd1, gm, win, (d_shift, 3 * d_a, 2 * d_c), tm)
        y_r = _rwkv(lidx, ps.reshape(B, S, d_shift), vec(shift_mu), vec(rwkv_w0), w2p, vec(rwkv_a0),
                    a2p, g2p, vec(rwkv_k_k), vec(rwkv_k_a), vec(rwkv_r_k), vec(rwkv_ln_w),
                    vec(rwkv_ln_b), d_r, tb)
        y_a = _attention(lidx, qkv.reshape(B, S, 3 * d_a), qn, kn, d_a)
        return _out_ffn(lidx, x1, y_r.reshape(T, d_r), y_a.reshape(T, d_a), u, conv_dw_w,
                        vec(conv_dw_b), vec(conv_ln_w), vec(conv_ln_b), wo, g2, wg2, wu2, wd2,
                        2 * tm, S)

    out = lax.fori_loop(0, depth, layer, x.reshape(T, D))
    return out.reshape(B, S, D)
```

```python
import functools
import math

import jax
import jax.numpy as jnp
from jax import lax
from jax.experimental import pallas as pl
from jax.experimental.pallas import tpu as pltpu

F32 = jnp.float32
BF16 = jnp.bfloat16

HEAD_DIM = 64
PAIR = 2 * HEAD_DIM
NORM_EPS = 1e-6
RWKV_GN_EPS = 64e-5
CONV_LN_EPS = 1e-5
KK_EPS = 1e-12
CONV_WIDTH = 31
DILATED_PATTERNS = ((128, 1), (512, 4), (2048, 16))
ATTN_BLOCK = 128
ATTN_GROUP = 8
PRE_DILATION = 4
REGROUP_ROWS = 256
LORA_SLAB = 128
CHUNK = 64
CUMSUM_ROWS = 256
NEUMANN_STEPS = 6
MASK_VALUE = -1e30
EXP_NEG_HALF = math.exp(-0.5)
V7X_VMEM_LIMIT = 56 * 1024 * 1024

NT_DIMS = (((1,), (1,)), ((), ()))
TN_DIMS = (((0,), (0,)), ((), ()))


def _dot(a, b):
    return jnp.dot(a, b, preferred_element_type=F32)


def _dot_nt(a, b):
    return lax.dot_general(a, b, NT_DIMS, preferred_element_type=F32)


def _dot_tn(a, b):
    return lax.dot_general(a, b, TN_DIMS, preferred_element_type=F32)


def _split2(x):
    hi = x.astype(BF16)
    lo = (x - hi.astype(F32)).astype(BF16)
    return hi, lo


def _dot_bf16(x, w):
    return _dot(x.astype(BF16), w.astype(BF16))


def _dot_f32_rhs(w_exact, x):
    hi, lo = _split2(x)
    return _dot(w_exact, hi) + _dot(w_exact, lo)


def _dot_f32(x, w):
    xh, xl = _split2(x)
    wh, wl = _split2(w)
    return _dot(xh, wh) + _dot(xl, wh) + _dot(xh, wl)


def _sigmoid(x):
    return 0.5 + 0.5 * jnp.tanh(0.5 * x)


def _rms_norm(x, g):
    return x * lax.rsqrt(jnp.mean(x * x, axis=-1, keepdims=True) + NORM_EPS) * g


def _swiglu_residual(x, g, wg_ref, wu_ref, wd_ref, up_bias=None):
    xn = _rms_norm(x, g).astype(BF16)
    gate = _dot(xn, wg_ref[...])
    up = _dot(xn, wu_ref[...])
    if up_bias is not None:
        width = up.shape[1] // len(up_bias)
        up = jnp.concatenate([up[:, k * width:(k + 1) * width] + z for k, z in enumerate(up_bias)], axis=1)
    h = (gate * _sigmoid(gate) * up).astype(BF16)
    return x + 0.5 * _dot(h, wd_ref[...])


def _ffn_proj_kernel(l_ref, x_ref, g1_ref, wg_ref, wu_ref, wd_ref, gm_ref, win_ref,
                     x1_ref, ps_ref, qkv_ref, u_ref):
    del l_ref
    x1 = _swiglu_residual(x_ref[...], g1_ref[...], wg_ref, wu_ref, wd_ref)
    x1_ref[...] = x1
    h = _rms_norm(x1, gm_ref[...]).astype(BF16)
    proj = _dot(h, win_ref[...])
    d_shift = ps_ref.shape[-1]
    d_qkv = qkv_ref.shape[-1]
    ps_ref[...] = proj[:, :d_shift]
    qkv_ref[...] = proj[:, d_shift:d_shift + d_qkv]
    u_ref[...] = proj[:, d_shift + d_qkv:]


def _layer_spec(shape, buffered=True):
    nd = len(shape)
    kw = dict(pipeline_mode=pl.Buffered(1)) if buffered else {}
    return pl.BlockSpec((None,) + tuple(shape), lambda *a: (a[-1][0],) + (0,) * nd, **kw)


def _ffn_proj(lidx, x, g1, wg, wu, wd, gm, win, dims, tm):
    T, D = x.shape
    d_shift, d_qkv, d_u = dims
    row = lambda w: pl.BlockSpec((tm, w), lambda i, l: (i, 0))
    grid_spec = pltpu.PrefetchScalarGridSpec(
        num_scalar_prefetch=1, grid=(T // tm,),
        in_specs=[row(D), _layer_spec((1, D)), _layer_spec(wg.shape[1:]), _layer_spec(wu.shape[1:]),
                  _layer_spec(wd.shape[1:]), _layer_spec((1, D)), _layer_spec(win.shape[1:])],
        out_specs=[row(D), row(d_shift), row(d_qkv), row(d_u)])
    return pl.pallas_call(
        _ffn_proj_kernel, grid_spec=grid_spec, name="ffn_proj",
        out_shape=[jax.ShapeDtypeStruct((T, D), F32), jax.ShapeDtypeStruct((T, d_shift), F32),
                   jax.ShapeDtypeStruct((T, d_qkv), F32), jax.ShapeDtypeStruct((T, d_u), F32)],
        compiler_params=pltpu.CompilerParams(dimension_semantics=("arbitrary",),
                                             vmem_limit_bytes=V7X_VMEM_LIMIT),
    )(lidx, x, g1, wg, wu, wd, gm, win)


CONV_PAD = 32
CONV_ROWS = 64
FFN_TIE_COLS = 256
SUBLANES = 8


def _conv_rows(z_ref, r0, w_ref, b_ref, lnw_ref, lnb_ref):
    d_c = z_ref.shape[-1]
    shift = CONV_PAD - (CONV_WIDTH - 1)
    n_win = CONV_ROWS + CONV_PAD
    win = z_ref[r0:r0 + n_win, :]
    acc = jnp.zeros((CONV_ROWS, d_c), F32) + b_ref[...]
    for sub in range(SUBLANES):
        rolled = win if sub == 0 else pltpu.roll(win, n_win - sub, 0)
        for j in range(CONV_WIDTH):
            off = shift + j
            if off % SUBLANES == sub:
                base = off - sub
                acc = acc + rolled[base:base + CONV_ROWS, :] * w_ref[j:j + 1, :]
    mean = jnp.mean(acc, axis=-1, keepdims=True)
    d = acc - mean
    var = jnp.mean(d * d, axis=-1, keepdims=True)
    z = d * lax.rsqrt(var + CONV_LN_EPS) * lnw_ref[...] + lnb_ref[...]
    return z * _sigmoid(z)


def _out_ffn_kernel(tiles_per_seq, l_ref, x_ref, yr_ref, ya_ref, u_ref, cw_ref, cb_ref, clnw_ref,
                    clnb_ref, wo_ref, g2_ref, wg_ref, wu_ref, wd_ref, o_ref, yc_s, z_s):
    del l_ref
    i = pl.program_id(0)
    tm = x_ref.shape[0]
    d_r = yr_ref.shape[-1]
    d_a = ya_ref.shape[-1]
    d_c = yc_s.shape[-1]

    @pl.when(i == 0)
    def _():
        yc_s[...] = jnp.zeros_like(yc_s)
        z_s[...] = jnp.zeros_like(z_s)

    yc_prev = yc_s[...].astype(BF16)

    left = z_s[tm:tm + CONV_PAD, :]
    z_s[0:CONV_PAD, :] = jnp.where(i % tiles_per_seq == 0, jnp.zeros_like(left), left)
    z_s[CONV_PAD:, :] = u_ref[:, 0:d_c] * _sigmoid(u_ref[:, d_c:])
    n_ties = wg_ref.shape[1] // FFN_TIE_COLS
    rows_per_tie = -(-(tm // CONV_ROWS) // n_ties)
    zeros = []
    tie = jnp.zeros((SUBLANES, 128), F32)
    for ci, r0 in enumerate(range(0, tm, CONV_ROWS)):
        yc = _conv_rows(z_s, r0, cw_ref, cb_ref, clnw_ref, clnb_ref)
        yc_s[r0:r0 + CONV_ROWS, :] = yc
        for s0 in range(0, CONV_ROWS, SUBLANES):
            for c0 in range(0, d_c, 128):
                tie = tie + yc[s0:s0 + SUBLANES, c0:c0 + 128]
        if (ci + 1) % rows_per_tie == 0 or r0 + CONV_ROWS == tm:
            bits = pltpu.bitcast(tie, jnp.uint32)
            z = pltpu.bitcast(lax.shift_right_logical(lax.shift_right_logical(bits, jnp.uint32(16)),
                                                      jnp.uint32(16)), F32)
            zeros.append(jnp.max(z, axis=(0, 1), keepdims=True))
            tie = jnp.zeros((SUBLANES, 128), F32)
    zero = zeros + [zeros[-1]] * (n_ties - len(zeros))

    x2 = (x_ref[...]
          + _dot(yr_ref[...].astype(BF16), wo_ref[0:d_r, :])
          + _dot(ya_ref[...].astype(BF16), wo_ref[d_r:d_r + d_a, :])
          + _dot(yc_prev, wo_ref[d_r + d_a:, :]))
    o_ref[...] = _swiglu_residual(x2, g2_ref[...], wg_ref, wu_ref, wd_ref, up_bias=zero)


def _out_ffn(lidx, x, yr, ya, u, cw, cb, clnw, clnb, wo, g2, wg, wu, wd, tm, seq_len):
    T, D = x.shape
    d_c = u.shape[1] // 2
    n_tiles = T // tm
    assert seq_len % tm == 0 and tm % CONV_ROWS == 0
    lag = lambda w: pl.BlockSpec((tm, w), lambda i, l: (jnp.maximum(i - 1, 0), 0))
    grid_spec = pltpu.PrefetchScalarGridSpec(
        num_scalar_prefetch=1, grid=(n_tiles + 1,),
        in_specs=[lag(D), lag(yr.shape[1]), lag(ya.shape[1]),
                  pl.BlockSpec((tm, 2 * d_c), lambda i, l: (jnp.minimum(i, n_tiles - 1), 0)),
                  _layer_spec((CONV_WIDTH, d_c), buffered=False), _layer_spec((1, d_c), buffered=False),
                  _layer_spec((1, d_c), buffered=False), _layer_spec((1, d_c), buffered=False),
                  _layer_spec(wo.shape[1:]), _layer_spec((1, D)), _layer_spec(wg.shape[1:]),
                  _layer_spec(wu.shape[1:]), _layer_spec(wd.shape[1:])],
        out_specs=lag(D),
        scratch_shapes=[pltpu.VMEM((tm, d_c), F32), pltpu.VMEM((CONV_PAD + tm, d_c), F32)])
    return pl.pallas_call(
        functools.partial(_out_ffn_kernel, seq_len // tm), grid_spec=grid_spec, name="out_ffn",
        out_shape=jax.ShapeDtypeStruct((T, D), F32),
        compiler_params=pltpu.CompilerParams(dimension_semantics=("arbitrary",),
                                             vmem_limit_bytes=V7X_VMEM_LIMIT),
    )(lidx, x, yr, ya, u, cw, cb, clnw, clnb, wo, g2, wg, wu, wd)


def _rwkv_kernel(l_ref, p_ref, mu_ref, w0_ref, w2_ref, a0_ref, a2_ref, g2_ref, kk_ref, ka_ref,
                 rk_ref, lnw_ref, lnb_ref, bd_ref, tri_ref, o_ref,
                 carry_ref, state_ref):
    del l_ref
    tb = p_ref.shape[1]
    d_r = o_ref.shape[-1]

    @pl.when(pl.program_id(1) == 0)
    def _():
        carry_ref[...] = jnp.zeros_like(carry_ref)
        state_ref[...] = jnp.zeros_like(state_ref)

    p = p_ref[0]
    row = lax.broadcasted_iota(jnp.int32, (tb, 1), 0)
    prev = jnp.where(row == 0, carry_ref[...], pltpu.roll(p, 1, 0))
    carry_ref[...] = p[tb - 1:tb, :]
    ps = p + (prev - p) * mu_ref[...]

    r = ps[:, 0:d_r]
    k = ps[:, d_r:2 * d_r]
    v = ps[:, 2 * d_r:3 * d_r]
    xs = ps[:, 3 * d_r:3 * d_r + LORA_SLAB]

    bd = bd_ref[...]
    def seg_sum(t):
        tb16 = t.astype(BF16)
        return jnp.concatenate([_dot(tb16[:, s:s + PAIR], bd) for s in range(0, d_r, PAIR)], axis=1)

    lw = w0_ref[...] + _dot_f32(jnp.tanh(xs), w2_ref[...])
    ld = (-0.5 * EXP_NEG_HALF) * jnp.tanh(0.5 * lw) - 0.5 * EXP_NEG_HALF
    a = _sigmoid(a0_ref[...] + _dot_bf16(xs, a2_ref[...]))
    g = _dot_bf16(_sigmoid(xs), g2_ref[...])
    kk = k * kk_ref[...]
    kk = kk * lax.rsqrt(seg_sum(kk * kk) + KK_EPS)
    ka = ka_ref[...]
    k2 = k * ((1.0 - ka) + a * ka)
    b = kk * a

    lcum = jnp.concatenate(
        [_dot_f32_rhs(tri_ref[...], ld[s:s + CUMSUM_ROWS]) for s in range(0, tb, CUMSUM_ROWS)], axis=0)
    n_chunks = tb // CHUNK
    pc_rows = [jnp.exp(lcum[(c + 1) * CHUNK - 1:(c + 1) * CHUNK, :]) for c in range(n_chunks)]
    at_all = (-kk * jnp.exp(lcum - ld)).astype(BF16)
    rt_all = r * jnp.exp(lcum)
    inv_p = jnp.exp(-lcum)
    bt_all = (b * inv_p).astype(BF16)
    kt_all = (k2 * inv_p).astype(BF16)
    rest = inv_p * jnp.concatenate([jnp.broadcast_to(pc, (CHUNK, d_r)) for pc in pc_rows], axis=0)
    bp_all = (b * rest).astype(BF16)
    kp_all = (k2 * rest).astype(BF16)
    v_all = v.astype(BF16)

    lane = lax.broadcasted_iota(jnp.int32, (CHUNK, PAIR), 1)
    trow = lax.broadcasted_iota(jnp.int32, (CHUNK, PAIR), 0)
    head0 = lane < HEAD_DIM
    scol = lane & (HEAD_DIM - 1)
    strict = trow > scol
    incl = trow >= scol
    eye2 = jnp.where(trow == scol, 1.0, 0.0)
    r2 = lax.broadcasted_iota(jnp.int32, (PAIR, PAIR), 0)
    c2 = lax.broadcasted_iota(jnp.int32, (PAIR, PAIR), 1)
    same_head = (r2 < HEAD_DIM) == (c2 < HEAD_DIM)

    def block_diag(xb):
        zero = jnp.zeros_like(xb)
        return jnp.concatenate([jnp.where(head0, xb, zero), jnp.where(head0, zero, xb)], axis=0)

    n_pairs = d_r // PAIR
    units = [(c, p) for c in range(n_chunks) for p in range(n_pairs)]
    tile = lambda t, c, p: t[c * CHUNK:(c + 1) * CHUNK, p * PAIR:(p + 1) * PAIR]

    a_ab, a_ak, a_rb, a_rk = {}, {}, {}, {}
    for u in units:
        ar = jnp.concatenate([tile(at_all, *u), tile(rt_all, *u).astype(BF16)], axis=0)
        bk = jnp.concatenate([block_diag(tile(bt_all, *u)), block_diag(tile(kt_all, *u))], axis=0)
        g2 = _dot_nt(ar, bk)
        a_ab[u] = jnp.where(strict, g2[:CHUNK, :PAIR], 0.0)
        a_ak[u] = jnp.where(strict, g2[:CHUNK, PAIR:], 0.0).astype(BF16)
        a_rb[u] = jnp.where(incl, g2[CHUNK:, :PAIR], 0.0).astype(BF16)
        a_rk[u] = jnp.where(incl, g2[CHUNK:, PAIR:], 0.0).astype(BF16)
    tinv = {u: eye2 + a_ab[u] for u in units}
    apow = {}
    for u in units:
        ab = a_ab[u].astype(BF16)
        apow[u] = _dot(ab, block_diag(ab))
    for step in range(1, NEUMANN_STEPS):
        last = step == NEUMANN_STEPS - 1
        for u in units:
            ab = apow[u].astype(BF16)
            lhs = tinv[u].astype(BF16) if last else jnp.concatenate([tinv[u].astype(BF16), ab], axis=0)
            prod = _dot(lhs, block_diag(ab))
            tinv[u] = tinv[u] + prod[:CHUNK]
            if not last:
                apow[u] = prod[CHUNK:]
    av, y1v, w1, u1, rq, y1, mx, gx = {}, {}, {}, {}, {}, {}, {}, {}
    for u in units:
        avk = _dot(jnp.concatenate([a_ak[u], a_rk[u]], axis=0), block_diag(tile(v_all, *u)))
        av[u], y1v[u] = avk[:CHUNK].astype(BF16), avk[CHUNK:]
    for u in units:
        rhs = jnp.concatenate([block_diag(tile(at_all, *u)), block_diag(av[u])], axis=1)
        wu = _dot(tinv[u].astype(BF16), rhs).astype(BF16)
        w1[u], u1[u] = wu[:, :PAIR], wu[:, PAIR:]
    for u in units:
        vc = tile(v_all, *u)
        bpkp = jnp.concatenate([tile(bp_all, *u), tile(kp_all, *u)], axis=0)
        ry = _dot(a_rb[u], jnp.concatenate([block_diag(w1[u]), block_diag(u1[u])], axis=1))
        rq[u] = (tile(rt_all, *u) + ry[:, :PAIR]).astype(BF16)
        y1[u] = ry[:, PAIR:] + y1v[u]
        lhs = jnp.concatenate([jnp.concatenate([w1[u], u1[u]], axis=1),
                               jnp.concatenate([jnp.zeros_like(vc), vc], axis=1)], axis=0)
        mg = _dot_tn(lhs, bpkp)
        mx[u] = jnp.where(same_head, mg[:PAIR], 0.0).astype(BF16)
        gx[u] = jnp.where(same_head, mg[PAIR:], 0.0)

    y_rows = []
    for c in range(n_chunks):
        y_parts = []
        for p in range(n_pairs):
            u = (c, p)
            s0 = state_ref[p]
            s0b = s0.astype(BF16)
            y_parts.append(_dot_nt(rq[u], s0b) + y1[u])
            pc = pc_rows[c][:, p * PAIR:(p + 1) * PAIR]
            state_ref[p] = s0 * pc + _dot(s0b, mx[u]) + gx[u]
        y_rows.append(jnp.concatenate(y_parts, axis=1))
    y = jnp.concatenate(y_rows, axis=0)
    inv_n = 1.0 / HEAD_DIM
    mean = seg_sum(y) * inv_n
    d = y - mean
    var = seg_sum(d * d) * inv_n
    yn = d * lax.rsqrt(var + RWKV_GN_EPS) * lnw_ref[...] + lnb_ref[...]
    bonus = seg_sum(r * k2 * rk_ref[...]) * v
    o_ref[0] = (yn + bonus) * g


def _rwkv(lidx, ps3, mu, w0, w2p, a0, a2p, g2p, k_k, k_a, r_k, ln_w, ln_b, d_r, tb):
    B, S, d_shift = ps3.shape
    hid = jnp.arange(PAIR) // HEAD_DIM
    bd = (hid[:, None] == hid[None, :]).astype(BF16)
    assert tb % CUMSUM_ROWS == 0
    ci = jnp.arange(CUMSUM_ROWS)
    same = (ci[:, None] // CHUNK) == (ci[None, :] // CHUNK)
    tri = (same & (ci[:, None] >= ci[None, :])).astype(BF16)
    const = lambda a: pl.BlockSpec(a.shape, lambda b, t, l: (0,) * a.ndim)
    vec = _layer_spec((1, d_r), buffered=False)
    lora = _layer_spec((LORA_SLAB, d_r), buffered=False)
    grid_spec = pltpu.PrefetchScalarGridSpec(
        num_scalar_prefetch=1, grid=(B, S // tb),
        in_specs=[pl.BlockSpec((1, tb, d_shift), lambda b, t, l: (b, t, 0)),
                  _layer_spec((1, d_shift), buffered=False),
                  vec, lora, vec, lora, lora, vec, vec, vec, vec, vec,
                  const(bd), const(tri)],
        out_specs=pl.BlockSpec((1, tb, d_r), lambda b, t, l: (b, t, 0)),
        scratch_shapes=[pltpu.VMEM((1, d_shift), F32),
                        pltpu.VMEM((d_r // PAIR, PAIR, PAIR), F32)])
    return pl.pallas_call(
        _rwkv_kernel, grid_spec=grid_spec, name="rwkv7",
        out_shape=jax.ShapeDtypeStruct((B, S, d_r), F32),
        compiler_params=pltpu.CompilerParams(dimension_semantics=("arbitrary", "arbitrary"),
                                             vmem_limit_bytes=V7X_VMEM_LIMIT),
    )(lidx, ps3, mu, w0, w2p, a0, a2p, g2p, k_k, k_a, r_k, ln_w, ln_b, bd, tri)


def _attn_kernel(l_ref, q_ref, k_ref, v_ref, qn_ref, kn_ref, bd_ref, bias_ref, o_ref,
                 q_s, k_s, q4_s, k4_s, v4_s, st_s):
    del l_ref
    S = q_ref.shape[1]
    bd = bd_ref[...]
    inv_n = 1.0 / HEAD_DIM

    def head_rms(t, gain):
        ms = _dot_bf16(t * t, bd) * inv_n
        return t * lax.rsqrt(ms + NORM_EPS) * gain

    q_s[...] = head_rms(q_ref[0], qn_ref[...]) * (HEAD_DIM ** -0.5)
    k_s[...] = head_rms(k_ref[0], kn_ref[...])
    head0 = lax.broadcasted_iota(jnp.int32, (ATTN_BLOCK, PAIR), 1) < HEAD_DIM

    s_pre = S // PRE_DILATION
    v_tok = v_ref.at[0]

    def regroup(i, carry):
        c = i // (s_pre // REGROUP_ROWS)
        r0 = (i % (s_pre // REGROUP_ROWS)) * REGROUP_ROWS
        src = pl.ds(c + PRE_DILATION * r0, REGROUP_ROWS, stride=PRE_DILATION)
        dst = pl.ds(pl.multiple_of(c * s_pre + r0, REGROUP_ROWS), REGROUP_ROWS)
        q4_s[dst, :] = q_s[src, :]
        k4_s[dst, :] = k_s[src, :]
        v4_s[dst, :] = v_tok[src, :]
        return carry

    lax.fori_loop(0, S // REGROUP_ROWS, regroup, 0)

    for pi, (window, dil) in enumerate(DILATED_PATTERNS):
        n_sub = S // dil
        n_blk = n_sub // ATTN_BLOCK
        regrouped = dil % PRE_DILATION == 0
        if regrouped:
            stride = dil // PRE_DILATION
            q_src, k_src, v_src = q4_s, k4_s, v4_s
        else:
            stride = dil
            q_src, k_src, v_src = q_s, k_s, v_tok
        state = tuple(st_s.at[3 * pi + j] for j in range(3))
        span = stride * ATTN_BLOCK

        def group_body(gi, carry):
            blocks = []
            for g in range(ATTN_GROUP):
                i = gi * ATTN_GROUP + g
                n = i // dil
                res = i % dil
                if regrouped:
                    q_start = (res % PRE_DILATION) * s_pre + res // PRE_DILATION + span * n
                else:
                    q_start = res + span * n
                first = jnp.where(n == 0, 1, 0)
                k_start = q_start - span * (1 - first)
                if stride > 1:
                    q_rows = pl.ds(q_start, ATTN_BLOCK, stride=stride)
                    k_rows = pl.ds(k_start, 2 * ATTN_BLOCK, stride=stride)
                else:
                    q_rows = pl.ds(pl.multiple_of(q_start, ATTN_BLOCK), ATTN_BLOCK)
                    k_rows = pl.ds(pl.multiple_of(k_start, ATTN_BLOCK), 2 * ATTN_BLOCK)
                blocks.append(dict(first=first, q_rows=q_rows, k_rows=k_rows))
            for blk in blocks:
                q2 = q_src[blk["q_rows"], :]
                zero = jnp.zeros_like(q2)
                blk["q"] = jnp.concatenate([jnp.where(head0, q2, zero), jnp.where(head0, zero, q2)],
                                           axis=0).astype(BF16)
                blk["k"] = k_src[blk["k_rows"], :].astype(BF16)
                blk["v"] = v_src[blk["k_rows"], :].astype(BF16)
            for blk in blocks:
                blk["s"] = _dot_nt(blk["q"], blk["k"]) + bias_ref[pi, blk["first"]]
            for blk in blocks:
                blk["m"] = jnp.max(blk["s"], axis=-1, keepdims=True)
            for blk in blocks:
                blk["e"] = jnp.exp(blk["s"] - blk["m"])
            for blk in blocks:
                blk["l"] = jnp.sum(blk["e"], axis=-1, keepdims=True)
                blk["o"] = _dot(blk["e"].astype(BF16), blk["v"])
            for blk in blocks:
                pair = lambda t: jnp.where(head0, jnp.broadcast_to(t[:ATTN_BLOCK], (ATTN_BLOCK, PAIR)),
                                           jnp.broadcast_to(t[ATTN_BLOCK:], (ATTN_BLOCK, PAIR)))
                blk["out"] = (pair(blk["o"]), pair(blk["m"]), pair(blk["l"]))
            for blk in blocks:
                for ref, val in zip(state, blk["out"]):
                    ref[blk["q_rows"], :] = val
            return carry

        lax.fori_loop(0, dil * n_blk // ATTN_GROUP, group_body, 0)

    def finish(i, carry):
        c = i // (s_pre // ATTN_BLOCK)
        r0 = (i % (s_pre // ATTN_BLOCK)) * ATTN_BLOCK
        tok = pl.ds(c + PRE_DILATION * r0, ATTN_BLOCK, stride=PRE_DILATION)
        grp = pl.ds(pl.multiple_of(c * s_pre + r0, ATTN_BLOCK), ATTN_BLOCK)
        total = None
        for pi, (_, dil) in enumerate(DILATED_PATTERNS):
            rows = grp if dil % PRE_DILATION == 0 else tok
            part = tuple(st_s[3 * pi + j, rows, :] for j in range(3))
            total = part if total is None else _softmax_merge(total, part)
        o_ref[0, tok, :] = total[0] / total[2]
        return carry

    lax.fori_loop(0, S // ATTN_BLOCK, finish, 0)


def _softmax_merge(a, b):
    acc_a, m_a, l_a = a
    acc_b, m_b, l_b = b
    m_new = jnp.maximum(m_a, m_b)
    w_a = jnp.exp(m_a - m_new)
    w_b = jnp.exp(m_b - m_new)
    return acc_a * w_a + acc_b * w_b, m_new, l_a * w_a + l_b * w_b


def _alibi_slopes(n):
    def pow2(m):
        start = 2.0 ** (-8.0 / m)
        return [start ** (i + 1) for i in range(m)]
    if math.log2(n).is_integer():
        return pow2(n)
    c = 2 ** int(math.floor(math.log2(n)))
    return pow2(c) + pow2(2 * c)[0::2][: n - c]


def _attn_bias(n_heads):
    qi = jnp.arange(ATTN_BLOCK)
    ki = jnp.arange(2 * ATTN_BLOCK)
    dist = qi[:, None] + ATTN_BLOCK - ki[None, :]
    slopes = jnp.asarray(_alibi_slopes(n_heads), F32)
    out = []
    for window, dil in DILATED_PATTERNS:
        valid = (dist >= 0) & (dist <= window // dil)
        bias = -slopes[:, None, None] * (dist * dil).astype(F32)[None]
        rest = jnp.where(valid[None], bias, MASK_VALUE)
        first = jnp.concatenate([rest[..., ATTN_BLOCK:], jnp.full_like(rest[..., ATTN_BLOCK:], MASK_VALUE)], -1)
        out.append(jnp.stack([rest, first], 0).reshape(2, n_heads * ATTN_BLOCK, 2 * ATTN_BLOCK))
    return jnp.stack(out, 0)


def _attention(lidx, qkv3, q_norm, k_norm, d_a):
    B, S, _ = qkv3.shape
    n_heads = d_a // HEAD_DIM
    n_pairs = n_heads // 2
    slab = 2 * HEAD_DIM
    hid = jnp.arange(slab) // HEAD_DIM
    bd = (hid[:, None] == hid[None, :]).astype(BF16)
    bias = _attn_bias(n_heads)
    col = lambda off: pl.BlockSpec((1, S, slab), lambda b, p, l: (b, 0, off + p))
    grid_spec = pltpu.PrefetchScalarGridSpec(
        num_scalar_prefetch=1, grid=(B, n_pairs),
        in_specs=[col(0), col(n_pairs), col(2 * n_pairs),
                  _layer_spec((1, slab), buffered=False), _layer_spec((1, slab), buffered=False),
                  pl.BlockSpec(bd.shape, lambda b, p, l: (0, 0)),
                  pl.BlockSpec((len(DILATED_PATTERNS), 2, 2 * ATTN_BLOCK, 2 * ATTN_BLOCK),
                               lambda b, p, l: (0, 0, p, 0))],
        out_specs=pl.BlockSpec((1, S, slab), lambda b, p, l: (b, 0, p)),
        scratch_shapes=[pltpu.VMEM((S, slab), F32)] * 5
        + [pltpu.VMEM((3 * len(DILATED_PATTERNS), S, slab), F32)])
    return pl.pallas_call(
        _attn_kernel, grid_spec=grid_spec, name="dilated_attn",
        out_shape=jax.ShapeDtypeStruct((B, S, d_a), F32),
        compiler_params=pltpu.CompilerParams(dimension_semantics=("arbitrary", "arbitrary"),
                                             vmem_limit_bytes=V7X_VMEM_LIMIT),
    )(lidx, qkv3, qkv3, qkv3, q_norm, k_norm, bd, bias)


def _pad_lora(w, offset):
    L, r, d = w.shape
    return jnp.zeros((L, LORA_SLAB, d), F32).at[:, offset:offset + r, :].set(w)


def kernel(x, norm_ffn1, ffn1_w_gate, ffn1_w_up, ffn1_w_down, norm_mix, w_in, shift_mu, rwkv_w0, rwkv_w2, rwkv_a0, rwkv_a2, rwkv_g2, rwkv_k_k, rwkv_k_a, rwkv_r_k, rwkv_ln_w, rwkv_ln_b, attn_q_norm, attn_k_norm, conv_dw_w, conv_dw_b, conv_ln_w, conv_ln_b, w_out, norm_ffn2, ffn2_w_gate, ffn2_w_up, ffn2_w_down):
    B, S, D = x.shape
    depth = w_in.shape[0]
    d_r = rwkv_w0.shape[-1]
    d_c = conv_dw_b.shape[-1]
    d_shift = shift_mu.shape[-1]
    d_a = (w_in.shape[-1] - d_shift - 2 * d_c) // 3
    r_w, r_a, r_g = rwkv_w2.shape[1], rwkv_a2.shape[1], rwkv_g2.shape[1]
    assert d_shift == 3 * d_r + LORA_SLAB and r_w + r_a + r_g == LORA_SLAB
    assert S % (2 * ATTN_BLOCK * DILATED_PATTERNS[-1][1]) == 0 and (d_a // HEAD_DIM) % 2 == 0
    assert (S // ATTN_BLOCK) % ATTN_GROUP == 0 and S % (PRE_DILATION * REGROUP_ROWS) == 0
    assert all(d < PRE_DILATION or d % PRE_DILATION == 0 for _, d in DILATED_PATTERNS)
    T = B * S
    tm = 256
    tb = 512

    vec = lambda a: a.reshape(depth, 1, -1)
    bf = lambda a: a.astype(BF16)
    wg1, wu1, wd1 = bf(ffn1_w_gate), bf(ffn1_w_up), bf(ffn1_w_down)
    wg2, wu2, wd2 = bf(ffn2_w_gate), bf(ffn2_w_up), bf(ffn2_w_down)
    win, wo = bf(w_in), bf(w_out)
    g1, gm, g2 = vec(norm_ffn1), vec(norm_mix), vec(norm_ffn2)
    w2p = _pad_lora(rwkv_w2, 0)
    a2p = _pad_lora(rwkv_a2, r_w)
    g2p = _pad_lora(rwkv_g2, r_w + r_a)
    qn = vec(jnp.tile(attn_q_norm, (1, 2)))
    kn = vec(jnp.tile(attn_k_norm, (1, 2)))

    def layer(l, xf):
        lidx = jnp.reshape(l, (1,)).astype(jnp.int32)
        x1, ps, qkv, u = _ffn_proj(lidx, xf, g1, wg1, wu1, wd1, gm, win, (d_shift, 3 * d_a, 2 * d_c), 2 * tm)
        y_r = _rwkv(lidx, ps.reshape(B, S, d_shift), vec(shift_mu), vec(rwkv_w0), w2p, vec(rwkv_a0),
                    a2p, g2p, vec(rwkv_k_k), vec(rwkv_k_a), vec(rwkv_r_k), vec(rwkv_ln_w),
                    vec(rwkv_ln_b), d_r, tb)
        y_a = _attention(lidx, qkv.reshape(B, S, 3 * d_a), qn, kn, d_a)
        return _out_ffn(lidx, x1, y_r.reshape(T, d_r), y_a.reshape(T, d_a), u, conv_dw_w,
                        vec(conv_dw_b), vec(conv_ln_w), vec(conv_ln_b), wo, g2, wg2, wu2, wd2,
                        2 * tm, S)

    out = lax.fori_loop(0, depth, layer, x.reshape(T, D))
    return out.reshape(B, S, D)
```

```python
import functools
import math

import jax
import jax.numpy as jnp
from jax import lax
from jax.experimental import pallas as pl
from jax.experimental.pallas import tpu as pltpu

F32 = jnp.float32
BF16 = jnp.bfloat16

LANES = 128
SUBLANES = 8
FFN_ROWS = 512
RWKV_ROWS = 512
HEAD_DIM = 64
PAIR = 2 * HEAD_DIM
NORM_EPS = 1e-6
RWKV_GN_EPS = 64e-5
CONV_LN_EPS = 1e-5
KK_EPS = 1e-12
CONV_WIDTH = 31
DILATED_PATTERNS = ((128, 1), (512, 4), (2048, 16))
ATTN_BLOCK = 128
ATTN_GROUP = 8
PRE_DILATION = 4
REGROUP_ROWS = 256
LORA_SLAB = 128
CHUNK = 64
GROUP_CHUNKS = 8
CUMSUM_ROWS = 256
NEUMANN_STEPS = 6
MASK_VALUE = -1e30
EXP_NEG_HALF = math.exp(-0.5)
V7X_VMEM_LIMIT = 56 * 1024 * 1024

NT_DIMS = (((1,), (1,)), ((), ()))
TN_DIMS = (((0,), (0,)), ((), ()))


def _dot(a, b):
    return jnp.dot(a, b, preferred_element_type=F32)


def _dot_nt(a, b):
    return lax.dot_general(a, b, NT_DIMS, preferred_element_type=F32)


def _dot_tn(a, b):
    return lax.dot_general(a, b, TN_DIMS, preferred_element_type=F32)


def _split2(x):
    hi = x.astype(BF16)
    lo = (x - hi.astype(F32)).astype(BF16)
    return hi, lo


def _dot_bf16(x, w):
    return _dot(x.astype(BF16), w.astype(BF16))


def _dot_f32_rhs(w_exact, x):
    hi, lo = _split2(x)
    return _dot(w_exact, hi) + _dot(w_exact, lo)


def _dot_f32(x, w):
    xh, xl = _split2(x)
    wh, wl = _split2(w)
    return _dot(xh, wh) + _dot(xl, wh) + _dot(xh, wl)


def _sigmoid(x):
    return 0.5 + 0.5 * jnp.tanh(0.5 * x)


def _rms_norm(x, g):
    return x * lax.rsqrt(jnp.mean(x * x, axis=-1, keepdims=True) + NORM_EPS) * g


def _swiglu_residual(x, g, wg_ref, wu_ref, wd_ref):
    xn = _rms_norm(x, g).astype(BF16)
    gate = _dot(xn, wg_ref[...])
    up = _dot(xn, wu_ref[...])
    h =(gate * _sigmoid(gate) * up).astype(BF16)
    return x + 0.5 * _dot(h, wd_ref[...])


def _ffn_proj_kernel(l_ref, x_ref, g1_ref, wg_ref, wu_ref, wd_ref, gm_ref, win_ref,
                     x1_ref, ps_ref, qkv_ref, u_ref):
    del l_ref
    x1 = _swiglu_residual(x_ref[...], g1_ref[...], wg_ref, wu_ref, wd_ref)
    x1_ref[...] = x1
    h = _rms_norm(x1, gm_ref[...]).astype(BF16)
    proj = _dot(h, win_ref[...])
    d_shift = ps_ref.shape[-1]
    d_qkv = qkv_ref.shape[-1]
    ps_ref[...] = proj[:, :d_shift]
    qkv_ref[...] = proj[:, d_shift:d_shift + d_qkv]
    u_ref[...] = proj[:, d_shift + d_qkv:]


def _layer_spec(shape, buffered=True):
    nd = len(shape)
    kw = dict(pipeline_mode=pl.Buffered(1)) if buffered else {}
    return pl.BlockSpec((None,) + tuple(shape), lambda *a: (a[-1][0],) + (0,) * nd, **kw)


def _ffn_proj(lidx, x, g1, wg, wu, wd, gm, win, dims, tm):
    T, D = x.shape
    d_shift, d_qkv, d_u = dims
    row = lambda w: pl.BlockSpec((tm, w), lambda i, l: (i, 0))
    grid_spec = pltpu.PrefetchScalarGridSpec(
        num_scalar_prefetch=1, grid=(T // tm,),
        in_specs=[row(D), _layer_spec((1, D)), _layer_spec(wg.shape[1:]), _layer_spec(wu.shape[1:]),
                  _layer_spec(wd.shape[1:]), _layer_spec((1, D)), _layer_spec(win.shape[1:])],
        out_specs=[row(D), row(d_shift), row(d_qkv), row(d_u)])
    return pl.pallas_call(
        _ffn_proj_kernel, grid_spec=grid_spec, name="ffn_proj",
        out_shape=[jax.ShapeDtypeStruct((T, D), F32), jax.ShapeDtypeStruct((T, d_shift), F32),
                   jax.ShapeDtypeStruct((T, d_qkv), F32), jax.ShapeDtypeStruct((T, d_u), F32)],
        compiler_params=pltpu.CompilerParams(dimension_semantics=("arbitrary",),
                                             vmem_limit_bytes=V7X_VMEM_LIMIT),
    )(lidx, x, g1, wg, wu, wd, gm, win)


CONV_PAD = 32
CONV_ROWS = 64


def _conv_rows(z_ref, r0, w_ref, b_ref, lnw_ref, lnb_ref):
    d_c = z_ref.shape[-1]
    shift = CONV_PAD - (CONV_WIDTH - 1)
    n_win = CONV_ROWS + CONV_PAD
    win = z_ref[r0:r0 + n_win, :]
    acc = jnp.zeros((CONV_ROWS, d_c), F32) + b_ref[...]
    for sub in range(SUBLANES):
        rolled = win if sub == 0 else pltpu.roll(win, n_win - sub, 0)
        for j in range(CONV_WIDTH):
            off = shift + j
            if off % SUBLANES == sub:
                base = off - sub
                acc = acc + rolled[base:base + CONV_ROWS, :] * w_ref[j:j + 1, :]
    mean = jnp.mean(acc, axis=-1, keepdims=True)
    d = acc - mean
    var = jnp.mean(d * d, axis=-1, keepdims=True)
    z = d * lax.rsqrt(var + CONV_LN_EPS) * lnw_ref[...] + lnb_ref[...]
    return z * _sigmoid(z)


def _out_ffn_kernel(tiles_per_seq, l_ref, x_ref, yr_ref, ya_ref, u_ref, cw_ref, cb_ref, clnw_ref,
                    clnb_ref, wo_ref, g2_ref, wg_ref, wu_ref, wd_ref, o_ref, yc_s, z_s):
    del l_ref
    i = pl.program_id(0)
    tm = x_ref.shape[0]
    d_r = yr_ref.shape[-1]
    d_a = ya_ref.shape[-1]
    d_c = yc_s.shape[-1]

    @pl.when(i == 0)
    def _():
        yc_s[...] = jnp.zeros_like(yc_s)
        z_s[...] = jnp.zeros_like(z_s)

    yc_prev = yc_s[...].astype(BF16)

    left = z_s[tm:tm + CONV_PAD, :]
    z_s[0:CONV_PAD, :] = jnp.where(i % tiles_per_seq == 0, jnp.zeros_like(left), left)
    z_s[CONV_PAD:, :] = u_ref[:, 0:d_c] * _sigmoid(u_ref[:, d_c:])
    for r0 in range(0, tm, CONV_ROWS):
        yc_s[r0:r0 + CONV_ROWS, :] = _conv_rows(z_s, r0, cw_ref, cb_ref, clnw_ref, clnb_ref)

    x2 = (x_ref[...]
          + _dot(yr_ref[...].astype(BF16), wo_ref[0:d_r, :])
          + _dot(ya_ref[...].astype(BF16), wo_ref[d_r:d_r + d_a, :])
          + _dot(yc_prev, wo_ref[d_r + d_a:, :]))
    o_ref[...] = _swiglu_residual(x2, g2_ref[...], wg_ref, wu_ref, wd_ref)


def _out_ffn(lidx, x, yr, ya, u, cw, cb, clnw, clnb, wo, g2, wg, wu, wd, tm, seq_len):
    T, D = x.shape
    d_c = u.shape[1] // 2
    n_tiles = T // tm
    assert seq_len % tm == 0 and tm % CONV_ROWS == 0
    lag = lambda w: pl.BlockSpec((tm, w), lambda i, l: (jnp.maximum(i - 1, 0), 0))
    grid_spec = pltpu.PrefetchScalarGridSpec(
        num_scalar_prefetch=1, grid=(n_tiles + 1,),
        in_specs=[lag(D), lag(yr.shape[1]), lag(ya.shape[1]),
                  pl.BlockSpec((tm, 2 * d_c), lambda i, l: (jnp.minimum(i, n_tiles - 1), 0)),
                  _layer_spec((CONV_WIDTH, d_c), buffered=False), _layer_spec((1, d_c), buffered=False),
                  _layer_spec((1, d_c), buffered=False), _layer_spec((1, d_c), buffered=False),
                  _layer_spec(wo.shape[1:]), _layer_spec((1, D)), _layer_spec(wg.shape[1:]),
                  _layer_spec(wu.shape[1:]), _layer_spec(wd.shape[1:])],
        out_specs=lag(D),
        scratch_shapes=[pltpu.VMEM((tm, d_c), F32), pltpu.VMEM((CONV_PAD + tm, d_c), F32)])
    return pl.pallas_call(
        functools.partial(_out_ffn_kernel, seq_len // tm), grid_spec=grid_spec, name="out_ffn",
        out_shape=jax.ShapeDtypeStruct((T, D), F32),
        compiler_params=pltpu.CompilerParams(dimension_semantics=("arbitrary",),
                                             vmem_limit_bytes=V7X_VMEM_LIMIT),
    )(lidx, x, yr, ya, u, cw, cb, clnw, clnb, wo, g2, wg, wu, wd)


def _rwkv_kernel(l_ref, p_ref, mu_ref, w0_ref, w2_ref, a0_ref, a2_ref, g2_ref, kk_ref, ka_ref,
                 rk_ref, lnw_ref, lnb_ref, bd_ref, tri_ref, o_ref,
                 carry_ref, state_ref):
    del l_ref
    tb = p_ref.shape[1]
    d_r = o_ref.shape[-1]

    @pl.when(pl.program_id(1) == 0)
    def _():
        carry_ref[...] = jnp.zeros_like(carry_ref)
        state_ref[...] = jnp.zeros_like(state_ref)

    p = p_ref[0]
    row = lax.broadcasted_iota(jnp.int32, (tb, 1), 0)
    prev = jnp.where(row == 0, carry_ref[...], pltpu.roll(p, 1, 0))
    carry_ref[...] = p[tb - 1:tb, :]
    ps = p + (prev - p) * mu_ref[...]

    r = ps[:, 0:d_r]
    k = ps[:, d_r:2 * d_r]
    v = ps[:, 2 * d_r:3 * d_r]
    xs = ps[:, 3 * d_r:3 * d_r + LORA_SLAB]

    bd = bd_ref[...]
    def seg_sum(t):
        tb16 = t.astype(BF16)
        return jnp.concatenate([_dot(tb16[:, s:s + PAIR], bd) for s in range(0, d_r, PAIR)], axis=1)

    lw = w0_ref[...] + _dot_f32(jnp.tanh(xs), w2_ref[...])
    ld = (-0.5 * EXP_NEG_HALF) * jnp.tanh(0.5 * lw) - 0.5 * EXP_NEG_HALF
    a = _sigmoid(a0_ref[...] + _dot_bf16(xs, a2_ref[...]))
    g = _dot_bf16(_sigmoid(xs), g2_ref[...])
    kk = k * kk_ref[...]
    kk = kk * lax.rsqrt(seg_sum(kk * kk) + KK_EPS)
    ka = ka_ref[...]
    k2 = k * ((1.0 - ka) + a * ka)
    b = kk * a

    lcum = jnp.concatenate(
        [_dot_f32_rhs(tri_ref[...], ld[s:s + CUMSUM_ROWS]) for s in range(0, tb, CUMSUM_ROWS)], axis=0)
    n_chunks = tb // CHUNK
    pc_rows = [jnp.exp(lcum[(c + 1) * CHUNK - 1:(c + 1) * CHUNK, :]) for c in range(n_chunks)]
    at_all = (-kk * jnp.exp(lcum - ld)).astype(BF16)
    rt_all = r * jnp.exp(lcum)
    inv_p = jnp.exp(-lcum)
    bt_all = (b * inv_p).astype(BF16)
    kt_all = (k2 * inv_p).astype(BF16)
    rest = inv_p * jnp.concatenate([jnp.broadcast_to(pc, (CHUNK, d_r)) for pc in pc_rows], axis=0)
    bp_all = (b * rest).astype(BF16)
    kp_all = (k2 * rest).astype(BF16)
    v_all = v.astype(BF16)

    lane = lax.broadcasted_iota(jnp.int32, (CHUNK, PAIR), 1)
    trow = lax.broadcasted_iota(jnp.int32, (CHUNK, PAIR), 0)
    head0 = lane < HEAD_DIM
    scol = lane & (HEAD_DIM - 1)
    strict = trow > scol
    incl = trow >= scol
    eye2 = jnp.where(trow == scol, 1.0, 0.0)
    r2 = lax.broadcasted_iota(jnp.int32, (PAIR, PAIR), 0)
    c2 = lax.broadcasted_iota(jnp.int32, (PAIR, PAIR), 1)
    same_head = (r2 < HEAD_DIM) == (c2 < HEAD_DIM)

    def block_diag(xb):
        zero = jnp.zeros_like(xb)
        return jnp.concatenate([jnp.where(head0, xb, zero), jnp.where(head0, zero, xb)], axis=0)

    n_pairs = d_r // PAIR
    tile = lambda t, c, p: t[c * CHUNK:(c + 1) * CHUNK, p * PAIR:(p + 1) * PAIR]
    inv_n = 1.0 / HEAD_DIM

    for c_lo in range(0, n_chunks, GROUP_CHUNKS):
        group_chunks = range(c_lo, min(c_lo + GROUP_CHUNKS, n_chunks))
        units = [(c, p) for c in group_chunks for p in range(n_pairs)]
        y = _rwkv_group(units, group_chunks, n_pairs, tile, block_diag, state_ref, pc_rows,
                        (at_all, rt_all, bt_all, kt_all, bp_all, kp_all, v_all),
                        (strict, incl, eye2, same_head))
        rows = slice(group_chunks[0] * CHUNK, (group_chunks[-1] + 1) * CHUNK)
        mean = seg_sum(y) * inv_n
        d = y - mean
        var = seg_sum(d * d) * inv_n
        yn = d * lax.rsqrt(var + RWKV_GN_EPS) * lnw_ref[...] + lnb_ref[...]
        bonus = seg_sum(r[rows] * k2[rows] * rk_ref[...]) * v[rows]
        o_ref[0, rows, :] = (yn + bonus) * g[rows]


def _rwkv_group(units, group_chunks, n_pairs, tile, block_diag, state_ref, pc_rows, scaled, masks):
    at_all, rt_all, bt_all, kt_all, bp_all, kp_all, v_all = scaled
    strict, incl, eye2, same_head = masks
    a_ab, a_ak, a_rb, a_rk = {}, {}, {}, {}
    for u in units:
        ar = jnp.concatenate([tile(at_all, *u), tile(rt_all, *u).astype(BF16)], axis=0)
        bk = jnp.concatenate([block_diag(tile(bt_all, *u)), block_diag(tile(kt_all, *u))], axis=0)
        g2 = _dot_nt(ar, bk)
        a_ab[u] = jnp.where(strict, g2[:CHUNK, :PAIR], 0.0)
        a_ak[u] = jnp.where(strict, g2[:CHUNK, PAIR:], 0.0).astype(BF16)
        a_rb[u] = jnp.where(incl, g2[CHUNK:, :PAIR], 0.0).astype(BF16)
        a_rk[u] = jnp.where(incl, g2[CHUNK:, PAIR:], 0.0).astype(BF16)
    tinv = {u: eye2 + a_ab[u] for u in units}
    apow = {}
    for u in units:
        ab = a_ab[u].astype(BF16)
        apow[u] = _dot(ab, block_diag(ab))
    for step in range(1, NEUMANN_STEPS):
        last = step == NEUMANN_STEPS - 1
        for u in units:
            ab = apow[u].astype(BF16)
            lhs = tinv[u].astype(BF16) if last else jnp.concatenate([tinv[u].astype(BF16), ab], axis=0)
            prod = _dot(lhs, block_diag(ab))
            tinv[u] = tinv[u] + prod[:CHUNK]
            if not last:
                apow[u] = prod[CHUNK:]
    av, y1v, w1, u1, rq, y1, mx, gx = {}, {}, {}, {}, {}, {}, {}, {}
    for u in units:
        avk = _dot(jnp.concatenate([a_ak[u], a_rk[u]], axis=0), block_diag(tile(v_all, *u)))
        av[u], y1v[u] = avk[:CHUNK].astype(BF16), avk[CHUNK:]
    for u in units:
        rhs = jnp.concatenate([block_diag(tile(at_all, *u)), block_diag(av[u])], axis=1)
        wu = _dot(tinv[u].astype(BF16), rhs).astype(BF16)
        w1[u], u1[u] = wu[:, :PAIR], wu[:, PAIR:]
    for u in units:
        vc = tile(v_all, *u)
        bpkp = jnp.concatenate([tile(bp_all, *u), tile(kp_all, *u)], axis=0)
        ry = _dot(a_rb[u], jnp.concatenate([block_diag(w1[u]), block_diag(u1[u])], axis=1))
        rq[u] = (tile(rt_all, *u) + ry[:, :PAIR]).astype(BF16)
        y1[u] = ry[:, PAIR:] + y1v[u]
        lhs = jnp.concatenate([jnp.concatenate([w1[u], u1[u]], axis=1),
                               jnp.concatenate([jnp.zeros_like(vc), vc], axis=1)], axis=0)
        mg = _dot_tn(lhs, bpkp)
        mx[u] = jnp.where(same_head, mg[:PAIR], 0.0).astype(BF16)
        gx[u] = jnp.where(same_head, mg[PAIR:], 0.0)

    y_rows = []
    for c in group_chunks:
        y_parts = []
        for p in range(n_pairs):
            u = (c, p)
            s0 = state_ref[p]
            s0b = s0.astype(BF16)
            y_parts.append(_dot_nt(rq[u], s0b) + y1[u])
            pc = pc_rows[c][:, p * PAIR:(p + 1) * PAIR]
            state_ref[p] = s0 * pc + _dot(s0b, mx[u]) + gx[u]
        y_rows.append(jnp.concatenate(y_parts, axis=1))
    return jnp.concatenate(y_rows, axis=0)


def _rwkv(lidx, ps3, mu, w0, w2p, a0, a2p, g2p, k_k, k_a, r_k, ln_w, ln_b, d_r, tb):
    B, S, d_shift = ps3.shape
    hid = jnp.arange(PAIR) // HEAD_DIM
    bd = (hid[:, None] == hid[None, :]).astype(BF16)
    assert tb % CUMSUM_ROWS == 0
    ci = jnp.arange(CUMSUM_ROWS)
    same = (ci[:, None] // CHUNK) == (ci[None, :] // CHUNK)
    tri = (same & (ci[:, None] >= ci[None, :])).astype(BF16)
    const = lambda a: pl.BlockSpec(a.shape, lambda b, t, l: (0,) * a.ndim)
    vec = _layer_spec((1, d_r), buffered=False)
    lora = _layer_spec((LORA_SLAB, d_r), buffered=False)
    grid_spec = pltpu.PrefetchScalarGridSpec(
        num_scalar_prefetch=1, grid=(B, S // tb),
        in_specs=[pl.BlockSpec((1, tb, d_shift), lambda b, t, l: (b, t, 0)),
                  _layer_spec((1, d_shift), buffered=False),
                  vec, lora, vec, lora, lora, vec, vec, vec, vec, vec,
                  const(bd), const(tri)],
        out_specs=pl.BlockSpec((1, tb, d_r), lambda b, t, l: (b, t, 0)),
        scratch_shapes=[pltpu.VMEM((1, d_shift), F32),
                        pltpu.VMEM((d_r // PAIR, PAIR, PAIR), F32)])
    return pl.pallas_call(
        _rwkv_kernel, grid_spec=grid_spec, name="rwkv7",
        out_shape=jax.ShapeDtypeStruct((B, S, d_r), F32),
        compiler_params=pltpu.CompilerParams(dimension_semantics=("arbitrary", "arbitrary"),
                                             vmem_limit_bytes=V7X_VMEM_LIMIT),
    )(lidx, ps3, mu, w0, w2p, a0, a2p, g2p, k_k, k_a, r_k, ln_w, ln_b, bd, tri)


def _attn_kernel(l_ref, q_ref, k_ref, v_ref, qn_ref, kn_ref, bd_ref, bias_ref, o_ref,
                 q_s, k_s, q4_s, k4_s, v4_s, st_s):
    del l_ref
    S = q_ref.shape[1]
    bd = bd_ref[...]
    inv_n = 1.0 / HEAD_DIM

    def head_rms(t, gain):
        ms = _dot_bf16(t * t, bd) * inv_n
        return t * lax.rsqrt(ms + NORM_EPS) * gain

    q_s[...] = head_rms(q_ref[0], qn_ref[...]) * (HEAD_DIM ** -0.5)
    k_s[...] = head_rms(k_ref[0], kn_ref[...])
    head0 = lax.broadcasted_iota(jnp.int32, (ATTN_BLOCK, PAIR), 1) < HEAD_DIM

    s_pre = S // PRE_DILATION
    v_tok = v_ref.at[0]

    def regroup(i, carry):
        c = i // (s_pre // REGROUP_ROWS)
        r0 = (i % (s_pre // REGROUP_ROWS)) * REGROUP_ROWS
        src = pl.ds(c + PRE_DILATION * r0, REGROUP_ROWS, stride=PRE_DILATION)
        dst = pl.ds(pl.multiple_of(c * s_pre + r0, REGROUP_ROWS), REGROUP_ROWS)
        q4_s[dst, :] = q_s[src, :]
        k4_s[dst, :] = k_s[src, :]
        v4_s[dst, :] = v_tok[src, :]
        return carry

    lax.fori_loop(0, S // REGROUP_ROWS, regroup, 0)

    for pi, (window, dil) in enumerate(DILATED_PATTERNS):
        n_sub = S // dil
        n_blk = n_sub // ATTN_BLOCK
        regrouped = dil % PRE_DILATION == 0
        if regrouped:
            stride = dil // PRE_DILATION
            q_src, k_src, v_src = q4_s, k4_s, v4_s
        else:
            stride = dil
            q_src, k_src, v_src = q_s, k_s, v_tok
        state = tuple(st_s.at[3 * pi + j] for j in range(3))
        span = stride * ATTN_BLOCK

        def group_body(gi, carry):
            blocks = []
            for g in range(ATTN_GROUP):
                i = gi * ATTN_GROUP + g
                n = i // dil
                res = i % dil
                if regrouped:
                    q_start = (res % PRE_DILATION) * s_pre + res // PRE_DILATION + span * n
                else:
                    q_start = res + span * n
                first = jnp.where(n == 0, 1, 0)
                k_start = q_start - span * (1 - first)
                if stride > 1:
                    q_rows = pl.ds(q_start, ATTN_BLOCK, stride=stride)
                    k_rows = pl.ds(k_start, 2 * ATTN_BLOCK, stride=stride)
                else:
                    q_rows = pl.ds(pl.multiple_of(q_start, ATTN_BLOCK), ATTN_BLOCK)
                    k_rows = pl.ds(pl.multiple_of(k_start, ATTN_BLOCK), 2 * ATTN_BLOCK)
                blocks.append(dict(first=first, q_rows=q_rows, k_rows=k_rows))
            for blk in blocks:
                q2 = q_src[blk["q_rows"], :]
                zero = jnp.zeros_like(q2)
                blk["q"] = jnp.concatenate([jnp.where(head0, q2, zero), jnp.where(head0, zero, q2)],
                                           axis=0).astype(BF16)
                blk["k"] = k_src[blk["k_rows"], :].astype(BF16)
                blk["v"] = v_src[blk["k_rows"], :].astype(BF16)
            for blk in blocks:
                blk["s"] = _dot_nt(blk["q"], blk["k"]) + bias_ref[pi, blk["first"]]
            for blk in blocks:
                blk["m"] = jnp.max(blk["s"], axis=-1, keepdims=True)
            for blk in blocks:
                blk["e"] = jnp.exp(blk["s"] - blk["m"])
            for blk in blocks:
                blk["l"] = jnp.sum(blk["e"], axis=-1, keepdims=True)
                blk["o"] = _dot(blk["e"].astype(BF16), blk["v"])
            for blk in blocks:
                pair = lambda t: jnp.where(head0, jnp.broadcast_to(t[:ATTN_BLOCK], (ATTN_BLOCK, PAIR)),
                                           jnp.broadcast_to(t[ATTN_BLOCK:], (ATTN_BLOCK, PAIR)))
                blk["out"] = (pair(blk["o"]), pair(blk["m"]), pair(blk["l"]))
            for blk in blocks:
                for ref, val in zip(state, blk["out"]):
                    ref[blk["q_rows"], :] = val
            return carry

        lax.fori_loop(0, dil * n_blk // ATTN_GROUP, group_body, 0)

    def finish(i, carry):
        c = i // (s_pre // ATTN_BLOCK)
        r0 = (i % (s_pre // ATTN_BLOCK)) * ATTN_BLOCK
        tok = pl.ds(c + PRE_DILATION * r0, ATTN_BLOCK, stride=PRE_DILATION)
        grp = pl.ds(pl.multiple_of(c * s_pre + r0, ATTN_BLOCK), ATTN_BLOCK)
        total = None
        for pi, (_, dil) in enumerate(DILATED_PATTERNS):
            rows = grp if dil % PRE_DILATION == 0 else tok
            part = tuple(st_s[3 * pi + j, rows, :] for j in range(3))
            total = part if total is None else _softmax_merge(total, part)
        o_ref[0, tok, :] = total[0] / total[2]
        return carry

    lax.fori_loop(0, S // ATTN_BLOCK, finish, 0)


def _softmax_merge(a, b):
    acc_a, m_a, l_a = a
    acc_b, m_b, l_b = b
    m_new = jnp.maximum(m_a, m_b)
    w_a = jnp.exp(m_a - m_new)
    w_b = jnp.exp(m_b - m_new)
    return acc_a * w_a + acc_b * w_b, m_new, l_a * w_a + l_b * w_b


def _alibi_slopes(n):
    def pow2(m):
        start = 2.0 ** (-8.0 / m)
        return [start ** (i + 1) for i in range(m)]
    if math.log2(n).is_integer():
        return pow2(n)
    c = 2 ** int(math.floor(math.log2(n)))
    return pow2(c) + pow2(2 * c)[0::2][: n - c]


def _attn_bias(n_heads):
    qi = jnp.arange(ATTN_BLOCK)
    ki = jnp.arange(2 * ATTN_BLOCK)
    dist = qi[:, None] + ATTN_BLOCK - ki[None, :]
    slopes = jnp.asarray(_alibi_slopes(n_heads), F32)
    out = []
    for window, dil in DILATED_PATTERNS:
        valid = (dist >= 0) & (dist <= window // dil)
        bias = -slopes[:, None, None] * (dist * dil).astype(F32)[None]
        rest = jnp.where(valid[None], bias, MASK_VALUE)
        first = jnp.concatenate([rest[..., ATTN_BLOCK:], jnp.full_like(rest[..., ATTN_BLOCK:], MASK_VALUE)], -1)
        out.append(jnp.stack([rest, first], 0).reshape(2, n_heads * ATTN_BLOCK, 2 * ATTN_BLOCK))
    return jnp.stack(out, 0)


def _attention(lidx, qkv3, q_norm, k_norm, d_a):
    B, S, _ = qkv3.shape
    n_heads = d_a // HEAD_DIM
    n_pairs = n_heads // 2
    slab = 2 * HEAD_DIM
    hid = jnp.arange(slab) // HEAD_DIM
    bd = (hid[:, None] == hid[None, :]).astype(BF16)
    bias = _attn_bias(n_heads)
    col = lambda off: pl.BlockSpec((1, S, slab), lambda b, p, l: (b, 0, off + p))
    grid_spec = pltpu.PrefetchScalarGridSpec(
        num_scalar_prefetch=1, grid=(B, n_pairs),
        in_specs=[col(0), col(n_pairs), col(2 * n_pairs),
                  _layer_spec((1, slab), buffered=False), _layer_spec((1, slab), buffered=False),
                  pl.BlockSpec(bd.shape, lambda b, p, l: (0, 0)),
                  pl.BlockSpec((len(DILATED_PATTERNS), 2, 2 * ATTN_BLOCK, 2 * ATTN_BLOCK),
                               lambda b, p, l: (0, 0, p, 0))],
        out_specs=pl.BlockSpec((1, S, slab), lambda b, p, l: (b, 0, p)),
        scratch_shapes=[pltpu.VMEM((S, slab), F32)] * 5
        + [pltpu.VMEM((3 * len(DILATED_PATTERNS), S, slab), F32)])
    return pl.pallas_call(
        _attn_kernel, grid_spec=grid_spec, name="dilated_attn",
        out_shape=jax.ShapeDtypeStruct((B, S, d_a), F32),
        compiler_params=pltpu.CompilerParams(dimension_semantics=("arbitrary", "arbitrary"),
                                             vmem_limit_bytes=V7X_VMEM_LIMIT),
    )(lidx, qkv3, qkv3, qkv3, q_norm, k_norm, bd, bias)


def _pad_lora(w, offset):
    L, r, d = w.shape
    return jnp.zeros((L, LORA_SLAB, d), F32).at[:, offset:offset + r, :].set(w)


def kernel(x, norm_ffn1, ffn1_w_gate, ffn1_w_up, ffn1_w_down, norm_mix, w_in, shift_mu, rwkv_w0, rwkv_w2, rwkv_a0, rwkv_a2, rwkv_g2, rwkv_k_k, rwkv_k_a, rwkv_r_k, rwkv_ln_w, rwkv_ln_b, attn_q_norm, attn_k_norm, conv_dw_w, conv_dw_b, conv_ln_w, conv_ln_b, w_out, norm_ffn2, ffn2_w_gate, ffn2_w_up, ffn2_w_down):
    B, S, D = x.shape
    depth = w_in.shape[0]
    d_r = rwkv_w0.shape[-1]
    d_c = conv_dw_b.shape[-1]
    d_shift = shift_mu.shape[-1]
    d_a = (w_in.shape[-1] - d_shift - 2 * d_c) // 3
    r_w, r_a, r_g = rwkv_w2.shape[1], rwkv_a2.shape[1], rwkv_g2.shape[1]
    assert d_shift == 3 * d_r + LORA_SLAB and r_w + r_a + r_g == LORA_SLAB
    assert S % (2 * ATTN_BLOCK * DILATED_PATTERNS[-1][1]) == 0 and (d_a // HEAD_DIM) % 2 == 0
    assert (S // ATTN_BLOCK) % ATTN_GROUP == 0 and S % (PRE_DILATION * REGROUP_ROWS) == 0
    assert all(d < PRE_DILATION or d % PRE_DILATION == 0 for _, d in DILATED_PATTERNS)
    T = B * S
    assert T % FFN_ROWS == 0 and S % RWKV_ROWS == 0

    vec = lambda a: a.reshape(depth, 1, -1)
    bf = lambda a: a.astype(BF16)
    wg1, wu1, wd1 = bf(ffn1_w_gate), bf(ffn1_w_up), bf(ffn1_w_down)
    wg2, wu2, wd2 = bf(ffn2_w_gate), bf(ffn2_w_up), bf(ffn2_w_down)
    win, wo = bf(w_in), bf(w_out)
    g1, gm, g2 = vec(norm_ffn1), vec(norm_mix), vec(norm_ffn2)
    w2p = _pad_lora(rwkv_w2, 0)
    a2p = _pad_lora(rwkv_a2, r_w)
    g2p = _pad_lora(rwkv_g2, r_w + r_a)
    qn = vec(jnp.tile(attn_q_norm, (1, 2)))
    kn = vec(jnp.tile(attn_k_norm, (1, 2)))

    def layer(l, xf):
        lidx = jnp.reshape(l, (1,)).astype(jnp.int32)
        x1, ps, qkv, u = _ffn_proj(lidx, xf, g1, wg1, wu1, wd1, gm, win, (d_shift, 3 * d_a, 2 * d_c),
                                   FFN_ROWS)
        y_r = _rwkv(lidx, ps.reshape(B, S, d_shift), vec(shift_mu), vec(rwkv_w0), w2p, vec(rwkv_a0),
                    a2p, g2p, vec(rwkv_k_k), vec(rwkv_k_a), vec(rwkv_r_k), vec(rwkv_ln_w),
                    vec(rwkv_ln_b), d_r, RWKV_ROWS)
        y_a = _attention(lidx, qkv.reshape(B, S, 3 * d_a), qn, kn, d_a)
        return _out_ffn(lidx, x1, y_r.reshape(T, d_r), y_a.reshape(T, d_a), u, conv_dw_w,
                        vec(conv_dw_b), vec(conv_ln_w), vec(conv_ln_b), wo, g2, wg2, wu2, wd2,
                        FFN_ROWS, S)

    out = lax.fori_loop(0, depth, layer, x.reshape(T, D))
    return out.reshape(B, S, D)
```

```python
import functools
import math

import jax
import jax.numpy as jnp
from jax import lax
from jax.experimental import pallas as pl
from jax.experimental.pallas import tpu as pltpu

F32 = jnp.float32
BF16 = jnp.bfloat16

LANES = 128
SUBLANES = 8
FFN_ROWS = 512
RWKV_ROWS = 1024
HEAD_DIM = 64
PAIR = 2 * HEAD_DIM
NORM_EPS = 1e-6
RWKV_GN_EPS = 64e-5
CONV_LN_EPS = 1e-5
KK_EPS = 1e-12
CONV_WIDTH = 31
DILATED_PATTERNS = ((128, 1), (512, 4), (2048, 16))
ATTN_BLOCK = 128
ATTN_GROUP = 8
PRE_DILATION = 4
REGROUP_ROWS = 256
LORA_SLAB = 128
CHUNK = 64
GROUP_CHUNKS = 8
CUMSUM_ROWS = 256
NEUMANN_STEPS = 6
MASK_VALUE = -1e30
EXP_NEG_HALF = math.exp(-0.5)
V7X_VMEM_LIMIT = 56 * 1024 * 1024

NT_DIMS = (((1,), (1,)), ((), ()))
TN_DIMS = (((0,), (0,)), ((), ()))


def _dot(a, b):
    return jnp.dot(a, b, preferred_element_type=F32)


def _dot_nt(a, b):
    return lax.dot_general(a, b, NT_DIMS, preferred_element_type=F32)


def _dot_tn(a, b):
    return lax.dot_general(a, b, TN_DIMS, preferred_element_type=F32)


def _split2(x):
    hi = x.astype(BF16)
    lo = (x - hi.astype(F32)).astype(BF16)
    return hi, lo


def _dot_bf16(x, w):
    return _dot(x.astype(BF16), w.astype(BF16))


def _dot_f32_rhs(w_exact, x):
    hi, lo = _split2(x)
    return _dot(w_exact, hi) + _dot(w_exact, lo)


def _dot_f32(x, w):
    xh, xl = _split2(x)
    wh, wl = _split2(w)
    return _dot(xh, wh) + _dot(xl, wh) + _dot(xh, wl)


def _sigmoid(x):
    return 0.5 + 0.5 * jnp.tanh(0.5 * x)


def _rms_norm(x, g):
    return x * lax.rsqrt(jnp.mean(x * x, axis=-1, keepdims=True) + NORM_EPS) * g


def _swiglu_residual(x, g, wg_ref, wu_ref, wd_ref):
    xn = _rms_norm(x, g).astype(BF16)
    gate = _dot(xn, wg_ref[...])
    up = _dot(xn, wu_ref[...])
    h =(gate * _sigmoid(gate) * up).astype(BF16)
    return x + 0.5 * _dot(h, wd_ref[...])


def _ffn_proj_kernel(l_ref, x_ref, g1_ref, wg_ref, wu_ref, wd_ref, gm_ref, win_ref,
                     x1_ref, ps_ref, qkv_ref, u_ref):
    del l_ref
    x1 = _swiglu_residual(x_ref[...], g1_ref[...], wg_ref, wu_ref, wd_ref)
    x1_ref[...] = x1
    h = _rms_norm(x1, gm_ref[...]).astype(BF16)
    proj = _dot(h, win_ref[...])
    d_shift = ps_ref.shape[-1]
    d_qkv = qkv_ref.shape[-1]
    ps_ref[...] = proj[:, :d_shift]
    qkv_ref[...] = proj[:, d_shift:d_shift + d_qkv]
    u_ref[...] = proj[:, d_shift + d_qkv:]


def _layer_spec(shape, buffered=True):
    nd = len(shape)
    kw = dict(pipeline_mode=pl.Buffered(1)) if buffered else {}
    return pl.BlockSpec((None,) + tuple(shape), lambda *a: (a[-1][0],) + (0,) * nd, **kw)


def _ffn_proj(lidx, x, g1, wg, wu, wd, gm, win, dims, tm):
    T, D = x.shape
    d_shift, d_qkv, d_u = dims
    row = lambda w: pl.BlockSpec((tm, w), lambda i, l: (i, 0))
    grid_spec = pltpu.PrefetchScalarGridSpec(
        num_scalar_prefetch=1, grid=(T // tm,),
        in_specs=[row(D), _layer_spec((1, D)), _layer_spec(wg.shape[1:]), _layer_spec(wu.shape[1:]),
                  _layer_spec(wd.shape[1:]), _layer_spec((1, D)), _layer_spec(win.shape[1:])],
        out_specs=[row(D), row(d_shift), row(d_qkv), row(d_u)])
    return pl.pallas_call(
        _ffn_proj_kernel, grid_spec=grid_spec, name="ffn_proj",
        out_shape=[jax.ShapeDtypeStruct((T, D), F32), jax.ShapeDtypeStruct((T, d_shift), F32),
                   jax.ShapeDtypeStruct((T, d_qkv), F32), jax.ShapeDtypeStruct((T, d_u), F32)],
        compiler_params=pltpu.CompilerParams(dimension_semantics=("arbitrary",),
                                             vmem_limit_bytes=V7X_VMEM_LIMIT),
    )(lidx, x, g1, wg, wu, wd, gm, win)


CONV_PAD = 32
CONV_ROWS = 64


def _conv_rows(z_ref, r0, w_ref, b_ref, lnw_ref, lnb_ref):
    d_c = z_ref.shape[-1]
    shift = CONV_PAD - (CONV_WIDTH - 1)
    n_win = CONV_ROWS + CONV_PAD
    win = z_ref[r0:r0 + n_win, :]
    acc = jnp.zeros((CONV_ROWS, d_c), F32) + b_ref[...]
    for sub in range(SUBLANES):
        rolled = win if sub == 0 else pltpu.roll(win, n_win - sub, 0)
        for j in range(CONV_WIDTH):
            off = shift + j
            if off % SUBLANES == sub:
                base = off - sub
                acc = acc + rolled[base:base + CONV_ROWS, :] * w_ref[j:j + 1, :]
    mean = jnp.mean(acc, axis=-1, keepdims=True)
    d = acc - mean
    var = jnp.mean(d * d, axis=-1, keepdims=True)
    z = d * lax.rsqrt(var + CONV_LN_EPS) * lnw_ref[...] + lnb_ref[...]
    return z * _sigmoid(z)


def _out_ffn_kernel(tiles_per_seq, l_ref, x_ref, yr_ref, ya_ref, u_ref, cw_ref, cb_ref, clnw_ref,
                    clnb_ref, wo_ref, g2_ref, wg_ref, wu_ref, wd_ref, o_ref, yc_s, z_s):
    del l_ref
    i = pl.program_id(0)
    tm = x_ref.shape[0]
    d_r = yr_ref.shape[-1]
    d_a = ya_ref.shape[-1]
    d_c = yc_s.shape[-1]

    @pl.when(i == 0)
    def _():
        yc_s[...] = jnp.zeros_like(yc_s)
        z_s[...] = jnp.zeros_like(z_s)

    yc_prev = yc_s[...].astype(BF16)

    left = z_s[tm:tm + CONV_PAD, :]
    z_s[0:CONV_PAD, :] = jnp.where(i % tiles_per_seq == 0, jnp.zeros_like(left), left)
    z_s[CONV_PAD:, :] = u_ref[:, 0:d_c] * _sigmoid(u_ref[:, d_c:])
    for r0 in range(0, tm, CONV_ROWS):
        yc_s[r0:r0 + CONV_ROWS, :] = _conv_rows(z_s, r0, cw_ref, cb_ref, clnw_ref, clnb_ref)

    x2 = (x_ref[...]
          + _dot(yr_ref[...].astype(BF16), wo_ref[0:d_r, :])
          + _dot(ya_ref[...].astype(BF16), wo_ref[d_r:d_r + d_a, :])
          + _dot(yc_prev, wo_ref[d_r + d_a:, :]))
    o_ref[...] = _swiglu_residual(x2, g2_ref[...], wg_ref, wu_ref, wd_ref)


def _out_ffn(lidx, x, yr, ya, u, cw, cb, clnw, clnb, wo, g2, wg, wu, wd, tm, seq_len):
    T, D = x.shape
    d_c = u.shape[1] // 2
    n_tiles = T // tm
    assert seq_len % tm == 0 and tm % CONV_ROWS == 0
    lag = lambda w: pl.BlockSpec((tm, w), lambda i, l: (jnp.maximum(i - 1, 0), 0))
    grid_spec = pltpu.PrefetchScalarGridSpec(
        num_scalar_prefetch=1, grid=(n_tiles + 1,),
        in_specs=[lag(D), lag(yr.shape[1]), lag(ya.shape[1]),
                  pl.BlockSpec((tm, 2 * d_c), lambda i, l: (jnp.minimum(i, n_tiles - 1), 0)),
                  _layer_spec((CONV_WIDTH, d_c), buffered=False), _layer_spec((1, d_c), buffered=False),
                  _layer_spec((1, d_c), buffered=False), _layer_spec((1, d_c), buffered=False),
                  _layer_spec(wo.shape[1:]), _layer_spec((1, D)), _layer_spec(wg.shape[1:]),
                  _layer_spec(wu.shape[1:]), _layer_spec(wd.shape[1:])],
        out_specs=lag(D),
        scratch_shapes=[pltpu.VMEM((tm, d_c), F32), pltpu.VMEM((CONV_PAD + tm, d_c), F32)])
    return pl.pallas_call(
        functools.partial(_out_ffn_kernel, seq_len // tm), grid_spec=grid_spec, name="out_ffn",
        out_shape=jax.ShapeDtypeStruct((T, D), F32),
        compiler_params=pltpu.CompilerParams(dimension_semantics=("arbitrary",),
                                             vmem_limit_bytes=V7X_VMEM_LIMIT),
    )(lidx, x, yr, ya, u, cw, cb, clnw, clnb, wo, g2, wg, wu, wd)


def _rwkv_kernel(l_ref, p_ref, mu_ref, w0_ref, w2_ref, a0_ref, a2_ref, g2_ref, kk_ref, ka_ref,
                 rk_ref, lnw_ref, lnb_ref, bd_ref, tri_ref, o_ref,
                 carry_ref, state_ref):
    del l_ref
    tb = p_ref.shape[1]
    d_r = o_ref.shape[-1]

    @pl.when(pl.program_id(1) == 0)
    def _():
        carry_ref[...] = jnp.zeros_like(carry_ref)
        state_ref[...] = jnp.zeros_like(state_ref)

    p = p_ref[0]
    row = lax.broadcasted_iota(jnp.int32, (tb, 1), 0)
    prev = jnp.where(row == 0, carry_ref[...], pltpu.roll(p, 1, 0))
    carry_ref[...] = p[tb - 1:tb, :]
    ps = p + (prev - p) * mu_ref[...]

    r = ps[:, 0:d_r]
    k = ps[:, d_r:2 * d_r]
    v = ps[:, 2 * d_r:3 * d_r]
    xs = ps[:, 3 * d_r:3 * d_r + LORA_SLAB]

    bd = bd_ref[...]
    def seg_sum(t):
        tb16 = t.astype(BF16)
        return jnp.concatenate([_dot(tb16[:, s:s + PAIR], bd) for s in range(0, d_r, PAIR)], axis=1)

    lw = w0_ref[...] + _dot_f32(jnp.tanh(xs), w2_ref[...])
    ld = (-0.5 * EXP_NEG_HALF) * jnp.tanh(0.5 * lw) - 0.5 * EXP_NEG_HALF
    a = _sigmoid(a0_ref[...] + _dot_bf16(xs, a2_ref[...]))
    g = _dot_bf16(_sigmoid(xs), g2_ref[...])
    kk = k * kk_ref[...]
    kk = kk * lax.rsqrt(seg_sum(kk * kk) + KK_EPS)
    ka = ka_ref[...]
    k2 = k * ((1.0 - ka) + a * ka)
    b = kk * a

    lcum = jnp.concatenate(
        [_dot_f32_rhs(tri_ref[...], ld[s:s + CUMSUM_ROWS]) for s in range(0, tb, CUMSUM_ROWS)], axis=0)
    n_chunks = tb // CHUNK
    pc_rows = [jnp.exp(lcum[(c + 1) * CHUNK - 1:(c + 1) * CHUNK, :]) for c in range(n_chunks)]
    at_all = (-kk * jnp.exp(lcum - ld)).astype(BF16)
    rt_all = r * jnp.exp(lcum)
    inv_p = jnp.exp(-lcum)
    bt_all = (b * inv_p).astype(BF16)
    kt_all = (k2 * inv_p).astype(BF16)
    rest = inv_p * jnp.concatenate([jnp.broadcast_to(pc, (CHUNK, d_r)) for pc in pc_rows], axis=0)
    bp_all = (b * rest).astype(BF16)
    kp_all = (k2 * rest).astype(BF16)
    v_all = v.astype(BF16)

    lane = lax.broadcasted_iota(jnp.int32, (CHUNK, PAIR), 1)
    trow = lax.broadcasted_iota(jnp.int32, (CHUNK, PAIR), 0)
    head0 = lane < HEAD_DIM
    scol = lane & (HEAD_DIM - 1)
    strict = trow > scol
    incl = trow >= scol
    eye2 = jnp.where(trow == scol, 1.0, 0.0)
    r2 = lax.broadcasted_iota(jnp.int32, (PAIR, PAIR), 0)
    c2 = lax.broadcasted_iota(jnp.int32, (PAIR, PAIR), 1)
    same_head = (r2 < HEAD_DIM) == (c2 < HEAD_DIM)

    def block_diag(xb):
        zero = jnp.zeros_like(xb)
        return jnp.concatenate([jnp.where(head0, xb, zero), jnp.where(head0, zero, xb)], axis=0)

    n_pairs = d_r // PAIR
    tile = lambda t, c, p: t[c * CHUNK:(c + 1) * CHUNK, p * PAIR:(p + 1) * PAIR]
    inv_n = 1.0 / HEAD_DIM

    for c_lo in range(0, n_chunks, GROUP_CHUNKS):
        group_chunks = range(c_lo, min(c_lo + GROUP_CHUNKS, n_chunks))
        units = [(c, p) for c in group_chunks for p in range(n_pairs)]
        y = _rwkv_group(units, group_chunks, n_pairs, tile, block_diag, state_ref, pc_rows,
                        (at_all, rt_all, bt_all, kt_all, bp_all, kp_all, v_all),
                        (strict, incl, eye2, same_head))
        rows = slice(group_chunks[0] * CHUNK, (group_chunks[-1] + 1) * CHUNK)
        mean = seg_sum(y) * inv_n
        d = y - mean
        var = seg_sum(d * d) * inv_n
        yn = d * lax.rsqrt(var + RWKV_GN_EPS) * lnw_ref[...] + lnb_ref[...]
        bonus = seg_sum(r[rows] * k2[rows] * rk_ref[...]) * v[rows]
        o_ref[0, rows, :] = (yn + bonus) * g[rows]


def _rwkv_group(units, group_chunks, n_pairs, tile, block_diag, state_ref, pc_rows, scaled, masks):
    at_all, rt_all, bt_all, kt_all, bp_all, kp_all, v_all = scaled
    strict, incl, eye2, same_head = masks
    a_ab, a_ak, a_rb, a_rk = {}, {}, {}, {}
    for u in units:
        ar = jnp.concatenate([tile(at_all, *u), tile(rt_all, *u).astype(BF16)], axis=0)
        bk = jnp.concatenate([block_diag(tile(bt_all, *u)), block_diag(tile(kt_all, *u))], axis=0)
        g2 = _dot_nt(ar, bk)
        a_ab[u] = jnp.where(strict, g2[:CHUNK, :PAIR], 0.0)
        a_ak[u] = jnp.where(strict, g2[:CHUNK, PAIR:], 0.0).astype(BF16)
        a_rb[u] = jnp.where(incl, g2[CHUNK:, :PAIR], 0.0).astype(BF16)
        a_rk[u] = jnp.where(incl, g2[CHUNK:, PAIR:], 0.0).astype(BF16)
    tinv = {u: eye2 + a_ab[u] for u in units}
    apow = {}
    for u in units:
        ab = a_ab[u].astype(BF16)
        apow[u] = _dot(ab, block_diag(ab))
    for step in range(1, NEUMANN_STEPS):
        last = step == NEUMANN_STEPS - 1
        for u in units:
            ab = apow[u].astype(BF16)
            lhs = tinv[u].astype(BF16) if last else jnp.concatenate([tinv[u].astype(BF16), ab], axis=0)
            prod = _dot(lhs, block_diag(ab))
            tinv[u] = tinv[u] + prod[:CHUNK]
            if not last:
                apow[u] = prod[CHUNK:]
    av, y1v, w1, u1, rq, y1, mx, gx = {}, {}, {}, {}, {}, {}, {}, {}
    for u in units:
        avk = _dot(jnp.concatenate([a_ak[u], a_rk[u]], axis=0), block_diag(tile(v_all, *u)))
        av[u], y1v[u] = avk[:CHUNK].astype(BF16), avk[CHUNK:]
    for u in units:
        rhs = jnp.concatenate([block_diag(tile(at_all, *u)), block_diag(av[u])], axis=1)
        wu = _dot(tinv[u].astype(BF16), rhs).astype(BF16)
        w1[u], u1[u] = wu[:, :PAIR], wu[:, PAIR:]
    for u in units:
        vc = tile(v_all, *u)
        bpkp = jnp.concatenate([tile(bp_all, *u), tile(kp_all, *u)], axis=0)
        ry = _dot(a_rb[u], jnp.concatenate([block_diag(w1[u]), block_diag(u1[u])], axis=1))
        rq[u] = (tile(rt_all, *u) + ry[:, :PAIR]).astype(BF16)
        y1[u] = ry[:, PAIR:] + y1v[u]
        lhs = jnp.concatenate([jnp.concatenate([w1[u], u1[u]], axis=1),
                               jnp.concatenate([jnp.zeros_like(vc), vc], axis=1)], axis=0)
        mg = _dot_tn(lhs, bpkp)
        mx[u] = jnp.where(same_head, mg[:PAIR], 0.0).astype(BF16)
        gx[u] = jnp.where(same_head, mg[PAIR:], 0.0)

    y_rows = []
    for c in group_chunks:
        y_parts = []
        for p in range(n_pairs):
            u = (c, p)
            s0 = state_ref[p]
            s0b = s0.astype(BF16)
            y_parts.append(_dot_nt(rq[u], s0b) + y1[u])
            pc = pc_rows[c][:, p * PAIR:(p + 1) * PAIR]
            state_ref[p] = s0 * pc + _dot(s0b, mx[u]) + gx[u]
        y_rows.append(jnp.concatenate(y_parts, axis=1))
    return jnp.concatenate(y_rows, axis=0)


def _rwkv(lidx, ps3, mu, w0, w2p, a0, a2p, g2p, k_k, k_a, r_k, ln_w, ln_b, d_r, tb):
    B, S, d_shift = ps3.shape
    hid = jnp.arange(PAIR) // HEAD_DIM
    bd = (hid[:, None] == hid[None, :]).astype(BF16)
    assert tb % CUMSUM_ROWS == 0
    ci = jnp.arange(CUMSUM_ROWS)
    same = (ci[:, None] // CHUNK) == (ci[None, :] // CHUNK)
    tri = (same & (ci[:, None] >= ci[None, :])).astype(BF16)
    const = lambda a: pl.BlockSpec(a.shape, lambda b, t, l: (0,) * a.ndim)
    vec = _layer_spec((1, d_r), buffered=False)
    lora = _layer_spec((LORA_SLAB, d_r), buffered=False)
    grid_spec = pltpu.PrefetchScalarGridSpec(
        num_scalar_prefetch=1, grid=(B, S // tb),
        in_specs=[pl.BlockSpec((1, tb, d_shift), lambda b, t, l: (b, t, 0)),
                  _layer_spec((1, d_shift), buffered=False),
                  vec, lora, vec, lora, lora, vec, vec, vec, vec, vec,
                  const(bd), const(tri)],
        out_specs=pl.BlockSpec((1, tb, d_r), lambda b, t, l: (b, t, 0)),
        scratch_shapes=[pltpu.VMEM((1, d_shift), F32),
                        pltpu.VMEM((d_r // PAIR, PAIR, PAIR), F32)])
    return pl.pallas_call(
        _rwkv_kernel, grid_spec=grid_spec, name="rwkv7",
        out_shape=jax.ShapeDtypeStruct((B, S, d_r), F32),
        compiler_params=pltpu.CompilerParams(dimension_semantics=("arbitrary", "arbitrary"),
                                             vmem_limit_bytes=V7X_VMEM_LIMIT),
    )(lidx, ps3, mu, w0, w2p, a0, a2p, g2p, k_k, k_a, r_k, ln_w, ln_b, bd, tri)


def _attn_kernel(l_ref, q_ref, k_ref, v_ref, qn_ref, kn_ref, bd_ref, bias_ref, o_ref,
                 q_s, k_s, q4_s, k4_s, v4_s, st_s):
    del l_ref
    S = q_ref.shape[1]
    bd = bd_ref[...]
    inv_n = 1.0 / HEAD_DIM

    def head_rms(t, gain):
        ms = _dot_bf16(t * t, bd) * inv_n
        return t * lax.rsqrt(ms + NORM_EPS) * gain

    q_s[...] = head_rms(q_ref[0], qn_ref[...]) * (HEAD_DIM ** -0.5)
    k_s[...] = head_rms(k_ref[0], kn_ref[...])
    head0 = lax.broadcasted_iota(jnp.int32, (ATTN_BLOCK, PAIR), 1) < HEAD_DIM

    s_pre = S // PRE_DILATION
    v_tok = v_ref.at[0]

    def regroup(i, carry):
        c = i // (s_pre // REGROUP_ROWS)
        r0 = (i % (s_pre // REGROUP_ROWS)) * REGROUP_ROWS
        src = pl.ds(c + PRE_DILATION * r0, REGROUP_ROWS, stride=PRE_DILATION)
        dst = pl.ds(pl.multiple_of(c * s_pre + r0, REGROUP_ROWS), REGROUP_ROWS)
        q4_s[dst, :] = q_s[src, :]
        k4_s[dst, :] = k_s[src, :]
        v4_s[dst, :] = v_tok[src, :]
        return carry

    lax.fori_loop(0, S // REGROUP_ROWS, regroup, 0)

    for pi, (window, dil) in enumerate(DILATED_PATTERNS):
        n_sub = S // dil
        n_blk = n_sub // ATTN_BLOCK
        regrouped = dil % PRE_DILATION == 0
        if regrouped:
            stride = dil // PRE_DILATION
            q_src, k_src, v_src = q4_s, k4_s, v4_s
        else:
            stride = dil
            q_src, k_src, v_src = q_s, k_s, v_tok
        state = tuple(st_s.at[3 * pi + j] for j in range(3))
        span = stride * ATTN_BLOCK

        def group_body(gi, carry):
            blocks = []
            for g in range(ATTN_GROUP):
                i = gi * ATTN_GROUP + g
                n = i // dil
                res = i % dil
                if regrouped:
                    q_start = (res % PRE_DILATION) * s_pre + res // PRE_DILATION + span * n
                else:
                    q_start = res + span * n
                first = jnp.where(n == 0, 1, 0)
                k_start = q_start - span * (1 - first)
                if stride > 1:
                    q_rows = pl.ds(q_start, ATTN_BLOCK, stride=stride)
                    k_rows = pl.ds(k_start, 2 * ATTN_BLOCK, stride=stride)
                else:
                    q_rows = pl.ds(pl.multiple_of(q_start, ATTN_BLOCK), ATTN_BLOCK)
                    k_rows = pl.ds(pl.multiple_of(k_start, ATTN_BLOCK), 2 * ATTN_BLOCK)
                blocks.append(dict(first=first, q_rows=q_rows, k_rows=k_rows))
            for blk in blocks:
                q2 = q_src[blk["q_rows"], :]
                zero = jnp.zeros_like(q2)
                blk["q"] = jnp.concatenate([jnp.where(head0, q2, zero), jnp.where(head0, zero, q2)],
                                           axis=0).astype(BF16)
                blk["k"] = k_src[blk["k_rows"], :].astype(BF16)
                blk["v"] = v_src[blk["k_rows"], :].astype(BF16)
            for blk in blocks:
                blk["s"] = _dot_nt(blk["q"], blk["k"]) + bias_ref[pi, blk["first"]]
            for blk in blocks:
                blk["m"] = jnp.max(blk["s"], axis=-1, keepdims=True)
            for blk in blocks:
                blk["e"] = jnp.exp(blk["s"] - blk["m"])
            for blk in blocks:
                blk["l"] = jnp.sum(blk["e"], axis=-1, keepdims=True)
                blk["o"] = _dot(blk["e"].astype(BF16), blk["v"])
            for blk in blocks:
                pair = lambda t: jnp.where(head0, jnp.broadcast_to(t[:ATTN_BLOCK], (ATTN_BLOCK, PAIR)),
                                           jnp.broadcast_to(t[ATTN_BLOCK:], (ATTN_BLOCK, PAIR)))
                blk["out"] = (pair(blk["o"]), pair(blk["m"]), pair(blk["l"]))
            for blk in blocks:
                for ref, val in zip(state, blk["out"]):
                    ref[blk["q_rows"], :] = val
            return carry

        lax.fori_loop(0, dil * n_blk // ATTN_GROUP, group_body, 0)

    def finish(i, carry):
        c = i // (s_pre // ATTN_BLOCK)
        r0 = (i % (s_pre // ATTN_BLOCK)) * ATTN_BLOCK
        tok = pl.ds(c + PRE_DILATION * r0, ATTN_BLOCK, stride=PRE_DILATION)
        grp = pl.ds(pl.multiple_of(c * s_pre + r0, ATTN_BLOCK), ATTN_BLOCK)
        total = None
        for pi, (_, dil) in enumerate(DILATED_PATTERNS):
            rows = grp if dil % PRE_DILATION == 0 else tok
            part = tuple(st_s[3 * pi + j, rows, :] for j in range(3))
            total = part if total is None else _softmax_merge(total, part)
        o_ref[0, tok, :] = total[0] / total[2]
        return carry

    lax.fori_loop(0, S // ATTN_BLOCK, finish, 0)


def _softmax_merge(a, b):
    acc_a, m_a, l_a = a
    acc_b, m_b, l_b = b
    m_new = jnp.maximum(m_a, m_b)
    w_a = jnp.exp(m_a - m_new)
    w_b = jnp.exp(m_b - m_new)
    return acc_a * w_a + acc_b * w_b, m_new, l_a * w_a + l_b * w_b


def _alibi_slopes(n):
    def pow2(m):
        start = 2.0 ** (-8.0 / m)
        return [start ** (i + 1) for i in range(m)]
    if math.log2(n).is_integer():
        return pow2(n)
    c = 2 ** int(math.floor(math.log2(n)))
    return pow2(c) + pow2(2 * c)[0::2][: n - c]


def _attn_bias(n_heads):
    qi = jnp.arange(ATTN_BLOCK)
    ki = jnp.arange(2 * ATTN_BLOCK)
    dist = qi[:, None] + ATTN_BLOCK - ki[None, :]
    slopes = jnp.asarray(_alibi_slopes(n_heads), F32)
    out = []
    for window, dil in DILATED_PATTERNS:
        valid = (dist >= 0) & (dist <= window // dil)
        bias = -slopes[:, None, None] * (dist * dil).astype(F32)[None]
        rest = jnp.where(valid[None], bias, MASK_VALUE)
        first = jnp.concatenate([rest[..., ATTN_BLOCK:], jnp.full_like(rest[..., ATTN_BLOCK:], MASK_VALUE)], -1)
        out.append(jnp.stack([rest, first], 0).reshape(2, n_heads * ATTN_BLOCK, 2 * ATTN_BLOCK))
    return jnp.stack(out, 0)


def _attention(lidx, qkv3, q_norm, k_norm, d_a):
    B, S, _ = qkv3.shape
    n_heads = d_a // HEAD_DIM
    n_pairs = n_heads // 2
    slab = 2 * HEAD_DIM
    hid = jnp.arange(slab) // HEAD_DIM
    bd = (hid[:, None] == hid[None, :]).astype(BF16)
    bias = _attn_bias(n_heads)
    col = lambda off: pl.BlockSpec((1, S, slab), lambda b, p, l: (b, 0, off + p))
    grid_spec = pltpu.PrefetchScalarGridSpec(
        num_scalar_prefetch=1, grid=(B, n_pairs),
        in_specs=[col(0), col(n_pairs), col(2 * n_pairs),
                  _layer_spec((1, slab), buffered=False), _layer_spec((1, slab), buffered=False),
                  pl.BlockSpec(bd.shape, lambda b, p, l: (0, 0)),
                  pl.BlockSpec((len(DILATED_PATTERNS), 2, 2 * ATTN_BLOCK, 2 * ATTN_BLOCK),
                               lambda b, p, l: (0, 0, p, 0))],
        out_specs=pl.BlockSpec((1, S, slab), lambda b, p, l: (b, 0, p)),
        scratch_shapes=[pltpu.VMEM((S, slab), F32)] * 5
        + [pltpu.VMEM((3 * len(DILATED_PATTERNS), S, slab), F32)])
    return pl.pallas_call(
        _attn_kernel, grid_spec=grid_spec, name="dilated_attn",
        out_shape=jax.ShapeDtypeStruct((B, S, d_a), F32),
        compiler_params=pltpu.CompilerParams(dimension_semantics=("arbitrary", "arbitrary"),
                                             vmem_limit_bytes=V7X_VMEM_LIMIT),
    )(lidx, qkv3, qkv3, qkv3, q_norm, k_norm, bd, bias)


def _pad_lora(w, offset):
    L, r, d = w.shape
    return jnp.zeros((L, LORA_SLAB, d), F32).at[:, offset:offset + r, :].set(w)


def kernel(x, norm_ffn1, ffn1_w_gate, ffn1_w_up, ffn1_w_down, norm_mix, w_in, shift_mu, rwkv_w0, rwkv_w2, rwkv_a0, rwkv_a2, rwkv_g2, rwkv_k_k, rwkv_k_a, rwkv_r_k, rwkv_ln_w, rwkv_ln_b, attn_q_norm, attn_k_norm, conv_dw_w, conv_dw_b, conv_ln_w, conv_ln_b, w_out, norm_ffn2, ffn2_w_gate, ffn2_w_up, ffn2_w_down):
    B, S, D = x.shape
    depth = w_in.shape[0]
    d_r = rwkv_w0.shape[-1]
    d_c = conv_dw_b.shape[-1]
    d_shift = shift_mu.shape[-1]
    d_a = (w_in.shape[-1] - d_shift - 2 * d_c) // 3
    r_w, r_a, r_g = rwkv_w2.shape[1], rwkv_a2.shape[1], rwkv_g2.shape[1]
    assert d_shift == 3 * d_r + LORA_SLAB and r_w + r_a + r_g == LORA_SLAB
    assert S % (2 * ATTN_BLOCK * DILATED_PATTERNS[-1][1]) == 0 and (d_a // HEAD_DIM) % 2 == 0
    assert (S // ATTN_BLOCK) % ATTN_GROUP == 0 and S % (PRE_DILATION * REGROUP_ROWS) == 0
    assert all(d < PRE_DILATION or d % PRE_DILATION == 0 for _, d in DILATED_PATTERNS)
    T = B * S
    assert T % FFN_ROWS == 0 and S % RWKV_ROWS == 0

    vec = lambda a: a.reshape(depth, 1, -1)
    bf = lambda a: a.astype(BF16)
    wg1, wu1, wd1 = bf(ffn1_w_gate), bf(ffn1_w_up), bf(ffn1_w_down)
    wg2, wu2, wd2 = bf(ffn2_w_gate), bf(ffn2_w_up), bf(ffn2_w_down)
    win, wo = bf(w_in), bf(w_out)
    g1, gm, g2 = vec(norm_ffn1), vec(norm_mix), vec(norm_ffn2)
    w2p = _pad_lora(rwkv_w2, 0)
    a2p = _pad_lora(rwkv_a2, r_w)
    g2p = _pad_lora(rwkv_g2, r_w + r_a)
    qn = vec(jnp.tile(attn_q_norm, (1, 2)))
    kn = vec(jnp.tile(attn_k_norm, (1, 2)))

    layer_index = lambda l: jnp.reshape(l, (1,)).astype(jnp.int32)

    def project(l, xf):
        return _ffn_proj(layer_index(l), xf, g1, wg1, wu1, wd1, gm, win, (d_shift, 3 * d_a, 2 * d_c),
                         FFN_ROWS)

    def mix(l, projected):
        lidx = layer_index(l)
        x1, ps, qkv, u = projected
        y_r = _rwkv(lidx, ps.reshape(B, S, d_shift), vec(shift_mu), vec(rwkv_w0), w2p, vec(rwkv_a0),
                    a2p, g2p, vec(rwkv_k_k), vec(rwkv_k_a), vec(rwkv_r_k), vec(rwkv_ln_w),
                    vec(rwkv_ln_b), d_r, RWKV_ROWS)
        y_a = _attention(lidx, qkv.reshape(B, S, 3 * d_a), qn, kn, d_a)
        return _out_ffn(lidx, x1, y_r.reshape(T, d_r), y_a.reshape(T, d_a), u, conv_dw_w,
                        vec(conv_dw_b), vec(conv_ln_w), vec(conv_ln_b), wo, g2, wg2, wu2, wd2,
                        FFN_ROWS, S)

    state = project(jnp.int32(0), x.reshape(T, D))
    state = lax.fori_loop(0, depth - 1, lambda l, st: tuple(project(l + 1, mix(l, st))), tuple(state))
    return mix(jnp.int32(depth - 1), state).reshape(B, S, D)
```

```python
import functools
import math

import jax
import jax.numpy as jnp
from jax import lax
from jax.experimental import pallas as pl
from jax.experimental.pallas import tpu as pltpu

F32 = jnp.float32
BF16 = jnp.bfloat16

LANES = 128
SUBLANES = 8
FFN_ROWS = 512
RWKV_ROWS = 1024
HEAD_DIM = 64
PAIR = 2 * HEAD_DIM
NORM_EPS = 1e-6
RWKV_GN_EPS = 64e-5
CONV_LN_EPS = 1e-5
KK_EPS = 1e-12
CONV_WIDTH = 31
DILATED_PATTERNS = ((128, 1), (512, 4), (2048, 16))
ATTN_BLOCK = 128
ATTN_GROUP = 8
PRE_DILATION = 4
REGROUP_ROWS = 256
LORA_SLAB = 128
CHUNK = 64
GROUP_CHUNKS = 8
CUMSUM_ROWS = 256
NEUMANN_STEPS = 6
MASK_VALUE = -1e30
SAFE_SOFTMAX_SHIFT = 40.0
EXP_NEG_HALF = math.exp(-0.5)
V7X_VMEM_LIMIT = 56 * 1024 * 1024

NT_DIMS = (((1,), (1,)), ((), ()))
TN_DIMS = (((0,), (0,)), ((), ()))


def _dot(a, b):
    return jnp.dot(a, b, preferred_element_type=F32)


def _dot_nt(a, b):
    return lax.dot_general(a, b, NT_DIMS, preferred_element_type=F32)


def _dot_tn(a, b):
    return lax.dot_general(a, b, TN_DIMS, preferred_element_type=F32)


def _split2(x):
    hi = x.astype(BF16)
    lo = (x - hi.astype(F32)).astype(BF16)
    return hi, lo


def _dot_bf16(x, w):
    return _dot(x.astype(BF16), w.astype(BF16))


def _dot_f32_rhs(w_exact, x):
    hi, lo = _split2(x)
    return _dot(w_exact, hi) + _dot(w_exact, lo)


def _dot_f32(x, w):
    xh, xl = _split2(x)
    wh, wl = _split2(w)
    return _dot(xh, wh) + _dot(xl, wh) + _dot(xh, wl)


def _sigmoid(x):
    return 0.5 + 0.5 * jnp.tanh(0.5 * x)


def _rms_norm(x, g):
    return x * lax.rsqrt(jnp.mean(x * x, axis=-1, keepdims=True) + NORM_EPS) * g


def _swiglu_residual(x, g, wg_ref, wu_ref, wd_ref):
    xn = _rms_norm(x, g).astype(BF16)
    gate = _dot(xn, wg_ref[...])
    up = _dot(xn, wu_ref[...])
    h =(gate * _sigmoid(gate) * up).astype(BF16)
    return x + 0.5 * _dot(h, wd_ref[...])


def _ffn_proj_kernel(l_ref, x_ref, g1_ref, wg_ref, wu_ref, wd_ref, gm_ref, win_ref,
                     x1_ref, ps_ref, qkv_ref, u_ref):
    del l_ref
    x1 = _swiglu_residual(x_ref[...], g1_ref[...], wg_ref, wu_ref, wd_ref)
    x1_ref[...] = x1
    h = _rms_norm(x1, gm_ref[...]).astype(BF16)
    proj = _dot(h, win_ref[...])
    d_shift = ps_ref.shape[-1]
    d_qkv = qkv_ref.shape[-1]
    ps_ref[...] = proj[:, :d_shift]
    qkv_ref[...] = proj[:, d_shift:d_shift + d_qkv]
    u_ref[...] = proj[:, d_shift + d_qkv:]


def _layer_spec(shape, buffered=True):
    nd = len(shape)
    kw = dict(pipeline_mode=pl.Buffered(1)) if buffered else {}
    return pl.BlockSpec((None,) + tuple(shape), lambda *a: (a[-1][0],) + (0,) * nd, **kw)


def _ffn_proj(lidx, x, g1, wg, wu, wd, gm, win, dims, tm):
    T, D = x.shape
    d_shift, d_qkv, d_u = dims
    row = lambda w: pl.BlockSpec((tm, w), lambda i, l: (i, 0))
    grid_spec = pltpu.PrefetchScalarGridSpec(
        num_scalar_prefetch=1, grid=(T // tm,),
        in_specs=[row(D), _layer_spec((1, D)), _layer_spec(wg.shape[1:]), _layer_spec(wu.shape[1:]),
                  _layer_spec(wd.shape[1:]), _layer_spec((1, D)), _layer_spec(win.shape[1:])],
        out_specs=[row(D), row(d_shift), row(d_qkv), row(d_u)])
    return pl.pallas_call(
        _ffn_proj_kernel, grid_spec=grid_spec, name="ffn_proj",
        out_shape=[jax.ShapeDtypeStruct((T, D), F32), jax.ShapeDtypeStruct((T, d_shift), F32),
                   jax.ShapeDtypeStruct((T, d_qkv), F32), jax.ShapeDtypeStruct((T, d_u), F32)],
        compiler_params=pltpu.CompilerParams(dimension_semantics=("arbitrary",),
                                             vmem_limit_bytes=V7X_VMEM_LIMIT),
    )(lidx, x, g1, wg, wu, wd, gm, win)


CONV_PAD = 32
CONV_ROWS = 64


def _conv_rows(z_ref, r0, w_ref, b_ref, lnw_ref, lnb_ref):
    d_c = z_ref.shape[-1]
    shift = CONV_PAD - (CONV_WIDTH - 1)
    n_win = CONV_ROWS + CONV_PAD
    win = z_ref[r0:r0 + n_win, :]
    acc = jnp.zeros((CONV_ROWS, d_c), F32) + b_ref[...]
    for sub in range(SUBLANES):
        rolled = win if sub == 0 else pltpu.roll(win, n_win - sub, 0)
        for j in range(CONV_WIDTH):
            off = shift + j
            if off % SUBLANES == sub:
                base = off - sub
                acc = acc + rolled[base:base + CONV_ROWS, :] * w_ref[j:j + 1, :]
    mean = jnp.mean(acc, axis=-1, keepdims=True)
    d = acc - mean
    var = jnp.mean(d * d, axis=-1, keepdims=True)
    z = d * lax.rsqrt(var + CONV_LN_EPS) * lnw_ref[...] + lnb_ref[...]
    return z * _sigmoid(z)


def _out_ffn_kernel(tiles_per_seq, l_ref, x_ref, yr_ref, ya_ref, u_ref, cw_ref, cb_ref, clnw_ref,
                    clnb_ref, wo_ref, g2_ref, wg_ref, wu_ref, wd_ref, o_ref, yc_s, z_s):
    del l_ref
    i = pl.program_id(0)
    tm = x_ref.shape[0]
    d_r = yr_ref.shape[-1]
    d_a = ya_ref.shape[-1]
    d_c = yc_s.shape[-1]

    @pl.when(i == 0)
    def _():
        yc_s[...] = jnp.zeros_like(yc_s)
        z_s[...] = jnp.zeros_like(z_s)

    yc_prev = yc_s[...].astype(BF16)

    left = z_s[tm:tm + CONV_PAD, :]
    z_s[0:CONV_PAD, :] = jnp.where(i % tiles_per_seq == 0, jnp.zeros_like(left), left)
    z_s[CONV_PAD:, :] = u_ref[:, 0:d_c] * _sigmoid(u_ref[:, d_c:])
    for r0 in range(0, tm, CONV_ROWS):
        yc_s[r0:r0 + CONV_ROWS, :] = _conv_rows(z_s, r0, cw_ref, cb_ref, clnw_ref, clnb_ref)

    x2 = (x_ref[...]
          + _dot(yr_ref[...].astype(BF16), wo_ref[0:d_r, :])
          + _dot(ya_ref[...].astype(BF16), wo_ref[d_r:d_r + d_a, :])
          + _dot(yc_prev, wo_ref[d_r + d_a:, :]))
    o_ref[...] = _swiglu_residual(x2, g2_ref[...], wg_ref, wu_ref, wd_ref)


def _out_ffn(lidx, x, yr, ya, u, cw, cb, clnw, clnb, wo, g2, wg, wu, wd, tm, seq_len):
    T, D = x.shape
    d_c = u.shape[1] // 2
    n_tiles = T // tm
    assert seq_len % tm == 0 and tm % CONV_ROWS == 0
    lag = lambda w: pl.BlockSpec((tm, w), lambda i, l: (jnp.maximum(i - 1, 0), 0))
    grid_spec = pltpu.PrefetchScalarGridSpec(
        num_scalar_prefetch=1, grid=(n_tiles + 1,),
        in_specs=[lag(D), lag(yr.shape[1]), lag(ya.shape[1]),
                  pl.BlockSpec((tm, 2 * d_c), lambda i, l: (jnp.minimum(i, n_tiles - 1), 0)),
                  _layer_spec((CONV_WIDTH, d_c), buffered=False), _layer_spec((1, d_c), buffered=False),
                  _layer_spec((1, d_c), buffered=False), _layer_spec((1, d_c), buffered=False),
                  _layer_spec(wo.shape[1:]), _layer_spec((1, D)), _layer_spec(wg.shape[1:]),
                  _layer_spec(wu.shape[1:]), _layer_spec(wd.shape[1:])],
        out_specs=lag(D),
        scratch_shapes=[pltpu.VMEM((tm, d_c), F32), pltpu.VMEM((CONV_PAD + tm, d_c), F32)])
    return pl.pallas_call(
        functools.partial(_out_ffn_kernel, seq_len // tm), grid_spec=grid_spec, name="out_ffn",
        out_shape=jax.ShapeDtypeStruct((T, D), F32),
        compiler_params=pltpu.CompilerParams(dimension_semantics=("arbitrary",),
                                             vmem_limit_bytes=V7X_VMEM_LIMIT),
    )(lidx, x, yr, ya, u, cw, cb, clnw, clnb, wo, g2, wg, wu, wd)


def _rwkv_kernel(l_ref, p_ref, mu_ref, w0_ref, w2_ref, a0_ref, a2_ref, g2_ref, kk_ref, ka_ref,
                 rk_ref, lnw_ref, lnb_ref, bd_ref, tri_ref, o_ref,
                 carry_ref, state_ref):
    del l_ref
    tb = p_ref.shape[1]
    d_r = o_ref.shape[-1]

    @pl.when(pl.program_id(1) == 0)
    def _():
        carry_ref[...] = jnp.zeros_like(carry_ref)
        state_ref[...] = jnp.zeros_like(state_ref)

    p = p_ref[0]
    row = lax.broadcasted_iota(jnp.int32, (tb, 1), 0)
    prev = jnp.where(row == 0, carry_ref[...], pltpu.roll(p, 1, 0))
    carry_ref[...] = p[tb - 1:tb, :]
    ps = p + (prev - p) * mu_ref[...]

    r = ps[:, 0:d_r]
    k = ps[:, d_r:2 * d_r]
    v = ps[:, 2 * d_r:3 * d_r]
    xs = ps[:, 3 * d_r:3 * d_r + LORA_SLAB]

    bd = bd_ref[...]
    def seg_sum(t):
        tb16 = t.astype(BF16)
        return jnp.concatenate([_dot(tb16[:, s:s + PAIR], bd) for s in range(0, d_r, PAIR)], axis=1)

    lw = w0_ref[...] + _dot_f32(jnp.tanh(xs), w2_ref[...])
    ld = (-0.5 * EXP_NEG_HALF) * jnp.tanh(0.5 * lw) - 0.5 * EXP_NEG_HALF
    a = _sigmoid(a0_ref[...] + _dot_bf16(xs, a2_ref[...]))
    g = _dot_bf16(_sigmoid(xs), g2_ref[...])
    kk = k * kk_ref[...]
    kk = kk * lax.rsqrt(seg_sum(kk * kk) + KK_EPS)
    ka = ka_ref[...]
    k2 = k * ((1.0 - ka) + a * ka)
    b = kk * a

    lcum = jnp.concatenate(
        [_dot_f32_rhs(tri_ref[...], ld[s:s + CUMSUM_ROWS]) for s in range(0, tb, CUMSUM_ROWS)], axis=0)
    n_chunks = tb // CHUNK
    pc_rows = [jnp.exp(lcum[(c + 1) * CHUNK - 1:(c + 1) * CHUNK, :]) for c in range(n_chunks)]
    at_all = (-kk * jnp.exp(lcum - ld)).astype(BF16)
    rt_all = r * jnp.exp(lcum)
    inv_p = jnp.exp(-lcum)
    bt_all = (b * inv_p).astype(BF16)
    kt_all = (k2 * inv_p).astype(BF16)
    rest = inv_p * jnp.concatenate([jnp.broadcast_to(pc, (CHUNK, d_r)) for pc in pc_rows], axis=0)
    bp_all = (b * rest).astype(BF16)
    kp_all = (k2 * rest).astype(BF16)
    v_all = v.astype(BF16)

    lane = lax.broadcasted_iota(jnp.int32, (CHUNK, PAIR), 1)
    trow = lax.broadcasted_iota(jnp.int32, (CHUNK, PAIR), 0)
    head0 = lane < HEAD_DIM
    scol = lane & (HEAD_DIM - 1)
    strict = trow > scol
    incl = trow >= scol
    eye2 = jnp.where(trow == scol, 1.0, 0.0)
    r2 = lax.broadcasted_iota(jnp.int32, (PAIR, PAIR), 0)
    c2 = lax.broadcasted_iota(jnp.int32, (PAIR, PAIR), 1)
    same_head = (r2 < HEAD_DIM) == (c2 < HEAD_DIM)

    def block_diag(xb):
        zero = jnp.zeros_like(xb)
        return jnp.concatenate([jnp.where(head0, xb, zero), jnp.where(head0, zero, xb)], axis=0)

    n_pairs = d_r // PAIR
    tile = lambda t, c, p: t[c * CHUNK:(c + 1) * CHUNK, p * PAIR:(p + 1) * PAIR]
    inv_n = 1.0 / HEAD_DIM

    for c_lo in range(0, n_chunks, GROUP_CHUNKS):
        group_chunks = range(c_lo, min(c_lo + GROUP_CHUNKS, n_chunks))
        units = [(c, p) for c in group_chunks for p in range(n_pairs)]
        y = _rwkv_group(units, group_chunks, n_pairs, tile, block_diag, state_ref, pc_rows,
                        (at_all, rt_all, bt_all, kt_all, bp_all, kp_all, v_all),
                        (strict, incl, eye2, same_head))
        rows = slice(group_chunks[0] * CHUNK, (group_chunks[-1] + 1) * CHUNK)
        mean = seg_sum(y) * inv_n
        d = y - mean
        var = seg_sum(d * d) * inv_n
        yn = d * lax.rsqrt(var + RWKV_GN_EPS) * lnw_ref[...] + lnb_ref[...]
        bonus = seg_sum(r[rows] * k2[rows] * rk_ref[...]) * v[rows]
        o_ref[0, rows, :] = (yn + bonus) * g[rows]


def _rwkv_group(units, group_chunks, n_pairs, tile, block_diag, state_ref, pc_rows, scaled, masks):
    at_all, rt_all, bt_all, kt_all, bp_all, kp_all, v_all = scaled
    strict, incl, eye2, same_head = masks
    a_ab, a_ak, a_rb, a_rk = {}, {}, {}, {}
    for u in units:
        ar = jnp.concatenate([tile(at_all, *u), tile(rt_all, *u).astype(BF16)], axis=0)
        bk = jnp.concatenate([block_diag(tile(bt_all, *u)), block_diag(tile(kt_all, *u))], axis=0)
        g2 = _dot_nt(ar, bk)
        a_ab[u] = jnp.where(strict, g2[:CHUNK, :PAIR], 0.0)
        a_ak[u] = jnp.where(strict, g2[:CHUNK, PAIR:], 0.0).astype(BF16)
        a_rb[u] = jnp.where(incl, g2[CHUNK:, :PAIR], 0.0).astype(BF16)
        a_rk[u] = jnp.where(incl, g2[CHUNK:, PAIR:], 0.0).astype(BF16)
    tinv = {u: eye2 + a_ab[u] for u in units}
    apow = {}
    for u in units:
        ab = a_ab[u].astype(BF16)
        apow[u] = _dot(ab, block_diag(ab))
    for step in range(1, NEUMANN_STEPS):
        last = step == NEUMANN_STEPS - 1
        for u in units:
            ab = apow[u].astype(BF16)
            lhs = tinv[u].astype(BF16) if last else jnp.concatenate([tinv[u].astype(BF16), ab], axis=0)
            prod = _dot(lhs, block_diag(ab))
            tinv[u] = tinv[u] + prod[:CHUNK]
            if not last:
                apow[u] = prod[CHUNK:]
    av, y1v, w1, u1, rq, y1, mx, gx = {}, {}, {}, {}, {}, {}, {}, {}
    for u in units:
        avk = _dot(jnp.concatenate([a_ak[u], a_rk[u]], axis=0), block_diag(tile(v_all, *u)))
        av[u], y1v[u] = avk[:CHUNK].astype(BF16), avk[CHUNK:]
    for u in units:
        rhs = jnp.concatenate([block_diag(tile(at_all, *u)), block_diag(av[u])], axis=1)
        wu = _dot(tinv[u].astype(BF16), rhs).astype(BF16)
        w1[u], u1[u] = wu[:, :PAIR], wu[:, PAIR:]
    for u in units:
        vc = tile(v_all, *u)
        bpkp = jnp.concatenate([tile(bp_all, *u), tile(kp_all, *u)], axis=0)
        ry = _dot(a_rb[u], jnp.concatenate([block_diag(w1[u]), block_diag(u1[u])], axis=1))
        rq[u] = (tile(rt_all, *u) + ry[:, :PAIR]).astype(BF16)
        y1[u] = ry[:, PAIR:] + y1v[u]
        lhs = jnp.concatenate([jnp.concatenate([w1[u], u1[u]], axis=1),
                               jnp.concatenate([jnp.zeros_like(vc), vc], axis=1)], axis=0)
        mg = _dot_tn(lhs, bpkp)
        mx[u] = jnp.where(same_head, mg[:PAIR], 0.0).astype(BF16)
        gx[u] = jnp.where(same_head, mg[PAIR:], 0.0)

    y_rows = []
    for c in group_chunks:
        y_parts = []
        for p in range(n_pairs):
            u = (c, p)
            s0 = state_ref[p]
            s0b = s0.astype(BF16)
            y_parts.append(_dot_nt(rq[u], s0b) + y1[u])
            pc = pc_rows[c][:, p * PAIR:(p + 1) * PAIR]
            state_ref[p] = s0 * pc + _dot(s0b, mx[u]) + gx[u]
        y_rows.append(jnp.concatenate(y_parts, axis=1))
    return jnp.concatenate(y_rows, axis=0)


def _rwkv(lidx, ps3, mu, w0, w2p, a0, a2p, g2p, k_k, k_a, r_k, ln_w, ln_b, d_r, tb):
    B, S, d_shift = ps3.shape
    hid = jnp.arange(PAIR) // HEAD_DIM
    bd = (hid[:, None] == hid[None, :]).astype(BF16)
    assert tb % CUMSUM_ROWS == 0
    ci = jnp.arange(CUMSUM_ROWS)
    same = (ci[:, None] // CHUNK) == (ci[None, :] // CHUNK)
    tri = (same & (ci[:, None] >= ci[None, :])).astype(BF16)
    const = lambda a: pl.BlockSpec(a.shape, lambda b, t, l: (0,) * a.ndim)
    vec = _layer_spec((1, d_r), buffered=False)
    lora = _layer_spec((LORA_SLAB, d_r), buffered=False)
    grid_spec = pltpu.PrefetchScalarGridSpec(
        num_scalar_prefetch=1, grid=(B, S // tb),
        in_specs=[pl.BlockSpec((1, tb, d_shift), lambda b, t, l: (b, t, 0)),
                  _layer_spec((1, d_shift), buffered=False),
                  vec, lora, vec, lora, lora, vec, vec, vec, vec, vec,
                  const(bd), const(tri)],
        out_specs=pl.BlockSpec((1, tb, d_r), lambda b, t, l: (b, t, 0)),
        scratch_shapes=[pltpu.VMEM((1, d_shift), F32),
                        pltpu.VMEM((d_r // PAIR, PAIR, PAIR), F32)])
    return pl.pallas_call(
        _rwkv_kernel, grid_spec=grid_spec, name="rwkv7",
        out_shape=jax.ShapeDtypeStruct((B, S, d_r), F32),
        compiler_params=pltpu.CompilerParams(dimension_semantics=("arbitrary", "arbitrary"),
                                             vmem_limit_bytes=V7X_VMEM_LIMIT),
    )(lidx, ps3, mu, w0, w2p, a0, a2p, g2p, k_k, k_a, r_k, ln_w, ln_b, bd, tri)


def _attn_kernel(l_ref, exact_ref, q_ref, k_ref, v_ref, qn_ref, kn_ref, bd_ref, bias_ref, o_ref,
                 q_s, k_s, q4_s, k4_s, v4_s, st_s):
    S = q_ref.shape[1]
    bd = bd_ref[...]
    inv_n = 1.0 / HEAD_DIM

    def head_rms(t, gain):
        ms = _dot_bf16(t * t, bd) * inv_n
        return t * lax.rsqrt(ms + NORM_EPS) * gain

    q_s[...] = head_rms(q_ref[0], qn_ref[...]) * (HEAD_DIM ** -0.5)
    k_s[...] = head_rms(k_ref[0], kn_ref[...])
    head0 = lax.broadcasted_iota(jnp.int32, (ATTN_BLOCK, PAIR), 1) < HEAD_DIM

    s_pre = S // PRE_DILATION
    v_tok = v_ref.at[0]

    def regroup(i, carry):
        c = i // (s_pre // REGROUP_ROWS)
        r0 = (i % (s_pre // REGROUP_ROWS)) * REGROUP_ROWS
        src = pl.ds(c + PRE_DILATION * r0, REGROUP_ROWS, stride=PRE_DILATION)
        dst = pl.ds(pl.multiple_of(c * s_pre + r0, REGROUP_ROWS), REGROUP_ROWS)
        q4_s[dst, :] = q_s[src, :]
        k4_s[dst, :] = k_s[src, :]
        v4_s[dst, :] = v_tok[src, :]
        return carry

    lax.fori_loop(0, S // REGROUP_ROWS, regroup, 0)

    exact = exact_ref[l_ref[0]] == 1
    pl.when(exact)(lambda: _attn_patterns(True, S, s_pre, head0, v_tok, bias_ref, o_ref,
                                          q_s, k_s, q4_s, k4_s, v4_s, st_s))
    pl.when(jnp.logical_not(exact))(lambda: _attn_patterns(False, S, s_pre, head0, v_tok, bias_ref, o_ref,
                                                           q_s, k_s, q4_s, k4_s, v4_s, st_s))


def _attn_patterns(row_max, S, s_pre, head0, v_tok, bias_ref, o_ref, q_s, k_s, q4_s, k4_s, v4_s, st_s):
    for pi, (window, dil) in enumerate(DILATED_PATTERNS):
        n_sub = S // dil
        n_blk = n_sub // ATTN_BLOCK
        regrouped = dil % PRE_DILATION == 0
        if regrouped:
            stride = dil // PRE_DILATION
            q_src, k_src, v_src = q4_s, k4_s, v4_s
        else:
            stride = dil
            q_src, k_src, v_src = q_s, k_s, v_tok
        state = tuple(st_s.at[3 * pi + j] for j in range(3))
        span = stride * ATTN_BLOCK

        def group_body(gi, carry):
            blocks = []
            for g in range(ATTN_GROUP):
                i = gi * ATTN_GROUP + g
                n = i // dil
                res = i % dil
                if regrouped:
                    q_start = (res % PRE_DILATION) * s_pre + res // PRE_DILATION + span * n
                else:
                    q_start = res + span * n
                first = jnp.where(n == 0, 1, 0)
                k_start = q_start - span * (1 - first)
                if stride > 1:
                    q_rows = pl.ds(q_start, ATTN_BLOCK, stride=stride)
                    k_rows = pl.ds(k_start, 2 * ATTN_BLOCK, stride=stride)
                else:
                    q_rows = pl.ds(pl.multiple_of(q_start, ATTN_BLOCK), ATTN_BLOCK)
                    k_rows = pl.ds(pl.multiple_of(k_start, ATTN_BLOCK), 2 * ATTN_BLOCK)
                blocks.append(dict(first=first, q_rows=q_rows, k_rows=k_rows))
            for blk in blocks:
                q2 = q_src[blk["q_rows"], :]
                zero = jnp.zeros_like(q2)
                blk["q"] = jnp.concatenate([jnp.where(head0, q2, zero), jnp.where(head0, zero, q2)],
                                           axis=0).astype(BF16)
                blk["k"] = k_src[blk["k_rows"], :].astype(BF16)
                blk["v"] = v_src[blk["k_rows"], :].astype(BF16)
            for blk in blocks:
                blk["s"] = _dot_nt(blk["q"], blk["k"]) + bias_ref[pi, blk["first"]]
            if row_max:
                for blk in blocks:
                    blk["m"] = jnp.max(blk["s"], axis=-1, keepdims=True)
                for blk in blocks:
                    blk["e"] = jnp.exp(blk["s"] - blk["m"])
            else:
                for blk in blocks:
                    blk["e"] = jnp.exp(blk["s"])
            for blk in blocks:
                blk["l"] = jnp.sum(blk["e"], axis=-1, keepdims=True)
                blk["o"] = _dot(blk["e"].astype(BF16), blk["v"])
            for blk in blocks:
                pair = lambda t: jnp.where(head0, jnp.broadcast_to(t[:ATTN_BLOCK], (ATTN_BLOCK, PAIR)),
                                           jnp.broadcast_to(t[ATTN_BLOCK:], (ATTN_BLOCK, PAIR)))
                blk["out"] = (pair(blk["o"]), pair(blk["m"]) if row_max else None, pair(blk["l"]))
            for blk in blocks:
                for ref, val in zip(state, blk["out"]):
                    if val is not None:
                        ref[blk["q_rows"], :] = val
            return carry

        lax.fori_loop(0, dil * n_blk // ATTN_GROUP, group_body, 0)

    def finish(i, carry):
        c = i // (s_pre // ATTN_BLOCK)
        r0 = (i % (s_pre // ATTN_BLOCK)) * ATTN_BLOCK
        tok = pl.ds(c + PRE_DILATION * r0, ATTN_BLOCK, stride=PRE_DILATION)
        grp = pl.ds(pl.multiple_of(c * s_pre + r0, ATTN_BLOCK), ATTN_BLOCK)
        total = None
        for pi, (_, dil) in enumerate(DILATED_PATTERNS):
            rows = grp if dil % PRE_DILATION == 0 else tok
            if row_max:
                part = tuple(st_s[3 * pi + j, rows, :] for j in range(3))
                total = part if total is None else _softmax_merge(total, part)
            else:
                part = (st_s[3 * pi, rows, :], None, st_s[3 * pi + 2, rows, :])
                total = part if total is None else (total[0] + part[0], None, total[2] + part[2])
        o_ref[0, tok, :] = total[0] / total[2]
        return carry

    lax.fori_loop(0, S // ATTN_BLOCK, finish, 0)


def _softmax_merge(a, b):
    acc_a, m_a, l_a = a
    acc_b, m_b, l_b = b
    m_new = jnp.maximum(m_a, m_b)
    w_a = jnp.exp(m_a - m_new)
    w_b = jnp.exp(m_b - m_new)
    return acc_a * w_a + acc_b * w_b, m_new, l_a * w_a + l_b * w_b


def _alibi_slopes(n):
    def pow2(m):
        start = 2.0 ** (-8.0 / m)
        return [start ** (i + 1) for i in range(m)]
    if math.log2(n).is_integer():
        return pow2(n)
    c = 2 ** int(math.floor(math.log2(n)))
    return pow2(c) + pow2(2 * c)[0::2][: n - c]


def _attn_bias(n_heads):
    qi = jnp.arange(ATTN_BLOCK)
    ki = jnp.arange(2 * ATTN_BLOCK)
    dist = qi[:, None] + ATTN_BLOCK - ki[None, :]
    slopes = jnp.asarray(_alibi_slopes(n_heads), F32)
    out = []
    for window, dil in DILATED_PATTERNS:
        valid = (dist >= 0) & (dist <= window // dil)
        bias = -slopes[:, None, None] * (dist * dil).astype(F32)[None]
        rest = jnp.where(valid[None], bias, MASK_VALUE)
        first = jnp.concatenate([rest[..., ATTN_BLOCK:], jnp.full_like(rest[..., ATTN_BLOCK:], MASK_VALUE)], -1)
        out.append(jnp.stack([rest, first], 0).reshape(2, n_heads * ATTN_BLOCK, 2 * ATTN_BLOCK))
    return jnp.stack(out, 0)


def _attention(lidx, qkv3, q_norm, k_norm, score_bound, d_a):
    B, S, _ = qkv3.shape
    n_heads = d_a // HEAD_DIM
    n_pairs = n_heads // 2
    slab = 2 * HEAD_DIM
    hid = jnp.arange(slab) // HEAD_DIM
    bd = (hid[:, None] == hid[None, :]).astype(BF16)
    bias = _attn_bias(n_heads)[None] - score_bound[:, None, None, None, None]
    exact = (score_bound > SAFE_SOFTMAX_SHIFT).astype(jnp.int32)
    col = lambda off: pl.BlockSpec((1, S, slab), lambda b, p, l, e: (b, 0, off + p))
    gain = pl.BlockSpec((None, 1, slab), lambda b, p, l, e: (l[0], 0, 0))
    grid_spec = pltpu.PrefetchScalarGridSpec(
        num_scalar_prefetch=2, grid=(B, n_pairs),
        in_specs=[col(0), col(n_pairs), col(2 * n_pairs), gain, gain,
                  pl.BlockSpec(bd.shape, lambda b, p, l, e: (0, 0)),
                  pl.BlockSpec((None, len(DILATED_PATTERNS), 2, 2 * ATTN_BLOCK, 2 * ATTN_BLOCK),
                               lambda b, p, l, e: (l[0], 0, 0, p, 0))],
        out_specs=pl.BlockSpec((1, S, slab), lambda b, p, l, e: (b, 0, p)),
        scratch_shapes=[pltpu.VMEM((S, slab), F32)] * 5
        + [pltpu.VMEM((3 * len(DILATED_PATTERNS), S, slab), F32)])
    return pl.pallas_call(
        _attn_kernel, grid_spec=grid_spec, name="dilated_attn",
        out_shape=jax.ShapeDtypeStruct((B, S, d_a), F32),
        compiler_params=pltpu.CompilerParams(dimension_semantics=("arbitrary", "arbitrary"),
                                             vmem_limit_bytes=V7X_VMEM_LIMIT),
    )(lidx, exact, qkv3, qkv3, qkv3, q_norm, k_norm, bd, bias)


def _pad_lora(w, offset):
    L, r, d = w.shape
    return jnp.zeros((L, LORA_SLAB, d), F32).at[:, offset:offset + r, :].set(w)


def kernel(x, norm_ffn1, ffn1_w_gate, ffn1_w_up, ffn1_w_down, norm_mix, w_in, shift_mu, rwkv_w0, rwkv_w2, rwkv_a0, rwkv_a2, rwkv_g2, rwkv_k_k, rwkv_k_a, rwkv_r_k, rwkv_ln_w, rwkv_ln_b, attn_q_norm, attn_k_norm, conv_dw_w, conv_dw_b, conv_ln_w, conv_ln_b, w_out, norm_ffn2, ffn2_w_gate, ffn2_w_up, ffn2_w_down):
    B, S, D = x.shape
    depth = w_in.shape[0]
    d_r = rwkv_w0.shape[-1]
    d_c = conv_dw_b.shape[-1]
    d_shift = shift_mu.shape[-1]
    d_a = (w_in.shape[-1] - d_shift - 2 * d_c) // 3
    r_w, r_a, r_g = rwkv_w2.shape[1], rwkv_a2.shape[1], rwkv_g2.shape[1]
    assert d_shift == 3 * d_r + LORA_SLAB and r_w + r_a + r_g == LORA_SLAB
    assert S % (2 * ATTN_BLOCK * DILATED_PATTERNS[-1][1]) == 0 and (d_a // HEAD_DIM) % 2 == 0
    assert (S // ATTN_BLOCK) % ATTN_GROUP == 0 and S % (PRE_DILATION * REGROUP_ROWS) == 0
    assert all(d < PRE_DILATION or d % PRE_DILATION == 0 for _, d in DILATED_PATTERNS)
    T = B * S
    assert T % FFN_ROWS == 0 and S % RWKV_ROWS == 0

    vec = lambda a: a.reshape(depth, 1, -1)
    bf = lambda a: a.astype(BF16)
    wg1, wu1, wd1 = bf(ffn1_w_gate), bf(ffn1_w_up), bf(ffn1_w_down)
    wg2, wu2, wd2 = bf(ffn2_w_gate), bf(ffn2_w_up), bf(ffn2_w_down)
    win, wo = bf(w_in), bf(w_out)
    g1, gm, g2 = vec(norm_ffn1), vec(norm_mix), vec(norm_ffn2)
    w2p = _pad_lora(rwkv_w2, 0)
    a2p = _pad_lora(rwkv_a2, r_w)
    g2p = _pad_lora(rwkv_g2, r_w + r_a)
    qn = vec(jnp.tile(attn_q_norm, (1, 2)))
    kn = vec(jnp.tile(attn_k_norm, (1, 2)))
    score_bound = (HEAD_DIM ** 0.5) * jnp.max(jnp.abs(attn_q_norm), -1) * jnp.max(jnp.abs(attn_k_norm), -1)

    layer_index = lambda l: jnp.reshape(l, (1,)).astype(jnp.int32)

    def project(l, xf):
        return _ffn_proj(layer_index(l), xf, g1, wg1, wu1, wd1, gm, win, (d_shift, 3 * d_a, 2 * d_c),
                         FFN_ROWS)

    def mix(l, projected):
        lidx = layer_index(l)
        x1, ps, qkv, u = projected
        y_r = _rwkv(lidx, ps.reshape(B, S, d_shift), vec(shift_mu), vec(rwkv_w0), w2p, vec(rwkv_a0),
                    a2p, g2p, vec(rwkv_k_k), vec(rwkv_k_a), vec(rwkv_r_k), vec(rwkv_ln_w),
                    vec(rwkv_ln_b), d_r, RWKV_ROWS)
        y_a = _attention(lidx, qkv.reshape(B, S, 3 * d_a), qn, kn, score_bound, d_a)
        return _out_ffn(lidx, x1, y_r.reshape(T, d_r), y_a.reshape(T, d_a), u, conv_dw_w,
                        vec(conv_dw_b), vec(conv_ln_w), vec(conv_ln_b), wo, g2, wg2, wu2, wd2,
                        FFN_ROWS, S)

    state = project(jnp.int32(0), x.reshape(T, D))
    state = lax.fori_loop(0, depth - 1, lambda l, st: tuple(project(l + 1, mix(l, st))), tuple(state))
    return mix(jnp.int32(depth - 1), state).reshape(B, S, D)
```

```python
import functools
import math

import jax
import jax.numpy as jnp
from jax import lax
from jax.experimental import pallas as pl
from jax.experimental.pallas import tpu as pltpu

F32 = jnp.float32
BF16 = jnp.bfloat16

LANES = 128
SUBLANES = 8
FFN_ROWS = 512
MIX_ROWS = 512
RWKV_ROWS = 1024
HEAD_DIM = 64
PAIR = 2 * HEAD_DIM
NORM_EPS = 1e-6
RWKV_GN_EPS = 64e-5
CONV_LN_EPS = 1e-5
KK_EPS = 1e-12
CONV_WIDTH = 31
DILATED_PATTERNS = ((128, 1), (512, 4), (2048, 16))
ATTN_BLOCK = 128
ATTN_GROUP = 16
ATTN_GROUP_EXACT = 4
PRE_DILATION = 4
REGROUP_ROWS = 256
LORA_SLAB = 128
CHUNK = 64
GROUP_CHUNKS = 8
CUMSUM_ROWS = 256
NEUMANN_STEPS = 6
MASK_VALUE = -1e30
SAFE_SOFTMAX_SHIFT = 40.0
EXP_NEG_HALF = math.exp(-0.5)
V7X_VMEM_LIMIT = 56 * 1024 * 1024

NT_DIMS = (((1,), (1,)), ((), ()))
TN_DIMS = (((0,), (0,)), ((), ()))


def _dot(a, b):
    return jnp.dot(a, b, preferred_element_type=F32)


def _dot_nt(a, b):
    return lax.dot_general(a, b, NT_DIMS, preferred_element_type=F32)


def _dot_tn(a, b):
    return lax.dot_general(a, b, TN_DIMS, preferred_element_type=F32)


def _split2(x):
    hi = x.astype(BF16)
    lo = (x - hi.astype(F32)).astype(BF16)
    return hi, lo


def _dot_bf16(x, w):
    return _dot(x.astype(BF16), w.astype(BF16))


def _dot_f32_rhs(w_exact, x):
    hi, lo = _split2(x)
    return _dot(w_exact, hi) + _dot(w_exact, lo)


def _dot_f32(x, w):
    xh, xl = _split2(x)
    wh, wl = _split2(w)
    return _dot(xh, wh) + _dot(xl, wh) + _dot(xh, wl)


def _sigmoid(x):
    return 0.5 + 0.5 * jnp.tanh(0.5 * x)


def _rms_norm(x, g):
    return x * lax.rsqrt(jnp.mean(x * x, axis=-1, keepdims=True) + NORM_EPS) * g


def _swiglu_residual(x, g, wg_ref, wu_ref, wd_ref):
    xn = _rms_norm(x, g).astype(BF16)
    gate = _dot(xn, wg_ref[...])
    up = _dot(xn, wu_ref[...])
    h =(gate * _sigmoid(gate) * up).astype(BF16)
    return x + 0.5 * _dot(h, wd_ref[...])


def _ffn_proj_kernel(l_ref, x_ref, g1_ref, wg_ref, wu_ref, wd_ref, gm_ref, win_ref,
                     x1_ref, ps_ref, qkv_ref, u_ref):
    del l_ref
    x1 = _swiglu_residual(x_ref[...], g1_ref[...], wg_ref, wu_ref, wd_ref)
    x1_ref[...] = x1
    h = _rms_norm(x1, gm_ref[...]).astype(BF16)
    proj = _dot(h, win_ref[...])
    d_shift = ps_ref.shape[-1]
    d_qkv = qkv_ref.shape[-1]
    ps_ref[...] = proj[:, :d_shift]
    qkv_ref[...] = proj[:, d_shift:d_shift + d_qkv]
    u_ref[...] = proj[:, d_shift + d_qkv:]


def _layer_spec(shape, buffered=True):
    nd = len(shape)
    kw = dict(pipeline_mode=pl.Buffered(1)) if buffered else {}
    return pl.BlockSpec((None,) + tuple(shape), lambda *a: (a[-1][0],) + (0,) * nd, **kw)


def _ffn_proj(lidx, x, g1, wg, wu, wd, gm, win, dims, tm):
    T, D = x.shape
    d_shift, d_qkv, d_u = dims
    row = lambda w: pl.BlockSpec((tm, w), lambda i, l: (i, 0))
    grid_spec = pltpu.PrefetchScalarGridSpec(
        num_scalar_prefetch=1, grid=(T // tm,),
        in_specs=[row(D), _layer_spec((1, D)), _layer_spec(wg.shape[1:]), _layer_spec(wu.shape[1:]),
                  _layer_spec(wd.shape[1:]), _layer_spec((1, D)), _layer_spec(win.shape[1:])],
        out_specs=[row(D), row(d_shift), row(d_qkv), row(d_u)])
    return pl.pallas_call(
        _ffn_proj_kernel, grid_spec=grid_spec, name="ffn_proj",
        out_shape=[jax.ShapeDtypeStruct((T, D), F32), jax.ShapeDtypeStruct((T, d_shift), F32),
                   jax.ShapeDtypeStruct((T, d_qkv), F32), jax.ShapeDtypeStruct((T, d_u), F32)],
        compiler_params=pltpu.CompilerParams(dimension_semantics=("arbitrary",),
                                             vmem_limit_bytes=V7X_VMEM_LIMIT),
    )(lidx, x, g1, wg, wu, wd, gm, win)


CONV_PAD = 32
CONV_ROWS = 64


def _conv_rows(z_ref, r0, w_ref, b_ref, lnw_ref, lnb_ref):
    d_c = z_ref.shape[-1]
    shift = CONV_PAD - (CONV_WIDTH - 1)
    n_win = CONV_ROWS + CONV_PAD
    win = z_ref[r0:r0 + n_win, :]
    acc = jnp.zeros((CONV_ROWS, d_c), F32) + b_ref[...]
    for sub in range(SUBLANES):
        rolled = win if sub == 0 else pltpu.roll(win, n_win - sub, 0)
        for j in range(CONV_WIDTH):
            off = shift + j
            if off % SUBLANES == sub:
                base = off - sub
                acc = acc + rolled[base:base + CONV_ROWS, :] * w_ref[j:j + 1, :]
    mean = jnp.mean(acc, axis=-1, keepdims=True)
    d = acc - mean
    var = jnp.mean(d * d, axis=-1, keepdims=True)
    z = d * lax.rsqrt(var + CONV_LN_EPS) * lnw_ref[...] + lnb_ref[...]
    return z * _sigmoid(z)


def _mix_out_kernel(tiles_per_seq, l_ref, x_ref, yr_ref, ya_ref, u_ref, cw_ref, cb_ref, clnw_ref,
                    clnb_ref, wo_ref, o_ref, z_s):
    del l_ref
    i = pl.program_id(0)
    tm = x_ref.shape[0]
    d_r = yr_ref.shape[-1]
    d_a = ya_ref.shape[-1]
    d_c = z_s.shape[-1]

    @pl.when(i == 0)
    def _():
        z_s[...] = jnp.zeros_like(z_s)

    left = z_s[tm:tm + CONV_PAD, :]
    z_s[0:CONV_PAD, :] = jnp.where(i % tiles_per_seq == 0, jnp.zeros_like(left), left)
    z_s[CONV_PAD:, :] = u_ref[:, 0:d_c] * _sigmoid(u_ref[:, d_c:])
    y_c = jnp.concatenate([_conv_rows(z_s, r0, cw_ref, cb_ref, clnw_ref, clnb_ref)
                           for r0 in range(0, tm, CONV_ROWS)], axis=0)
    o_ref[...] = (x_ref[...]
                  + _dot(yr_ref[...].astype(BF16), wo_ref[0:d_r, :])
                  + _dot(ya_ref[...].astype(BF16), wo_ref[d_r:d_r + d_a, :])
                  + _dot(y_c.astype(BF16), wo_ref[d_r + d_a:, :]))


def _mix_out(lidx, x, yr, ya, u, cw, cb, clnw, clnb, wo, tm, seq_len):
    T, D = x.shape
    d_c = u.shape[1] // 2
    assert seq_len % tm == 0 and tm % CONV_ROWS == 0
    row = lambda w: pl.BlockSpec((tm, w), lambda i, l: (i, 0))
    grid_spec = pltpu.PrefetchScalarGridSpec(
        num_scalar_prefetch=1, grid=(T // tm,),
        in_specs=[row(D), row(yr.shape[1]), row(ya.shape[1]), row(2 * d_c),
                  _layer_spec((CONV_WIDTH, d_c), buffered=False), _layer_spec((1, d_c), buffered=False),
                  _layer_spec((1, d_c), buffered=False), _layer_spec((1, d_c), buffered=False),
                  _layer_spec(wo.shape[1:])],
        out_specs=row(D),
        scratch_shapes=[pltpu.VMEM((CONV_PAD + tm, d_c), F32)])
    return pl.pallas_call(
        functools.partial(_mix_out_kernel, seq_len // tm), grid_spec=grid_spec, name="mix_out",
        out_shape=jax.ShapeDtypeStruct((T, D), F32),
        compiler_params=pltpu.CompilerParams(dimension_semantics=("arbitrary",),
                                             vmem_limit_bytes=V7X_VMEM_LIMIT),
    )(lidx, x, yr, ya, u, cw, cb, clnw, clnb, wo)


def _ffn_kernel(l_ref, x_ref, g_ref, wg_ref, wu_ref, wd_ref, o_ref):
    del l_ref
    o_ref[...] = _swiglu_residual(x_ref[...], g_ref[...], wg_ref, wu_ref, wd_ref)


def _ffn(lidx, x, g, wg, wu, wd, tm):
    T, D = x.shape
    row = pl.BlockSpec((tm, D), lambda i, l: (i, 0))
    grid_spec = pltpu.PrefetchScalarGridSpec(
        num_scalar_prefetch=1, grid=(T // tm,),
        in_specs=[row, _layer_spec((1, D)), _layer_spec(wg.shape[1:]), _layer_spec(wu.shape[1:]),
                  _layer_spec(wd.shape[1:])],
        out_specs=row)
    return pl.pallas_call(
        _ffn_kernel, grid_spec=grid_spec, name="ffn2",
        out_shape=jax.ShapeDtypeStruct((T, D), F32),
        compiler_params=pltpu.CompilerParams(dimension_semantics=("arbitrary",),
                                             vmem_limit_bytes=V7X_VMEM_LIMIT),
    )(lidx, x, g, wg, wu, wd)


def _rwkv_kernel(l_ref, p_ref, mu_ref, w0_ref, w2_ref, a0_ref, a2_ref, g2_ref, kk_ref, ka_ref,
                 rk_ref, lnw_ref, lnb_ref, bd_ref, tri_ref, o_ref,
                 carry_ref, state_ref):
    del l_ref
    tb = p_ref.shape[1]
    d_r = o_ref.shape[-1]

    @pl.when(pl.program_id(1) == 0)
    def _():
        carry_ref[...] = jnp.zeros_like(carry_ref)
        state_ref[...] = jnp.zeros_like(state_ref)

    p = p_ref[0]
    row = lax.broadcasted_iota(jnp.int32, (tb, 1), 0)
    prev = jnp.where(row == 0, carry_ref[...], pltpu.roll(p, 1, 0))
    carry_ref[...] = p[tb - 1:tb, :]
    ps = p + (prev - p) * mu_ref[...]

    r = ps[:, 0:d_r]
    k = ps[:, d_r:2 * d_r]
    v = ps[:, 2 * d_r:3 * d_r]
    xs = ps[:, 3 * d_r:3 * d_r + LORA_SLAB]

    bd = bd_ref[...]
    def seg_sum(t):
        tb16 = t.astype(BF16)
        return jnp.concatenate([_dot(tb16[:, s:s + PAIR], bd) for s in range(0, d_r, PAIR)], axis=1)

    lw = w0_ref[...] + _dot_f32(jnp.tanh(xs), w2_ref[...])
    ld = (-0.5 * EXP_NEG_HALF) * jnp.tanh(0.5 * lw) - 0.5 * EXP_NEG_HALF
    a = _sigmoid(a0_ref[...] + _dot_bf16(xs, a2_ref[...]))
    g = _dot_bf16(_sigmoid(xs), g2_ref[...])
    kk = k * kk_ref[...]
    kk = kk * lax.rsqrt(seg_sum(kk * kk) + KK_EPS)
    ka = ka_ref[...]
    k2 = k * ((1.0 - ka) + a * ka)
    b = kk * a

    lcum = jnp.concatenate(
        [_dot_f32_rhs(tri_ref[...], ld[s:s + CUMSUM_ROWS]) for s in range(0, tb, CUMSUM_ROWS)], axis=0)
    n_chunks = tb // CHUNK
    pc_rows = [jnp.exp(lcum[(c + 1) * CHUNK - 1:(c + 1) * CHUNK, :]) for c in range(n_chunks)]
    at_all = (-kk * jnp.exp(lcum - ld)).astype(BF16)
    rt_all = r * jnp.exp(lcum)
    inv_p = jnp.exp(-lcum)
    bt_all = (b * inv_p).astype(BF16)
    kt_all = (k2 * inv_p).astype(BF16)
    rest = inv_p * jnp.concatenate([jnp.broadcast_to(pc, (CHUNK, d_r)) for pc in pc_rows], axis=0)
    bp_all = (b * rest).astype(BF16)
    kp_all = (k2 * rest).astype(BF16)
    v_all = v.astype(BF16)

    lane = lax.broadcasted_iota(jnp.int32, (CHUNK, PAIR), 1)
    trow = lax.broadcasted_iota(jnp.int32, (CHUNK, PAIR), 0)
    head0 = lane < HEAD_DIM
    scol = lane & (HEAD_DIM - 1)
    strict = trow > scol
    incl = trow >= scol
    eye2 = jnp.where(trow == scol, 1.0, 0.0)
    r2 = lax.broadcasted_iota(jnp.int32, (PAIR, PAIR), 0)
    c2 = lax.broadcasted_iota(jnp.int32, (PAIR, PAIR), 1)
    same_head = (r2 < HEAD_DIM) == (c2 < HEAD_DIM)

    def block_diag(xb):
        zero = jnp.zeros_like(xb)
        return jnp.concatenate([jnp.where(head0, xb, zero), jnp.where(head0, zero, xb)], axis=0)

    n_pairs = d_r // PAIR
    tile = lambda t, c, p: t[c * CHUNK:(c + 1) * CHUNK, p * PAIR:(p + 1) * PAIR]
    inv_n = 1.0 / HEAD_DIM

    for c_lo in range(0, n_chunks, GROUP_CHUNKS):
        group_chunks = range(c_lo, min(c_lo + GROUP_CHUNKS, n_chunks))
        units = [(c, p) for c in group_chunks for p in range(n_pairs)]
        y = _rwkv_group(units, group_chunks, n_pairs, tile, block_diag, state_ref, pc_rows,
                        (at_all, rt_all, bt_all, kt_all, bp_all, kp_all, v_all),
                        (strict, incl, eye2, same_head))
        rows = slice(group_chunks[0] * CHUNK, (group_chunks[-1] + 1) * CHUNK)
        mean = seg_sum(y) * inv_n
        d = y - mean
        var = seg_sum(d * d) * inv_n
        yn = d * lax.rsqrt(var + RWKV_GN_EPS) * lnw_ref[...] + lnb_ref[...]
        bonus = seg_sum(r[rows] * k2[rows] * rk_ref[...]) * v[rows]
        o_ref[0, rows, :] = (yn + bonus) * g[rows]


def _rwkv_group(units, group_chunks, n_pairs, tile, block_diag, state_ref, pc_rows, scaled, masks):
    at_all, rt_all, bt_all, kt_all, bp_all, kp_all, v_all = scaled
    strict, incl, eye2, same_head = masks
    a_ab, a_ak, a_rb, a_rk = {}, {}, {}, {}
    for u in units:
        ar = jnp.concatenate([tile(at_all, *u), tile(rt_all, *u).astype(BF16)], axis=0)
        bk = jnp.concatenate([block_diag(tile(bt_all, *u)), block_diag(tile(kt_all, *u))], axis=0)
        g2 = _dot_nt(ar, bk)
        a_ab[u] = jnp.where(strict, g2[:CHUNK, :PAIR], 0.0)
        a_ak[u] = jnp.where(strict, g2[:CHUNK, PAIR:], 0.0).astype(BF16)
        a_rb[u] = jnp.where(incl, g2[CHUNK:, :PAIR], 0.0).astype(BF16)
        a_rk[u] = jnp.where(incl, g2[CHUNK:, PAIR:], 0.0).astype(BF16)
    tinv = {u: eye2 + a_ab[u] for u in units}
    apow = {}
    for u in units:
        ab = a_ab[u].astype(BF16)
        apow[u] = _dot(ab, block_diag(ab))
    for step in range(1, NEUMANN_STEPS):
        last = step == NEUMANN_STEPS - 1
        for u in units:
            ab = apow[u].astype(BF16)
            lhs = tinv[u].astype(BF16) if last else jnp.concatenate([tinv[u].astype(BF16), ab], axis=0)
            prod = _dot(lhs, block_diag(ab))
            tinv[u] = tinv[u] + prod[:CHUNK]
            if not last:
                apow[u] = prod[CHUNK:]
    av, y1v, w1, u1, rq, y1, mx, gx = {}, {}, {}, {}, {}, {}, {}, {}
    for u in units:
        avk = _dot(jnp.concatenate([a_ak[u], a_rk[u]], axis=0), block_diag(tile(v_all, *u)))
        av[u], y1v[u] = avk[:CHUNK].astype(BF16), avk[CHUNK:]
    for u in units:
        rhs = jnp.concatenate([block_diag(tile(at_all, *u)), block_diag(av[u])], axis=1)
        wu = _dot(tinv[u].astype(BF16), rhs).astype(BF16)
        w1[u], u1[u] = wu[:, :PAIR], wu[:, PAIR:]
    for u in units:
        vc = tile(v_all, *u)
        bpkp = jnp.concatenate([tile(bp_all, *u), tile(kp_all, *u)], axis=0)
        ry = _dot(a_rb[u], jnp.concatenate([block_diag(w1[u]), block_diag(u1[u])], axis=1))
        rq[u] = (tile(rt_all, *u) + ry[:, :PAIR]).astype(BF16)
        y1[u] = ry[:, PAIR:] + y1v[u]
        lhs = jnp.concatenate([jnp.concatenate([w1[u], u1[u]], axis=1),
                               jnp.concatenate([jnp.zeros_like(vc), vc], axis=1)], axis=0)
        mg = _dot_tn(lhs, bpkp)
        mx[u] = jnp.where(same_head, mg[:PAIR], 0.0).astype(BF16)
        gx[u] = jnp.where(same_head, mg[PAIR:], 0.0)

    y_rows = []
    for c in group_chunks:
        y_parts = []
        for p in range(n_pairs):
            u = (c, p)
            s0 = state_ref[p]
            s0b = s0.astype(BF16)
            y_parts.append(_dot_nt(rq[u], s0b) + y1[u])
            pc = pc_rows[c][:, p * PAIR:(p + 1) * PAIR]
            state_ref[p] = s0 * pc + _dot(s0b, mx[u]) + gx[u]
        y_rows.append(jnp.concatenate(y_parts, axis=1))
    return jnp.concatenate(y_rows, axis=0)


def _rwkv(lidx, ps3, mu, w0, w2p, a0, a2p, g2p, k_k, k_a, r_k, ln_w, ln_b, d_r, tb):
    B, S, d_shift = ps3.shape
    hid = jnp.arange(PAIR) // HEAD_DIM
    bd = (hid[:, None] == hid[None, :]).astype(BF16)
    assert tb % CUMSUM_ROWS == 0
    ci = jnp.arange(CUMSUM_ROWS)
    same = (ci[:, None] // CHUNK) == (ci[None, :] // CHUNK)
    tri = (same & (ci[:, None] >= ci[None, :])).astype(BF16)
    const = lambda a: pl.BlockSpec(a.shape, lambda b, t, l: (0,) * a.ndim)
    vec = _layer_spec((1, d_r), buffered=False)
    lora = _layer_spec((LORA_SLAB, d_r), buffered=False)
    grid_spec = pltpu.PrefetchScalarGridSpec(
        num_scalar_prefetch=1, grid=(B, S // tb),
        in_specs=[pl.BlockSpec((1, tb, d_shift), lambda b, t, l: (b, t, 0)),
                  _layer_spec((1, d_shift), buffered=False),
                  vec, lora, vec, lora, lora, vec, vec, vec, vec, vec,
                  const(bd), const(tri)],
        out_specs=pl.BlockSpec((1, tb, d_r), lambda b, t, l: (b, t, 0)),
        scratch_shapes=[pltpu.VMEM((1, d_shift), F32),
                        pltpu.VMEM((d_r // PAIR, PAIR, PAIR), F32)])
    return pl.pallas_call(
        _rwkv_kernel, grid_spec=grid_spec, name="rwkv7",
        out_shape=jax.ShapeDtypeStruct((B, S, d_r), F32),
        compiler_params=pltpu.CompilerParams(dimension_semantics=("arbitrary", "arbitrary"),
                                             vmem_limit_bytes=V7X_VMEM_LIMIT),
    )(lidx, ps3, mu, w0, w2p, a0, a2p, g2p, k_k, k_a, r_k, ln_w, ln_b, bd, tri)


def _attn_kernel(l_ref, exact_ref, q_ref, k_ref, v_ref, qn_ref, kn_ref, bd_ref, bias_ref, o_ref,
                 q_s, k_s, q4_s, k4_s, v4_s, st_s):
    S = q_ref.shape[1]
    bd = bd_ref[...]
    inv_n = 1.0 / HEAD_DIM

    def head_rms(t, gain):
        ms = _dot_bf16(t * t, bd) * inv_n
        return t * lax.rsqrt(ms + NORM_EPS) * gain

    q_s[...] = head_rms(q_ref[0], qn_ref[...]) * (HEAD_DIM ** -0.5)
    k_s[...] = head_rms(k_ref[0], kn_ref[...])
    head0 = lax.broadcasted_iota(jnp.int32, (ATTN_BLOCK, PAIR), 1) < HEAD_DIM

    s_pre = S // PRE_DILATION
    v_tok = v_ref.at[0]

    def regroup(i, carry):
        c = i // (s_pre // REGROUP_ROWS)
        r0 = (i % (s_pre // REGROUP_ROWS)) * REGROUP_ROWS
        src = pl.ds(c + PRE_DILATION * r0, REGROUP_ROWS, stride=PRE_DILATION)
        dst = pl.ds(pl.multiple_of(c * s_pre + r0, REGROUP_ROWS), REGROUP_ROWS)
        q4_s[dst, :] = q_s[src, :]
        k4_s[dst, :] = k_s[src, :]
        v4_s[dst, :] = v_tok[src, :]
        return carry

    lax.fori_loop(0, S // REGROUP_ROWS, regroup, 0)

    exact = exact_ref[l_ref[0]] == 1
    pl.when(exact)(lambda: _attn_patterns(True, S, s_pre, head0, v_tok, bias_ref, o_ref,
                                          q_s, k_s, q4_s, k4_s, v4_s, st_s))
    pl.when(jnp.logical_not(exact))(lambda: _attn_patterns(False, S, s_pre, head0, v_tok, bias_ref, o_ref,
                                                           q_s, k_s, q4_s, k4_s, v4_s, st_s))


def _attn_patterns(row_max, S, s_pre, head0, v_tok, bias_ref, o_ref, q_s, k_s, q4_s, k4_s, v4_s, st_s):
    group = ATTN_GROUP_EXACT if row_max else ATTN_GROUP
    for pi, (window, dil) in enumerate(DILATED_PATTERNS):
        n_sub = S // dil
        n_blk = n_sub // ATTN_BLOCK
        regrouped = dil % PRE_DILATION == 0
        if regrouped:
            stride = dil // PRE_DILATION
            q_src, k_src, v_src = q4_s, k4_s, v4_s
        else:
            stride = dil
            q_src, k_src, v_src = q_s, k_s, v_tok
        state = tuple(st_s.at[3 * pi + j] for j in range(3))
        span = stride * ATTN_BLOCK

        def group_body(gi, carry):
            blocks = []
            for g in range(group):
                i = gi * group + g
                n = i // dil
                res = i % dil
                if regrouped:
                    q_start = (res % PRE_DILATION) * s_pre + res // PRE_DILATION + span * n
                else:
                    q_start = res + span * n
                first = jnp.where(n == 0, 1, 0)
                k_start = q_start - span * (1 - first)
                if stride > 1:
                    q_rows = pl.ds(q_start, ATTN_BLOCK, stride=stride)
                    k_rows = pl.ds(k_start, 2 * ATTN_BLOCK, stride=stride)
                else:
                    q_rows = pl.ds(pl.multiple_of(q_start, ATTN_BLOCK), ATTN_BLOCK)
                    k_rows = pl.ds(pl.multiple_of(k_start, ATTN_BLOCK), 2 * ATTN_BLOCK)
                blocks.append(dict(first=first, q_rows=q_rows, k_rows=k_rows))
            for blk in blocks:
                q2 = q_src[blk["q_rows"], :]
                zero = jnp.zeros_like(q2)
                blk["q"] = jnp.concatenate([jnp.where(head0, q2, zero), jnp.where(head0, zero, q2)],
                                           axis=0).astype(BF16)
                blk["k"] = k_src[blk["k_rows"], :].astype(BF16)
                blk["v"] = v_src[blk["k_rows"], :].astype(BF16)
            for blk in blocks:
                blk["s"] = _dot_nt(blk["q"], blk["k"]) + bias_ref[pi, blk["first"]]
            if row_max:
                for blk in blocks:
                    blk["m"] = jnp.max(blk["s"], axis=-1, keepdims=True)
                for blk in blocks:
                    blk["e"] = jnp.exp(blk["s"] - blk["m"])
            else:
                for blk in blocks:
                    blk["e"] = jnp.exp(blk["s"])
            for blk in blocks:
                blk["l"] = jnp.sum(blk["e"], axis=-1, keepdims=True)
                blk["o"] = _dot(blk["e"].astype(BF16), blk["v"])
            for blk in blocks:
                pair = lambda t: jnp.where(head0, jnp.broadcast_to(t[:ATTN_BLOCK], (ATTN_BLOCK, PAIR)),
                                           jnp.broadcast_to(t[ATTN_BLOCK:], (ATTN_BLOCK, PAIR)))
                blk["out"] = (pair(blk["o"]), pair(blk["m"]) if row_max else None, pair(blk["l"]))
            for blk in blocks:
                for ref, val in zip(state, blk["out"]):
                    if val is not None:
                        ref[blk["q_rows"], :] = val
            return carry

        lax.fori_loop(0, dil * n_blk // group, group_body, 0)

    def finish(i, carry):
        c = i // (s_pre // ATTN_BLOCK)
        r0 = (i % (s_pre // ATTN_BLOCK)) * ATTN_BLOCK
        tok = pl.ds(c + PRE_DILATION * r0, ATTN_BLOCK, stride=PRE_DILATION)
        grp = pl.ds(pl.multiple_of(c * s_pre + r0, ATTN_BLOCK), ATTN_BLOCK)
        total = None
        for pi, (_, dil) in enumerate(DILATED_PATTERNS):
            rows = grp if dil % PRE_DILATION == 0 else tok
            if row_max:
                part = tuple(st_s[3 * pi + j, rows, :] for j in range(3))
                total = part if total is None else _softmax_merge(total, part)
            else:
                part = (st_s[3 * pi, rows, :], None, st_s[3 * pi + 2, rows, :])
                total = part if total is None else (total[0] + part[0], None, total[2] + part[2])
        o_ref[0, tok, :] = total[0] / total[2]
        return carry

    lax.fori_loop(0, S // ATTN_BLOCK, finish, 0)


def _softmax_merge(a, b):
    acc_a, m_a, l_a = a
    acc_b, m_b, l_b = b
    m_new = jnp.maximum(m_a, m_b)
    w_a = jnp.exp(m_a - m_new)
    w_b = jnp.exp(m_b - m_new)
    return acc_a * w_a + acc_b * w_b, m_new, l_a * w_a + l_b * w_b


def _alibi_slopes(n):
    def pow2(m):
        start = 2.0 ** (-8.0 / m)
        return [start ** (i + 1) for i in range(m)]
    if math.log2(n).is_integer():
        return pow2(n)
    c = 2 ** int(math.floor(math.log2(n)))
    return pow2(c) + pow2(2 * c)[0::2][: n - c]


def _attn_bias(n_heads):
    qi = jnp.arange(ATTN_BLOCK)
    ki = jnp.arange(2 * ATTN_BLOCK)
    dist = qi[:, None] + ATTN_BLOCK - ki[None, :]
    slopes = jnp.asarray(_alibi_slopes(n_heads), F32)
    out = []
    for window, dil in DILATED_PATTERNS:
        valid = (dist >= 0) & (dist <= window // dil)
        bias = -slopes[:, None, None] * (dist * dil).astype(F32)[None]
        rest = jnp.where(valid[None], bias, MASK_VALUE)
        first = jnp.concatenate([rest[..., ATTN_BLOCK:], jnp.full_like(rest[..., ATTN_BLOCK:], MASK_VALUE)], -1)
        out.append(jnp.stack([rest, first], 0).reshape(2, n_heads * ATTN_BLOCK, 2 * ATTN_BLOCK))
    return jnp.stack(out, 0)


def _attention(lidx, qkv3, q_norm, k_norm, score_bound, d_a):
    B, S, _ = qkv3.shape
    n_heads = d_a // HEAD_DIM
    n_pairs = n_heads // 2
    slab = 2 * HEAD_DIM
    hid = jnp.arange(slab) // HEAD_DIM
    bd = (hid[:, None] == hid[None, :]).astype(BF16)
    bias = _attn_bias(n_heads)[None] - score_bound[:, None, None, None, None]
    exact = (score_bound > SAFE_SOFTMAX_SHIFT).astype(jnp.int32)
    col = lambda off: pl.BlockSpec((1, S, slab), lambda b, p, l, e: (b, 0, off + p))
    gain = pl.BlockSpec((None, 1, slab), lambda b, p, l, e: (l[0], 0, 0))
    grid_spec = pltpu.PrefetchScalarGridSpec(
        num_scalar_prefetch=2, grid=(B, n_pairs),
        in_specs=[col(0), col(n_pairs), col(2 * n_pairs), gain, gain,
                  pl.BlockSpec(bd.shape, lambda b, p, l, e: (0, 0)),
                  pl.BlockSpec((None, len(DILATED_PATTERNS), 2, 2 * ATTN_BLOCK, 2 * ATTN_BLOCK),
                               lambda b, p, l, e: (l[0], 0, 0, p, 0))],
        out_specs=pl.BlockSpec((1, S, slab), lambda b, p, l, e: (b, 0, p)),
        scratch_shapes=[pltpu.VMEM((S, slab), F32)] * 5
        + [pltpu.VMEM((3 * len(DILATED_PATTERNS), S, slab), F32)])
    return pl.pallas_call(
        _attn_kernel, grid_spec=grid_spec, name="dilated_attn",
        out_shape=jax.ShapeDtypeStruct((B, S, d_a), F32),
        compiler_params=pltpu.CompilerParams(dimension_semantics=("arbitrary", "arbitrary"),
                                             vmem_limit_bytes=V7X_VMEM_LIMIT),
    )(lidx, exact, qkv3, qkv3, qkv3, q_norm, k_norm, bd, bias)


def _pad_lora(w, offset):
    L, r, d = w.shape
    return jnp.zeros((L, LORA_SLAB, d), F32).at[:, offset:offset + r, :].set(w)


def kernel(x, norm_ffn1, ffn1_w_gate, ffn1_w_up, ffn1_w_down, norm_mix, w_in, shift_mu, rwkv_w0, rwkv_w2, rwkv_a0, rwkv_a2, rwkv_g2, rwkv_k_k, rwkv_k_a, rwkv_r_k, rwkv_ln_w, rwkv_ln_b, attn_q_norm, attn_k_norm, conv_dw_w, conv_dw_b, conv_ln_w, conv_ln_b, w_out, norm_ffn2, ffn2_w_gate, ffn2_w_up, ffn2_w_down):
    B, S, D = x.shape
    depth = w_in.shape[0]
    d_r = rwkv_w0.shape[-1]
    d_c = conv_dw_b.shape[-1]
    d_shift = shift_mu.shape[-1]
    d_a = (w_in.shape[-1] - d_shift - 2 * d_c) // 3
    r_w, r_a, r_g = rwkv_w2.shape[1], rwkv_a2.shape[1], rwkv_g2.shape[1]
    assert d_shift == 3 * d_r + LORA_SLAB and r_w + r_a + r_g == LORA_SLAB
    assert S % (2 * ATTN_BLOCK * DILATED_PATTERNS[-1][1]) == 0 and (d_a // HEAD_DIM) % 2 == 0
    assert (S // ATTN_BLOCK) % max(ATTN_GROUP, ATTN_GROUP_EXACT) == 0
    assert S % (PRE_DILATION * REGROUP_ROWS) == 0
    assert all(d < PRE_DILATION or d % PRE_DILATION == 0 for _, d in DILATED_PATTERNS)
    T = B * S
    assert T % FFN_ROWS == 0 and S % RWKV_ROWS == 0

    vec = lambda a: a.reshape(depth, 1, -1)
    bf = lambda a: a.astype(BF16)
    wg1, wu1, wd1 = bf(ffn1_w_gate), bf(ffn1_w_up), bf(ffn1_w_down)
    wg2, wu2, wd2 = bf(ffn2_w_gate), bf(ffn2_w_up), bf(ffn2_w_down)
    win, wo = bf(w_in), bf(w_out)
    g1, gm, g2 = vec(norm_ffn1), vec(norm_mix), vec(norm_ffn2)
    w2p = _pad_lora(rwkv_w2, 0)
    a2p = _pad_lora(rwkv_a2, r_w)
    g2p = _pad_lora(rwkv_g2, r_w + r_a)
    qn = vec(jnp.tile(attn_q_norm, (1, 2)))
    kn = vec(jnp.tile(attn_k_norm, (1, 2)))
    score_bound = (HEAD_DIM ** 0.5) * jnp.max(jnp.abs(attn_q_norm), -1) * jnp.max(jnp.abs(attn_k_norm), -1)

    layer_index = lambda l: jnp.reshape(l, (1,)).astype(jnp.int32)

    def project(l, xf):
        return _ffn_proj(layer_index(l), xf, g1, wg1, wu1, wd1, gm, win, (d_shift, 3 * d_a, 2 * d_c),
                         FFN_ROWS)

    def mix(l, projected):
        lidx = layer_index(l)
        x1, ps, qkv, u = projected
        y_r = _rwkv(lidx, ps.reshape(B, S, d_shift), vec(shift_mu), vec(rwkv_w0), w2p, vec(rwkv_a0),
                    a2p, g2p, vec(rwkv_k_k), vec(rwkv_k_a), vec(rwkv_r_k), vec(rwkv_ln_w),
                    vec(rwkv_ln_b), d_r, RWKV_ROWS)
        y_a = _attention(lidx, qkv.reshape(B, S, 3 * d_a), qn, kn, score_bound, d_a)
        x2 = _mix_out(lidx, x1, y_r.reshape(T, d_r), y_a.reshape(T, d_a), u, conv_dw_w,
                      vec(conv_dw_b), vec(conv_ln_w), vec(conv_ln_b), wo, MIX_ROWS, S)
        return _ffn(lidx, x2, g2, wg2, wu2, wd2, FFN_ROWS)

    state = project(jnp.int32(0), x.reshape(T, D))
    state = lax.fori_loop(0, depth - 1, lambda l, st: tuple(project(l + 1, mix(l, st))), tuple(state))
    return mix(jnp.int32(depth - 1), state).reshape(B, S, D)
```

```python
import functools
import math

import jax
import jax.numpy as jnp
from jax import lax
from jax.experimental import pallas as pl
from jax.experimental.pallas import tpu as pltpu

F32 = jnp.float32
BF16 = jnp.bfloat16

LANES = 128
SUBLANES = 8
FFN_ROWS = 512
MIX_ROWS = 1024
RWKV_ROWS = 1024
HEAD_DIM = 64
PAIR = 2 * HEAD_DIM
NORM_EPS = 1e-6
RWKV_GN_EPS = 64e-5
CONV_LN_EPS = 1e-5
KK_EPS = 1e-12
CONV_WIDTH = 31
DILATED_PATTERNS = ((128, 1), (512, 4), (2048, 16))
ATTN_BLOCK = 128
ATTN_GROUP = 16
ATTN_GROUP_EXACT = 4
PRE_DILATION = 4
REGROUP_ROWS = 256
LORA_SLAB = 128
CHUNK = 64
GROUP_CHUNKS = 8
CUMSUM_ROWS = 256
NEUMANN_STEPS = 6
MASK_VALUE = -1e30
SAFE_SOFTMAX_SHIFT = 40.0
EXP_NEG_HALF = math.exp(-0.5)
V7X_VMEM_LIMIT = 56 * 1024 * 1024

NT_DIMS = (((1,), (1,)), ((), ()))
TN_DIMS = (((0,), (0,)), ((), ()))


def _dot(a, b):
    return jnp.dot(a, b, preferred_element_type=F32)


def _dot_nt(a, b):
    return lax.dot_general(a, b, NT_DIMS, preferred_element_type=F32)


def _dot_tn(a, b):
    return lax.dot_general(a, b, TN_DIMS, preferred_element_type=F32)


def _split2(x):
    hi = x.astype(BF16)
    lo = (x - hi.astype(F32)).astype(BF16)
    return hi, lo


def _dot_bf16(x, w):
    return _dot(x.astype(BF16), w.astype(BF16))


def _dot_f32_rhs(w_exact, x):
    hi, lo = _split2(x)
    return _dot(w_exact, hi) + _dot(w_exact, lo)


def _dot_f32(x, w):
    xh, xl = _split2(x)
    wh, wl = _split2(w)
    return _dot(xh, wh) + _dot(xl, wh) + _dot(xh, wl)


def _sigmoid(x):
    return 0.5 + 0.5 * jnp.tanh(0.5 * x)


def _rms_norm(x, g):
    return x * lax.rsqrt(jnp.mean(x * x, axis=-1, keepdims=True) + NORM_EPS) * g


def _swiglu_residual(x, g, wg_ref, wu_ref, wd_ref):
    xn = _rms_norm(x, g).astype(BF16)
    gate = _dot(xn, wg_ref[...])
    up = _dot(xn, wu_ref[...])
    h =(gate * _sigmoid(gate) * up).astype(BF16)
    return x + 0.5 * _dot(h, wd_ref[...])


def _ffn_proj_kernel(l_ref, x_ref, g1_ref, wg_ref, wu_ref, wd_ref, gm_ref, win_ref,
                     x1_ref, ps_ref, qkv_ref, u_ref):
    del l_ref
    x1 = _swiglu_residual(x_ref[...], g1_ref[...], wg_ref, wu_ref, wd_ref)
    x1_ref[...] = x1
    h = _rms_norm(x1, gm_ref[...]).astype(BF16)
    proj = _dot(h, win_ref[...])
    d_shift = ps_ref.shape[-1]
    d_qkv = qkv_ref.shape[-1]
    ps_ref[...] = proj[:, :d_shift]
    qkv_ref[...] = proj[:, d_shift:d_shift + d_qkv]
    u_ref[...] = proj[:, d_shift + d_qkv:]


def _layer_spec(shape, buffered=True):
    nd = len(shape)
    kw = dict(pipeline_mode=pl.Buffered(1)) if buffered else {}
    return pl.BlockSpec((None,) + tuple(shape), lambda *a: (a[-1][0],) + (0,) * nd, **kw)


def _ffn_proj(lidx, x, g1, wg, wu, wd, gm, win, dims, tm):
    T, D = x.shape
    d_shift, d_qkv, d_u = dims
    row = lambda w: pl.BlockSpec((tm, w), lambda i, l: (i, 0))
    grid_spec = pltpu.PrefetchScalarGridSpec(
        num_scalar_prefetch=1, grid=(T // tm,),
        in_specs=[row(D), _layer_spec((1, D)), _layer_spec(wg.shape[1:]), _layer_spec(wu.shape[1:]),
                  _layer_spec(wd.shape[1:]), _layer_spec((1, D)), _layer_spec(win.shape[1:])],
        out_specs=[row(D), row(d_shift), row(d_qkv), row(d_u)])
    return pl.pallas_call(
        _ffn_proj_kernel, grid_spec=grid_spec, name="ffn_proj",
        out_shape=[jax.ShapeDtypeStruct((T, D), F32), jax.ShapeDtypeStruct((T, d_shift), F32),
                   jax.ShapeDtypeStruct((T, d_qkv), F32), jax.ShapeDtypeStruct((T, d_u), F32)],
        compiler_params=pltpu.CompilerParams(dimension_semantics=("arbitrary",),
                                             vmem_limit_bytes=V7X_VMEM_LIMIT),
    )(lidx, x, g1, wg, wu, wd, gm, win)


CONV_PAD = 32
CONV_ROWS = 64


def _conv_rows(z_ref, r0, w_ref, b_ref, lnw_ref, lnb_ref):
    d_c = z_ref.shape[-1]
    shift = CONV_PAD - (CONV_WIDTH - 1)
    n_win = CONV_ROWS + CONV_PAD
    win = z_ref[r0:r0 + n_win, :]
    acc = jnp.zeros((CONV_ROWS, d_c), F32) + b_ref[...]
    for sub in range(SUBLANES):
        rolled = win if sub == 0 else pltpu.roll(win, n_win - sub, 0)
        for j in range(CONV_WIDTH):
            off = shift + j
            if off % SUBLANES == sub:
                base = off - sub
                acc = acc + rolled[base:base + CONV_ROWS, :] * w_ref[j:j + 1, :]
    mean = jnp.mean(acc, axis=-1, keepdims=True)
    d = acc - mean
    var = jnp.mean(d * d, axis=-1, keepdims=True)
    z = d * lax.rsqrt(var + CONV_LN_EPS) * lnw_ref[...] + lnb_ref[...]
    return z * _sigmoid(z)


def _mix_out_kernel(tiles_per_seq, l_ref, x_ref, yr_ref, ya_ref, u_ref, cw_ref, cb_ref, clnw_ref,
                    clnb_ref, wo_ref, o_ref, z_s):
    del l_ref
    i = pl.program_id(0)
    tm = x_ref.shape[0]
    d_c = z_s.shape[-1]

    @pl.when(i == 0)
    def _():
        z_s[...] = jnp.zeros_like(z_s)

    left = z_s[tm:tm + CONV_PAD, :]
    z_s[0:CONV_PAD, :] = jnp.where(i % tiles_per_seq == 0, jnp.zeros_like(left), left)
    z_s[CONV_PAD:, :] = u_ref[:, 0:d_c] * _sigmoid(u_ref[:, d_c:])
    y_c = jnp.concatenate([_conv_rows(z_s, r0, cw_ref, cb_ref, clnw_ref, clnb_ref)
                           for r0 in range(0, tm, CONV_ROWS)], axis=0)
    mix = jnp.concatenate([yr_ref[...].astype(BF16), ya_ref[...].astype(BF16), y_c.astype(BF16)], axis=1)
    o_ref[...] = x_ref[...] + _dot(mix, wo_ref[...])


def _mix_out(lidx, x, yr, ya, u, cw, cb, clnw, clnb, wo, tm, seq_len):
    T, D = x.shape
    d_c = u.shape[1] // 2
    assert seq_len % tm == 0 and tm % CONV_ROWS == 0
    row = lambda w: pl.BlockSpec((tm, w), lambda i, l: (i, 0))
    grid_spec = pltpu.PrefetchScalarGridSpec(
        num_scalar_prefetch=1, grid=(T // tm,),
        in_specs=[row(D), row(yr.shape[1]), row(ya.shape[1]), row(2 * d_c),
                  _layer_spec((CONV_WIDTH, d_c), buffered=False), _layer_spec((1, d_c), buffered=False),
                  _layer_spec((1, d_c), buffered=False), _layer_spec((1, d_c), buffered=False),
                  _layer_spec(wo.shape[1:])],
        out_specs=row(D),
        scratch_shapes=[pltpu.VMEM((CONV_PAD + tm, d_c), F32)])
    return pl.pallas_call(
        functools.partial(_mix_out_kernel, seq_len // tm), grid_spec=grid_spec, name="mix_out",
        out_shape=jax.ShapeDtypeStruct((T, D), F32),
        compiler_params=pltpu.CompilerParams(dimension_semantics=("arbitrary",),
                                             vmem_limit_bytes=V7X_VMEM_LIMIT),
    )(lidx, x, yr, ya, u, cw, cb, clnw, clnb, wo)


def _ffn_kernel(l_ref, x_ref, g_ref, wg_ref, wu_ref, wd_ref, o_ref):
    del l_ref
    o_ref[...] = _swiglu_residual(x_ref[...], g_ref[...], wg_ref, wu_ref, wd_ref)


def _ffn(lidx, x, g, wg, wu, wd, tm):
    T, D = x.shape
    row = pl.BlockSpec((tm, D), lambda i, l: (i, 0))
    grid_spec = pltpu.PrefetchScalarGridSpec(
        num_scalar_prefetch=1, grid=(T // tm,),
        in_specs=[row, _layer_spec((1, D)), _layer_spec(wg.shape[1:]), _layer_spec(wu.shape[1:]),
                  _layer_spec(wd.shape[1:])],
        out_specs=row)
    return pl.pallas_call(
        _ffn_kernel, grid_spec=grid_spec, name="ffn2",
        out_shape=jax.ShapeDtypeStruct((T, D), F32),
        compiler_params=pltpu.CompilerParams(dimension_semantics=("arbitrary",),
                                             vmem_limit_bytes=V7X_VMEM_LIMIT),
    )(lidx, x, g, wg, wu, wd)


def _rwkv_kernel(l_ref, p_ref, mu_ref, w0_ref, w2_ref, a0_ref, a2_ref, g2_ref, kk_ref, ka_ref,
                 rk_ref, lnw_ref, lnb_ref, bd_ref, tri_ref, o_ref,
                 carry_ref, state_ref):
    del l_ref
    tb = p_ref.shape[1]
    d_r = o_ref.shape[-1]

    @pl.when(pl.program_id(1) == 0)
    def _():
        carry_ref[...] = jnp.zeros_like(carry_ref)
        state_ref[...] = jnp.zeros_like(state_ref)

    p = p_ref[0]
    row = lax.broadcasted_iota(jnp.int32, (tb, 1), 0)
    prev = jnp.where(row == 0, carry_ref[...], pltpu.roll(p, 1, 0))
    carry_ref[...] = p[tb - 1:tb, :]
    ps = p + (prev - p) * mu_ref[...]

    r = ps[:, 0:d_r]
    k = ps[:, d_r:2 * d_r]
    v = ps[:, 2 * d_r:3 * d_r]
    xs = ps[:, 3 * d_r:3 * d_r + LORA_SLAB]

    bd = bd_ref[...]
    def seg_sum(t):
        tb16 = t.astype(BF16)
        return jnp.concatenate([_dot(tb16[:, s:s + PAIR], bd) for s in range(0, d_r, PAIR)], axis=1)

    lw = w0_ref[...] + _dot_f32(jnp.tanh(xs), w2_ref[...])
    ld = (-0.5 * EXP_NEG_HALF) * jnp.tanh(0.5 * lw) - 0.5 * EXP_NEG_HALF
    a = _sigmoid(a0_ref[...] + _dot_bf16(xs, a2_ref[...]))
    g = _dot_bf16(_sigmoid(xs), g2_ref[...])
    kk = k * kk_ref[...]
    kk = kk * lax.rsqrt(seg_sum(kk * kk) + KK_EPS)
    ka = ka_ref[...]
    k2 = k * ((1.0 - ka) + a * ka)
    b = kk * a

    lcum = jnp.concatenate(
        [_dot_f32_rhs(tri_ref[...], ld[s:s + CUMSUM_ROWS]) for s in range(0, tb, CUMSUM_ROWS)], axis=0)
    n_chunks = tb // CHUNK
    pc_rows = [jnp.exp(lcum[(c + 1) * CHUNK - 1:(c + 1) * CHUNK, :]) for c in range(n_chunks)]
    at_all = (-kk * jnp.exp(lcum - ld)).astype(BF16)
    rt_all = r * jnp.exp(lcum)
    inv_p = jnp.exp(-lcum)
    bt_all = (b * inv_p).astype(BF16)
    kt_all = (k2 * inv_p).astype(BF16)
    rest = inv_p * jnp.concatenate([jnp.broadcast_to(pc, (CHUNK, d_r)) for pc in pc_rows], axis=0)
    bp_all = (b * rest).astype(BF16)
    kp_all = (k2 * rest).astype(BF16)
    v_all = v.astype(BF16)

    lane = lax.broadcasted_iota(jnp.int32, (CHUNK, PAIR), 1)
    trow = lax.broadcasted_iota(jnp.int32, (CHUNK, PAIR), 0)
    head0 = lane < HEAD_DIM
    scol = lane & (HEAD_DIM - 1)
    strict = trow > scol
    incl = trow >= scol
    eye2 = jnp.where(trow == scol, 1.0, 0.0)
    r2 = lax.broadcasted_iota(jnp.int32, (PAIR, PAIR), 0)
    c2 = lax.broadcasted_iota(jnp.int32, (PAIR, PAIR), 1)
    same_head = (r2 < HEAD_DIM) == (c2 < HEAD_DIM)

    def block_diag(xb):
        zero = jnp.zeros_like(xb)
        return jnp.concatenate([jnp.where(head0, xb, zero), jnp.where(head0, zero, xb)], axis=0)

    n_pairs = d_r // PAIR
    tile = lambda t, c, p: t[c * CHUNK:(c + 1) * CHUNK, p * PAIR:(p + 1) * PAIR]
    inv_n = 1.0 / HEAD_DIM

    for c_lo in range(0, n_chunks, GROUP_CHUNKS):
        group_chunks = range(c_lo, min(c_lo + GROUP_CHUNKS, n_chunks))
        units = [(c, p) for c in group_chunks for p in range(n_pairs)]
        y = _rwkv_group(units, group_chunks, n_pairs, tile, block_diag, state_ref, pc_rows,
                        (at_all, rt_all, bt_all, kt_all, bp_all, kp_all, v_all),
                        (strict, incl, eye2, same_head))
        rows = slice(group_chunks[0] * CHUNK, (group_chunks[-1] + 1) * CHUNK)
        mean = seg_sum(y) * inv_n
        d = y - mean
        var = seg_sum(d * d) * inv_n
        yn = d * lax.rsqrt(var + RWKV_GN_EPS) * lnw_ref[...] + lnb_ref[...]
        bonus = seg_sum(r[rows] * k2[rows] * rk_ref[...]) * v[rows]
        o_ref[0, rows, :] = (yn + bonus) * g[rows]


def _rwkv_group(units, group_chunks, n_pairs, tile, block_diag, state_ref, pc_rows, scaled, masks):
    at_all, rt_all, bt_all, kt_all, bp_all, kp_all, v_all = scaled
    strict, incl, eye2, same_head = masks
    a_ab, a_ak, a_rb, a_rk = {}, {}, {}, {}
    for u in units:
        ar = jnp.concatenate([tile(at_all, *u), tile(rt_all, *u).astype(BF16)], axis=0)
        bk = jnp.concatenate([block_diag(tile(bt_all, *u)), block_diag(tile(kt_all, *u))], axis=0)
        g2 = _dot_nt(ar, bk)
        a_ab[u] = jnp.where(strict, g2[:CHUNK, :PAIR], 0.0)
        a_ak[u] = jnp.where(strict, g2[:CHUNK, PAIR:], 0.0).astype(BF16)
        a_rb[u] = jnp.where(incl, g2[CHUNK:, :PAIR], 0.0).astype(BF16)
        a_rk[u] = jnp.where(incl, g2[CHUNK:, PAIR:], 0.0).astype(BF16)
    tinv = {u: eye2 + a_ab[u] for u in units}
    apow = {}
    for u in units:
        ab = a_ab[u].astype(BF16)
        apow[u] = _dot(ab, block_diag(ab))
    for step in range(1, NEUMANN_STEPS):
        last = step == NEUMANN_STEPS - 1
        for u in units:
            ab = apow[u].astype(BF16)
            lhs = tinv[u].astype(BF16) if last else jnp.concatenate([tinv[u].astype(BF16), ab], axis=0)
            prod = _dot(lhs, block_diag(ab))
            tinv[u] = tinv[u] + prod[:CHUNK]
            if not last:
                apow[u] = prod[CHUNK:]
    av, y1v, w1, u1, rq, y1, mx, gx = {}, {}, {}, {}, {}, {}, {}, {}
    for u in units:
        avk = _dot(jnp.concatenate([a_ak[u], a_rk[u]], axis=0), block_diag(tile(v_all, *u)))
        av[u], y1v[u] = avk[:CHUNK].astype(BF16), avk[CHUNK:]
    for u in units:
        rhs = jnp.concatenate([block_diag(tile(at_all, *u)), block_diag(av[u])], axis=1)
        wu = _dot(tinv[u].astype(BF16), rhs).astype(BF16)
        w1[u], u1[u] = wu[:, :PAIR], wu[:, PAIR:]
    for u in units:
        vc = tile(v_all, *u)
        bpkp = jnp.concatenate([tile(bp_all, *u), tile(kp_all, *u)], axis=0)
        ry = _dot(a_rb[u], jnp.concatenate([block_diag(w1[u]), block_diag(u1[u])], axis=1))
        rq[u] = (tile(rt_all, *u) + ry[:, :PAIR]).astype(BF16)
        y1[u] = ry[:, PAIR:] + y1v[u]
        lhs = jnp.concatenate([jnp.concatenate([w1[u], u1[u]], axis=1),
                               jnp.concatenate([jnp.zeros_like(vc), vc], axis=1)], axis=0)
        mg = _dot_tn(lhs, bpkp)
        mx[u] = jnp.where(same_head, mg[:PAIR], 0.0).astype(BF16)
        gx[u] = jnp.where(same_head, mg[PAIR:], 0.0)

    y_rows = []
    for c in group_chunks:
        y_parts = []
        for p in range(n_pairs):
            u = (c, p)
            s0 = state_ref[p]
            s0b = s0.astype(BF16)
            y_parts.append(_dot_nt(rq[u], s0b) + y1[u])
            pc = pc_rows[c][:, p * PAIR:(p + 1) * PAIR]
            state_ref[p] = s0 * pc + _dot(s0b, mx[u]) + gx[u]
        y_rows.append(jnp.concatenate(y_parts, axis=1))
    return jnp.concatenate(y_rows, axis=0)


def _rwkv(lidx, ps3, mu, w0, w2p, a0, a2p, g2p, k_k, k_a, r_k, ln_w, ln_b, d_r, tb):
    B, S, d_shift = ps3.shape
    hid = jnp.arange(PAIR) // HEAD_DIM
    bd = (hid[:, None] == hid[None, :]).astype(BF16)
    assert tb % CUMSUM_ROWS == 0
    ci = jnp.arange(CUMSUM_ROWS)
    same = (ci[:, None] // CHUNK) == (ci[None, :] // CHUNK)
    tri = (same & (ci[:, None] >= ci[None, :])).astype(BF16)
    const = lambda a: pl.BlockSpec(a.shape, lambda b, t, l: (0,) * a.ndim)
    vec = _layer_spec((1, d_r), buffered=False)
    lora = _layer_spec((LORA_SLAB, d_r), buffered=False)
    grid_spec = pltpu.PrefetchScalarGridSpec(
        num_scalar_prefetch=1, grid=(B, S // tb),
        in_specs=[pl.BlockSpec((1, tb, d_shift), lambda b, t, l: (b, t, 0)),
                  _layer_spec((1, d_shift), buffered=False),
                  vec, lora, vec, lora, lora, vec, vec, vec, vec, vec,
                  const(bd), const(tri)],
        out_specs=pl.BlockSpec((1, tb, d_r), lambda b, t, l: (b, t, 0)),
        scratch_shapes=[pltpu.VMEM((1, d_shift), F32),
                        pltpu.VMEM((d_r // PAIR, PAIR, PAIR), F32)])
    return pl.pallas_call(
        _rwkv_kernel, grid_spec=grid_spec, name="rwkv7",
        out_shape=jax.ShapeDtypeStruct((B, S, d_r), F32),
        compiler_params=pltpu.CompilerParams(dimension_semantics=("arbitrary", "arbitrary"),
                                             vmem_limit_bytes=V7X_VMEM_LIMIT),
    )(lidx, ps3, mu, w0, w2p, a0, a2p, g2p, k_k, k_a, r_k, ln_w, ln_b, bd, tri)


def _attn_kernel(l_ref, exact_ref, q_ref, k_ref, v_ref, qn_ref, kn_ref, bd_ref, bias_ref, o_ref,
                 q_s, k_s, q4_s, k4_s, v4_s, st_s):
    S = q_ref.shape[1]
    bd = bd_ref[...]
    inv_n = 1.0 / HEAD_DIM

    def head_rms(t, gain):
        ms = _dot_bf16(t * t, bd) * inv_n
        return t * lax.rsqrt(ms + NORM_EPS) * gain

    q_s[...] = head_rms(q_ref[0], qn_ref[...]) * (HEAD_DIM ** -0.5)
    k_s[...] = head_rms(k_ref[0], kn_ref[...])
    head0 = lax.broadcasted_iota(jnp.int32, (ATTN_BLOCK, PAIR), 1) < HEAD_DIM

    s_pre = S // PRE_DILATION
    v_tok = v_ref.at[0]

    def regroup(i, carry):
        c = i // (s_pre // REGROUP_ROWS)
        r0 = (i % (s_pre // REGROUP_ROWS)) * REGROUP_ROWS
        src = pl.ds(c + PRE_DILATION * r0, REGROUP_ROWS, stride=PRE_DILATION)
        dst = pl.ds(pl.multiple_of(c * s_pre + r0, REGROUP_ROWS), REGROUP_ROWS)
        q4_s[dst, :] = q_s[src, :]
        k4_s[dst, :] = k_s[src, :]
        v4_s[dst, :] = v_tok[src, :]
        return carry

    lax.fori_loop(0, S // REGROUP_ROWS, regroup, 0)

    exact = exact_ref[l_ref[0]] == 1
    pl.when(exact)(lambda: _attn_patterns(True, S, s_pre, head0, v_tok, bias_ref, o_ref,
                                          q_s, k_s, q4_s, k4_s, v4_s, st_s))
    pl.when(jnp.logical_not(exact))(lambda: _attn_patterns(False, S, s_pre, head0, v_tok, bias_ref, o_ref,
                                                           q_s, k_s, q4_s, k4_s, v4_s, st_s))


def _attn_patterns(row_max, S, s_pre, head0, v_tok, bias_ref, o_ref, q_s, k_s, q4_s, k4_s, v4_s, st_s):
    group = ATTN_GROUP_EXACT if row_max else ATTN_GROUP
    for pi, (window, dil) in enumerate(DILATED_PATTERNS):
        n_sub = S // dil
        n_blk = n_sub // ATTN_BLOCK
        regrouped = dil % PRE_DILATION == 0
        if regrouped:
            stride = dil // PRE_DILATION
            q_src, k_src, v_src = q4_s, k4_s, v4_s
        else:
            stride = dil
            q_src, k_src, v_src = q_s, k_s, v_tok
        state = tuple(st_s.at[3 * pi + j] for j in range(3))
        span = stride * ATTN_BLOCK

        def group_body(gi, carry):
            blocks = []
            for g in range(group):
                i = gi * group + g
                n = i // dil
                res = i % dil
                if regrouped:
                    q_start = (res % PRE_DILATION) * s_pre + res // PRE_DILATION + span * n
                else:
                    q_start = res + span * n
                first = jnp.where(n == 0, 1, 0)
                k_start = q_start - span * (1 - first)
                if stride > 1:
                    q_rows = pl.ds(q_start, ATTN_BLOCK, stride=stride)
                    k_rows = pl.ds(k_start, 2 * ATTN_BLOCK, stride=stride)
                else:
                    q_rows = pl.ds(pl.multiple_of(q_start, ATTN_BLOCK), ATTN_BLOCK)
                    k_rows = pl.ds(pl.multiple_of(k_start, ATTN_BLOCK), 2 * ATTN_BLOCK)
                blocks.append(dict(first=first, q_rows=q_rows, k_rows=k_rows))
            for blk in blocks:
                q2 = q_src[blk["q_rows"], :]
                zero = jnp.zeros_like(q2)
                blk["q"] = jnp.concatenate([jnp.where(head0, q2, zero), jnp.where(head0, zero, q2)],
                                           axis=0).astype(BF16)
                blk["k"] = k_src[blk["k_rows"], :].astype(BF16)
                blk["v"] = v_src[blk["k_rows"], :].astype(BF16)
            for blk in blocks:
                blk["s"] = _dot_nt(blk["q"], blk["k"]) + bias_ref[pi, blk["first"]]
            if row_max:
                for blk in blocks:
                    blk["m"] = jnp.max(blk["s"], axis=-1, keepdims=True)
                for blk in blocks:
                    blk["e"] = jnp.exp(blk["s"] - blk["m"])
            else:
                for blk in blocks:
                    blk["e"] = jnp.exp(blk["s"])
            for blk in blocks:
                blk["l"] = jnp.sum(blk["e"], axis=-1, keepdims=True)
                blk["o"] = _dot(blk["e"].astype(BF16), blk["v"])
            for blk in blocks:
                pair = lambda t: jnp.where(head0, jnp.broadcast_to(t[:ATTN_BLOCK], (ATTN_BLOCK, PAIR)),
                                           jnp.broadcast_to(t[ATTN_BLOCK:], (ATTN_BLOCK, PAIR)))
                blk["out"] = (pair(blk["o"]), pair(blk["m"]) if row_max else None, pair(blk["l"]))
            for blk in blocks:
                for ref, val in zip(state, blk["out"]):
                    if val is not None:
                        ref[blk["q_rows"], :] = val
            return carry

        lax.fori_loop(0, dil * n_blk // group, group_body, 0)

    def finish(i, carry):
        c = i // (s_pre // ATTN_BLOCK)
        r0 = (i % (s_pre // ATTN_BLOCK)) * ATTN_BLOCK
        tok = pl.ds(c + PRE_DILATION * r0, ATTN_BLOCK, stride=PRE_DILATION)
        grp = pl.ds(pl.multiple_of(c * s_pre + r0, ATTN_BLOCK), ATTN_BLOCK)
        total = None
        for pi, (_, dil) in enumerate(DILATED_PATTERNS):
            rows = grp if dil % PRE_DILATION == 0 else tok
            if row_max:
                part = tuple(st_s[3 * pi + j, rows, :] for j in range(3))
                total = part if total is None else _softmax_merge(total, part)
            else:
                part = (st_s[3 * pi, rows, :], None, st_s[3 * pi + 2, rows, :])
                total = part if total is None else (total[0] + part[0], None, total[2] + part[2])
        o_ref[0, tok, :] = total[0] / total[2]
        return carry

    lax.fori_loop(0, S // ATTN_BLOCK, finish, 0)


def _softmax_merge(a, b):
    acc_a, m_a, l_a = a
    acc_b, m_b, l_b = b
    m_new = jnp.maximum(m_a, m_b)
    w_a = jnp.exp(m_a - m_new)
    w_b = jnp.exp(m_b - m_new)
    return acc_a * w_a + acc_b * w_b, m_new, l_a * w_a + l_b * w_b


def _alibi_slopes(n):
    def pow2(m):
        start = 2.0 ** (-8.0 / m)
        return [start ** (i + 1) for i in range(m)]
    if math.log2(n).is_integer():
        return pow2(n)
    c = 2 ** int(math.floor(math.log2(n)))
    return pow2(c) + pow2(2 * c)[0::2][: n - c]


def _attn_bias(n_heads):
    qi = jnp.arange(ATTN_BLOCK)
    ki = jnp.arange(2 * ATTN_BLOCK)
    dist = qi[:, None] + ATTN_BLOCK - ki[None, :]
    slopes = jnp.asarray(_alibi_slopes(n_heads), F32)
    out = []
    for window, dil in DILATED_PATTERNS:
        valid = (dist >= 0) & (dist <= window // dil)
        bias = -slopes[:, None, None] * (dist * dil).astype(F32)[None]
        rest = jnp.where(valid[None], bias, MASK_VALUE)
        first = jnp.concatenate([rest[..., ATTN_BLOCK:], jnp.full_like(rest[..., ATTN_BLOCK:], MASK_VALUE)], -1)
        out.append(jnp.stack([rest, first], 0).reshape(2, n_heads * ATTN_BLOCK, 2 * ATTN_BLOCK))
    return jnp.stack(out, 0)


def _attention(lidx, qkv3, q_norm, k_norm, score_bound, d_a):
    B, S, _ = qkv3.shape
    n_heads = d_a // HEAD_DIM
    n_pairs = n_heads // 2
    slab = 2 * HEAD_DIM
    hid = jnp.arange(slab) // HEAD_DIM
    bd = (hid[:, None] == hid[None, :]).astype(BF16)
    bias = _attn_bias(n_heads)[None] - score_bound[:, None, None, None, None]
    exact = (score_bound > SAFE_SOFTMAX_SHIFT).astype(jnp.int32)
    col = lambda off: pl.BlockSpec((1, S, slab), lambda b, p, l, e: (b, 0, off + p))
    gain = pl.BlockSpec((None, 1, slab), lambda b, p, l, e: (l[0], 0, 0))
    grid_spec = pltpu.PrefetchScalarGridSpec(
        num_scalar_prefetch=2, grid=(B, n_pairs),
        in_specs=[col(0), col(n_pairs), col(2 * n_pairs), gain, gain,
                  pl.BlockSpec(bd.shape, lambda b, p, l, e: (0, 0)),
                  pl.BlockSpec((None, len(DILATED_PATTERNS), 2, 2 * ATTN_BLOCK, 2 * ATTN_BLOCK),
                               lambda b, p, l, e: (l[0], 0, 0, p, 0))],
        out_specs=pl.BlockSpec((1, S, slab), lambda b, p, l, e: (b, 0, p)),
        scratch_shapes=[pltpu.VMEM((S, slab), F32)] * 5
        + [pltpu.VMEM((3 * len(DILATED_PATTERNS), S, slab), F32)])
    return pl.pallas_call(
        _attn_kernel, grid_spec=grid_spec, name="dilated_attn",
        out_shape=jax.ShapeDtypeStruct((B, S, d_a), F32),
        compiler_params=pltpu.CompilerParams(dimension_semantics=("arbitrary", "arbitrary"),
                                             vmem_limit_bytes=V7X_VMEM_LIMIT),
    )(lidx, exact, qkv3, qkv3, qkv3, q_norm, k_norm, bd, bias)


def _pad_lora(w, offset):
    L, r, d = w.shape
    return jnp.zeros((L, LORA_SLAB, d), F32).at[:, offset:offset + r, :].set(w)


def kernel(x, norm_ffn1, ffn1_w_gate, ffn1_w_up, ffn1_w_down, norm_mix, w_in, shift_mu, rwkv_w0, rwkv_w2, rwkv_a0, rwkv_a2, rwkv_g2, rwkv_k_k, rwkv_k_a, rwkv_r_k, rwkv_ln_w, rwkv_ln_b, attn_q_norm, attn_k_norm, conv_dw_w, conv_dw_b, conv_ln_w, conv_ln_b, w_out, norm_ffn2, ffn2_w_gate, ffn2_w_up, ffn2_w_down):
    B, S, D = x.shape
    depth = w_in.shape[0]
    d_r = rwkv_w0.shape[-1]
    d_c = conv_dw_b.shape[-1]
    d_shift = shift_mu.shape[-1]
    d_a = (w_in.shape[-1] - d_shift - 2 * d_c) // 3
    r_w, r_a, r_g = rwkv_w2.shape[1], rwkv_a2.shape[1], rwkv_g2.shape[1]
    assert d_shift == 3 * d_r + LORA_SLAB and r_w + r_a + r_g == LORA_SLAB
    assert S % (2 * ATTN_BLOCK * DILATED_PATTERNS[-1][1]) == 0 and (d_a // HEAD_DIM) % 2 == 0
    assert (S // ATTN_BLOCK) % max(ATTN_GROUP, ATTN_GROUP_EXACT) == 0
    assert S % (PRE_DILATION * REGROUP_ROWS) == 0
    assert all(d < PRE_DILATION or d % PRE_DILATION == 0 for _, d in DILATED_PATTERNS)
    T = B * S
    assert T % FFN_ROWS == 0 and S % RWKV_ROWS == 0

    vec = lambda a: a.reshape(depth, 1, -1)
    bf = lambda a: a.astype(BF16)
    wg1, wu1, wd1 = bf(ffn1_w_gate), bf(ffn1_w_up), bf(ffn1_w_down)
    wg2, wu2, wd2 = bf(ffn2_w_gate), bf(ffn2_w_up), bf(ffn2_w_down)
    win, wo = bf(w_in), bf(w_out)
    g1, gm, g2 = vec(norm_ffn1), vec(norm_mix), vec(norm_ffn2)
    w2p = _pad_lora(rwkv_w2, 0)
    a2p = _pad_lora(rwkv_a2, r_w)
    g2p = _pad_lora(rwkv_g2, r_w + r_a)
    qn = vec(jnp.tile(attn_q_norm, (1, 2)))
    kn = vec(jnp.tile(attn_k_norm, (1, 2)))
    score_bound = (HEAD_DIM ** 0.5) * jnp.max(jnp.abs(attn_q_norm), -1) * jnp.max(jnp.abs(attn_k_norm), -1)

    layer_index = lambda l: jnp.reshape(l, (1,)).astype(jnp.int32)

    def project(l, xf):
        return _ffn_proj(layer_index(l), xf, g1, wg1, wu1, wd1, gm, win, (d_shift, 3 * d_a, 2 * d_c),
                         FFN_ROWS)

    def mix(l, projected):
        lidx = layer_index(l)
        x1, ps, qkv, u = projected
        y_r = _rwkv(lidx, ps.reshape(B, S, d_shift), vec(shift_mu), vec(rwkv_w0), w2p, vec(rwkv_a0),
                    a2p, g2p, vec(rwkv_k_k), vec(rwkv_k_a), vec(rwkv_r_k), vec(rwkv_ln_w),
                    vec(rwkv_ln_b), d_r, RWKV_ROWS)
        y_a = _attention(lidx, qkv.reshape(B, S, 3 * d_a), qn, kn, score_bound, d_a)
        x2 = _mix_out(lidx, x1, y_r.reshape(T, d_r), y_a.reshape(T, d_a), u, conv_dw_w,
                      vec(conv_dw_b), vec(conv_ln_w), vec(conv_ln_b), wo, MIX_ROWS, S)
        return _ffn(lidx, x2, g2, wg2, wu2, wd2, 2 * FFN_ROWS)

    state = project(jnp.int32(0), x.reshape(T, D))
    state = lax.fori_loop(0, depth - 1, lambda l, st: tuple(project(l + 1, mix(l, st))), tuple(state))
    return mix(jnp.int32(depth - 1), state).reshape(B, S, D)
```

```python
import functools
import math

import jax
import jax.numpy as jnp
from jax import lax
from jax.experimental import pallas as pl
from jax.experimental.pallas import tpu as pltpu

F32 = jnp.float32
BF16 = jnp.bfloat16

LANES = 128
SUBLANES = 8
FFN_ROWS = 512
MIX_ROWS = 1024
RWKV_ROWS = 1024
HEAD_DIM = 64
PAIR = 2 * HEAD_DIM
NORM_EPS = 1e-6
RWKV_GN_EPS = 64e-5
CONV_LN_EPS = 1e-5
KK_EPS = 1e-12
CONV_WIDTH = 31
DILATED_PATTERNS = ((128, 1), (512, 4), (2048, 16))
ATTN_BLOCK = 128
ATTN_GROUP = 16
ATTN_GROUP_EXACT = 4
PRE_DILATION = 4
REGROUP_ROWS = 256
LORA_SLAB = 128
CHUNK = 64
GROUP_CHUNKS = 8
CUMSUM_ROWS = 256
NEUMANN_STEPS = 6
MASK_VALUE = -1e30
SAFE_SOFTMAX_SHIFT = 40.0
EXP_NEG_HALF = math.exp(-0.5)
V7X_VMEM_LIMIT = 56 * 1024 * 1024

NT_DIMS = (((1,), (1,)), ((), ()))
TN_DIMS = (((0,), (0,)), ((), ()))


def _dot(a, b):
    return jnp.dot(a, b, preferred_element_type=F32)


def _dot_nt(a, b):
    return lax.dot_general(a, b, NT_DIMS, preferred_element_type=F32)


def _dot_tn(a, b):
    return lax.dot_general(a, b, TN_DIMS, preferred_element_type=F32)


def _split2(x):
    hi = x.astype(BF16)
    lo = (x - hi.astype(F32)).astype(BF16)
    return hi, lo


def _dot_bf16(x, w):
    return _dot(x.astype(BF16), w.astype(BF16))


def _dot_f32_rhs(w_exact, x):
    hi, lo = _split2(x)
    return _dot(w_exact, hi) + _dot(w_exact, lo)


def _dot_f32(x, w):
    xh, xl = _split2(x)
    wh, wl = _split2(w)
    return _dot(xh, wh) + _dot(xl, wh) + _dot(xh, wl)


def _sigmoid(x):
    return 0.5 + 0.5 * jnp.tanh(0.5 * x)


def _rms_norm(x, g):
    return x * lax.rsqrt(jnp.mean(x * x, axis=-1, keepdims=True) + NORM_EPS) * g


def _swiglu_residual(x, g, wg_ref, wu_ref, wd_ref):
    xn = _rms_norm(x, g).astype(BF16)
    gate = _dot(xn, wg_ref[...])
    up = _dot(xn, wu_ref[...])
    h =(gate * _sigmoid(gate) * up).astype(BF16)
    return x + 0.5 * _dot(h, wd_ref[...])


def _ffn_proj_kernel(l_ref, x_ref, g1_ref, wg_ref, wu_ref, wd_ref, gm_ref, win_ref,
                     x1_ref, ps_ref, qkv_ref, u_ref):
    del l_ref
    x1 = _swiglu_residual(x_ref[...], g1_ref[...], wg_ref, wu_ref, wd_ref)
    x1_ref[...] = x1
    h = _rms_norm(x1, gm_ref[...]).astype(BF16)
    proj = _dot(h, win_ref[...])
    d_shift = ps_ref.shape[-1]
    d_qkv = qkv_ref.shape[-1]
    ps_ref[...] = proj[:, :d_shift]
    qkv_ref[...] = proj[:, d_shift:d_shift + d_qkv]
    u_ref[...] = proj[:, d_shift + d_qkv:]


def _layer_spec(shape, buffered=True):
    nd = len(shape)
    kw = dict(pipeline_mode=pl.Buffered(1)) if buffered else {}
    return pl.BlockSpec((None,) + tuple(shape), lambda *a: (a[-1][0],) + (0,) * nd, **kw)


def _ffn_proj(lidx, x, g1, wg, wu, wd, gm, win, dims, tm):
    T, D = x.shape
    d_shift, d_qkv, d_u = dims
    row = lambda w: pl.BlockSpec((tm, w), lambda i, l: (i, 0))
    grid_spec = pltpu.PrefetchScalarGridSpec(
        num_scalar_prefetch=1, grid=(T // tm,),
        in_specs=[row(D), _layer_spec((1, D)), _layer_spec(wg.shape[1:]), _layer_spec(wu.shape[1:]),
                  _layer_spec(wd.shape[1:]), _layer_spec((1, D)), _layer_spec(win.shape[1:])],
        out_specs=[row(D), row(d_shift), row(d_qkv), row(d_u)])
    return pl.pallas_call(
        _ffn_proj_kernel, grid_spec=grid_spec, name="ffn_proj",
        out_shape=[jax.ShapeDtypeStruct((T, D), F32), jax.ShapeDtypeStruct((T, d_shift), F32),
                   jax.ShapeDtypeStruct((T, d_qkv), F32), jax.ShapeDtypeStruct((T, d_u), F32)],
        compiler_params=pltpu.CompilerParams(dimension_semantics=("arbitrary",),
                                             vmem_limit_bytes=V7X_VMEM_LIMIT),
    )(lidx, x, g1, wg, wu, wd, gm, win)


CONV_PAD = 32
CONV_ROWS = 64


def _conv_rows(z_ref, r0, w_ref, b_ref, lnw_ref, lnb_ref):
    d_c = z_ref.shape[-1]
    shift = CONV_PAD - (CONV_WIDTH - 1)
    n_win = CONV_ROWS + CONV_PAD
    win = z_ref[r0:r0 + n_win, :]
    acc = jnp.zeros((CONV_ROWS, d_c), F32) + b_ref[...]
    for sub in range(SUBLANES):
        rolled = win if sub == 0 else pltpu.roll(win, n_win - sub, 0)
        for j in range(CONV_WIDTH):
            off = shift + j
            if off % SUBLANES == sub:
                base = off - sub
                acc = acc + rolled[base:base + CONV_ROWS, :] * w_ref[j:j + 1, :]
    mean = jnp.mean(acc, axis=-1, keepdims=True)
    d = acc - mean
    var = jnp.mean(d * d, axis=-1, keepdims=True)
    z = d * lax.rsqrt(var + CONV_LN_EPS) * lnw_ref[...] + lnb_ref[...]
    return z * _sigmoid(z)


def _mix_out_kernel(tiles_per_seq, l_ref, x_ref, yr_ref, ya_ref, u_ref, cw_ref, cb_ref, clnw_ref,
                    clnb_ref, wo_ref, o_ref, z_s):
    del l_ref
    i = pl.program_id(0)
    tm = x_ref.shape[0]
    d_c = z_s.shape[-1]

    @pl.when(i == 0)
    def _():
        z_s[...] = jnp.zeros_like(z_s)

    left = z_s[tm:tm + CONV_PAD, :]
    z_s[0:CONV_PAD, :] = jnp.where(i % tiles_per_seq == 0, jnp.zeros_like(left), left)
    z_s[CONV_PAD:, :] = u_ref[:, 0:d_c] * _sigmoid(u_ref[:, d_c:])
    y_c = jnp.concatenate([_conv_rows(z_s, r0, cw_ref, cb_ref, clnw_ref, clnb_ref)
                           for r0 in range(0, tm, CONV_ROWS)], axis=0)
    mix = jnp.concatenate([yr_ref[...].astype(BF16), ya_ref[...].astype(BF16), y_c.astype(BF16)], axis=1)
    o_ref[...] = x_ref[...] + _dot(mix, wo_ref[...])


def _mix_out(lidx, x, yr, ya, u, cw, cb, clnw, clnb, wo, tm, seq_len):
    T, D = x.shape
    d_c = u.shape[1] // 2
    assert seq_len % tm == 0 and tm % CONV_ROWS == 0
    row = lambda w: pl.BlockSpec((tm, w), lambda i, l: (i, 0))
    grid_spec = pltpu.PrefetchScalarGridSpec(
        num_scalar_prefetch=1, grid=(T // tm,),
        in_specs=[row(D), row(yr.shape[1]), row(ya.shape[1]), row(2 * d_c),
                  _layer_spec((CONV_WIDTH, d_c), buffered=False), _layer_spec((1, d_c), buffered=False),
                  _layer_spec((1, d_c), buffered=False), _layer_spec((1, d_c), buffered=False),
                  _layer_spec(wo.shape[1:])],
        out_specs=row(D),
        scratch_shapes=[pltpu.VMEM((CONV_PAD + tm, d_c), F32)])
    return pl.pallas_call(
        functools.partial(_mix_out_kernel, seq_len // tm), grid_spec=grid_spec, name="mix_out",
        out_shape=jax.ShapeDtypeStruct((T, D), F32),
        compiler_params=pltpu.CompilerParams(dimension_semantics=("arbitrary",),
                                             vmem_limit_bytes=V7X_VMEM_LIMIT),
    )(lidx, x, yr, ya, u, cw, cb, clnw, clnb, wo)


def _ffn_kernel(l_ref, x_ref, g_ref, wg_ref, wu_ref, wd_ref, o_ref):
    del l_ref
    o_ref[...] = _swiglu_residual(x_ref[...], g_ref[...], wg_ref, wu_ref, wd_ref)


def _ffn(lidx, x, g, wg, wu, wd, tm):
    T, D = x.shape
    row = pl.BlockSpec((tm, D), lambda i, l: (i, 0))
    grid_spec = pltpu.PrefetchScalarGridSpec(
        num_scalar_prefetch=1, grid=(T // tm,),
        in_specs=[row, _layer_spec((1, D)), _layer_spec(wg.shape[1:]), _layer_spec(wu.shape[1:]),
                  _layer_spec(wd.shape[1:])],
        out_specs=row)
    return pl.pallas_call(
        _ffn_kernel, grid_spec=grid_spec, name="ffn2",
        out_shape=jax.ShapeDtypeStruct((T, D), F32),
        compiler_params=pltpu.CompilerParams(dimension_semantics=("arbitrary",),
                                             vmem_limit_bytes=V7X_VMEM_LIMIT),
    )(lidx, x, g, wg, wu, wd)


def _rwkv_kernel(l_ref, p_ref, mu_ref, w0_ref, w2_ref, a0_ref, a2_ref, g2_ref, kk_ref, ka_ref,
                 rk_ref, lnw_ref, lnb_ref, bd_ref, tri_ref, o_ref,
                 carry_ref, state_ref):
    del l_ref
    tb = p_ref.shape[1]
    d_r = o_ref.shape[-1]

    @pl.when(pl.program_id(1) == 0)
    def _():
        carry_ref[...] = jnp.zeros_like(carry_ref)
        state_ref[...] = jnp.zeros_like(state_ref)

    p = p_ref[0]
    row = lax.broadcasted_iota(jnp.int32, (tb, 1), 0)
    prev = jnp.where(row == 0, carry_ref[...], pltpu.roll(p, 1, 0))
    carry_ref[...] = p[tb - 1:tb, :]
    ps = p + (prev - p) * mu_ref[...]

    r = ps[:, 0:d_r]
    k = ps[:, d_r:2 * d_r]
    v = ps[:, 2 * d_r:3 * d_r]
    xs = ps[:, 3 * d_r:3 * d_r + LORA_SLAB]

    bd = bd_ref[...]
    def seg_sum(t):
        tb16 = t.astype(BF16)
        return jnp.concatenate([_dot(tb16[:, s:s + PAIR], bd) for s in range(0, d_r, PAIR)], axis=1)

    lw = w0_ref[...] + _dot_f32(jnp.tanh(xs), w2_ref[...])
    ld = (-0.5 * EXP_NEG_HALF) * jnp.tanh(0.5 * lw) - 0.5 * EXP_NEG_HALF
    a = _sigmoid(a0_ref[...] + _dot_bf16(xs, a2_ref[...]))
    g = _dot_bf16(_sigmoid(xs), g2_ref[...])
    kk = k * kk_ref[...]
    kk = kk * lax.rsqrt(seg_sum(kk * kk) + KK_EPS)
    ka = ka_ref[...]
    k2 = k * ((1.0 - ka) + a * ka)
    b = kk * a

    lcum = jnp.concatenate(
        [_dot_f32_rhs(tri_ref[...], ld[s:s + CUMSUM_ROWS]) for s in range(0, tb, CUMSUM_ROWS)], axis=0)
    n_chunks = tb // CHUNK
    pc_rows = [jnp.exp(lcum[(c + 1) * CHUNK - 1:(c + 1) * CHUNK, :]) for c in range(n_chunks)]
    at_all = (-kk * jnp.exp(lcum - ld)).astype(BF16)
    rt_all = r * jnp.exp(lcum)
    inv_p = jnp.exp(-lcum)
    bt_all = (b * inv_p).astype(BF16)
    kt_all = (k2 * inv_p).astype(BF16)
    rest = inv_p * jnp.concatenate([jnp.broadcast_to(pc, (CHUNK, d_r)) for pc in pc_rows], axis=0)
    bp_all = (b * rest).astype(BF16)
    kp_all = (k2 * rest).astype(BF16)
    v_all = v.astype(BF16)

    lane = lax.broadcasted_iota(jnp.int32, (CHUNK, PAIR), 1)
    trow = lax.broadcasted_iota(jnp.int32, (CHUNK, PAIR), 0)
    head0 = lane < HEAD_DIM
    scol = lane & (HEAD_DIM - 1)
    strict = trow > scol
    incl = trow >= scol
    eye2 = jnp.where(trow == scol, 1.0, 0.0)
    r2 = lax.broadcasted_iota(jnp.int32, (PAIR, PAIR), 0)
    c2 = lax.broadcasted_iota(jnp.int32, (PAIR, PAIR), 1)
    same_head = (r2 < HEAD_DIM) == (c2 < HEAD_DIM)

    def block_diag(xb):
        zero = jnp.zeros_like(xb)
        return jnp.concatenate([jnp.where(head0, xb, zero), jnp.where(head0, zero, xb)], axis=0)

    n_pairs = d_r // PAIR
    tile = lambda t, c, p: t[c * CHUNK:(c + 1) * CHUNK, p * PAIR:(p + 1) * PAIR]
    inv_n = 1.0 / HEAD_DIM

    for c_lo in range(0, n_chunks, GROUP_CHUNKS):
        group_chunks = range(c_lo, min(c_lo + GROUP_CHUNKS, n_chunks))
        units = [(c, p) for c in group_chunks for p in range(n_pairs)]
        y = _rwkv_group(units, group_chunks, n_pairs, tile, block_diag, state_ref, pc_rows,
                        (at_all, rt_all, bt_all, kt_all, bp_all, kp_all, v_all),
                        (strict, incl, eye2, same_head))
        rows = slice(group_chunks[0] * CHUNK, (group_chunks[-1] + 1) * CHUNK)
        mean = seg_sum(y) * inv_n
        d = y - mean
        var = seg_sum(d * d) * inv_n
        yn = d * lax.rsqrt(var + RWKV_GN_EPS) * lnw_ref[...] + lnb_ref[...]
        bonus = seg_sum(r[rows] * k2[rows] * rk_ref[...]) * v[rows]
        o_ref[0, rows, :] = (yn + bonus) * g[rows]


def _rwkv_group(units, group_chunks, n_pairs, tile, block_diag, state_ref, pc_rows, scaled, masks):
    at_all, rt_all, bt_all, kt_all, bp_all, kp_all, v_all = scaled
    strict, incl, eye2, same_head = masks
    a_ab, a_ak, a_rb, a_rk = {}, {}, {}, {}
    for u in units:
        ar = jnp.concatenate([tile(at_all, *u), tile(rt_all, *u).astype(BF16)], axis=0)
        bk = jnp.concatenate([block_diag(tile(bt_all, *u)), block_diag(tile(kt_all, *u))], axis=0)
        g2 = _dot_nt(ar, bk)
        a_ab[u] = jnp.where(strict, g2[:CHUNK, :PAIR], 0.0)
        a_ak[u] = jnp.where(strict, g2[:CHUNK, PAIR:], 0.0).astype(BF16)
        a_rb[u] = jnp.where(incl, g2[CHUNK:, :PAIR], 0.0).astype(BF16)
        a_rk[u] = jnp.where(incl, g2[CHUNK:, PAIR:], 0.0).astype(BF16)
    tinv = {u: eye2 + a_ab[u] for u in units}
    apow = {}
    for u in units:
        ab = a_ab[u].astype(BF16)
        apow[u] = _dot(ab, block_diag(ab))
    for step in range(1, NEUMANN_STEPS):
        last = step == NEUMANN_STEPS - 1
        for u in units:
            ab = apow[u].astype(BF16)
            lhs = tinv[u].astype(BF16) if last else jnp.concatenate([tinv[u].astype(BF16), ab], axis=0)
            prod = _dot(lhs, block_diag(ab))
            tinv[u] = tinv[u] + prod[:CHUNK]
            if not last:
                apow[u] = prod[CHUNK:]
    av, y1v, w1, u1, rq, y1, mx, gx = {}, {}, {}, {}, {}, {}, {}, {}
    for u in units:
        avk = _dot(jnp.concatenate([a_ak[u], a_rk[u]], axis=0), block_diag(tile(v_all, *u)))
        av[u], y1v[u] = avk[:CHUNK].astype(BF16), avk[CHUNK:]
    for u in units:
        rhs = jnp.concatenate([block_diag(tile(at_all, *u)), block_diag(av[u])], axis=1)
        wu = _dot(tinv[u].astype(BF16), rhs).astype(BF16)
        w1[u], u1[u] = wu[:, :PAIR], wu[:, PAIR:]
    for u in units:
        vc = tile(v_all, *u)
        bpkp = jnp.concatenate([tile(bp_all, *u), tile(kp_all, *u)], axis=0)
        ry = _dot(a_rb[u], jnp.concatenate([block_diag(w1[u]), block_diag(u1[u])], axis=1))
        rq[u] = (tile(rt_all, *u) + ry[:, :PAIR]).astype(BF16)
        y1[u] = ry[:, PAIR:] + y1v[u]
        lhs = jnp.concatenate([jnp.concatenate([w1[u], u1[u]], axis=1),
                               jnp.concatenate([jnp.zeros_like(vc), vc], axis=1)], axis=0)
        mg = _dot_tn(lhs, bpkp)
        mx[u] = jnp.where(same_head, mg[:PAIR], 0.0).astype(BF16)
        gx[u] = jnp.where(same_head, mg[PAIR:], 0.0)

    y_rows = []
    for c in group_chunks:
        y_parts = []
        for p in range(n_pairs):
            u = (c, p)
            s0 = state_ref[p]
            s0b = s0.astype(BF16)
            y_parts.append(_dot_nt(rq[u], s0b) + y1[u])
            pc = pc_rows[c][:, p * PAIR:(p + 1) * PAIR]
            state_ref[p] = s0 * pc + _dot(s0b, mx[u]) + gx[u]
        y_rows.append(jnp.concatenate(y_parts, axis=1))
    return jnp.concatenate(y_rows, axis=0)


def _rwkv(lidx, ps3, mu, w0, w2p, a0, a2p, g2p, k_k, k_a, r_k, ln_w, ln_b, d_r, tb):
    B, S, d_shift = ps3.shape
    hid = jnp.arange(PAIR) // HEAD_DIM
    bd = (hid[:, None] == hid[None, :]).astype(BF16)
    assert tb % CUMSUM_ROWS == 0
    ci = jnp.arange(CUMSUM_ROWS)
    same = (ci[:, None] // CHUNK) == (ci[None, :] // CHUNK)
    tri = (same & (ci[:, None] >= ci[None, :])).astype(BF16)
    const = lambda a: pl.BlockSpec(a.shape, lambda b, t, l: (0,) * a.ndim)
    vec = _layer_spec((1, d_r), buffered=False)
    lora = _layer_spec((LORA_SLAB, d_r), buffered=False)
    grid_spec = pltpu.PrefetchScalarGridSpec(
        num_scalar_prefetch=1, grid=(B, S // tb),
        in_specs=[pl.BlockSpec((1, tb, d_shift), lambda b, t, l: (b, t, 0)),
                  _layer_spec((1, d_shift), buffered=False),
                  vec, lora, vec, lora, lora, vec, vec, vec, vec, vec,
                  const(bd), const(tri)],
        out_specs=pl.BlockSpec((1, tb, d_r), lambda b, t, l: (b, t, 0)),
        scratch_shapes=[pltpu.VMEM((1, d_shift), F32),
                        pltpu.VMEM((d_r // PAIR, PAIR, PAIR), F32)])
    return pl.pallas_call(
        _rwkv_kernel, grid_spec=grid_spec, name="rwkv7",
        out_shape=jax.ShapeDtypeStruct((B, S, d_r), F32),
        compiler_params=pltpu.CompilerParams(dimension_semantics=("arbitrary", "arbitrary"),
                                             vmem_limit_bytes=V7X_VMEM_LIMIT),
    )(lidx, ps3, mu, w0, w2p, a0, a2p, g2p, k_k, k_a, r_k, ln_w, ln_b, bd, tri)


def _attn_kernel(l_ref, exact_ref, q_ref, k_ref, v_ref, qn_ref, kn_ref, bd_ref, bias_ref, o_ref,
                 q_s, k_s, q4_s, k4_s, v4_s, st_s):
    S = q_ref.shape[1]
    bd = bd_ref[...]
    inv_n = 1.0 / HEAD_DIM

    def head_rms(t, gain):
        ms = _dot_bf16(t * t, bd) * inv_n
        return t * lax.rsqrt(ms + NORM_EPS) * gain

    q_s[...] = head_rms(q_ref[0], qn_ref[...]) * (HEAD_DIM ** -0.5)
    k_s[...] = head_rms(k_ref[0], kn_ref[...])
    head0 = lax.broadcasted_iota(jnp.int32, (ATTN_BLOCK, PAIR), 1) < HEAD_DIM

    s_pre = S // PRE_DILATION
    v_tok = v_ref.at[0]

    def regroup(i, carry):
        c = i // (s_pre // REGROUP_ROWS)
        r0 = (i % (s_pre // REGROUP_ROWS)) * REGROUP_ROWS
        src = pl.ds(c + PRE_DILATION * r0, REGROUP_ROWS, stride=PRE_DILATION)
        dst = pl.ds(pl.multiple_of(c * s_pre + r0, REGROUP_ROWS), REGROUP_ROWS)
        q4_s[dst, :] = q_s[src, :]
        k4_s[dst, :] = k_s[src, :]
        v4_s[dst, :] = v_tok[src, :]
        return carry

    lax.fori_loop(0, S // REGROUP_ROWS, regroup, 0)

    exact = exact_ref[l_ref[0]] == 1
    pl.when(exact)(lambda: _attn_patterns(True, S, s_pre, head0, v_tok, bias_ref, o_ref,
                                          q_s, k_s, q4_s, k4_s, v4_s, st_s))
    pl.when(jnp.logical_not(exact))(lambda: _attn_patterns(False, S, s_pre, head0, v_tok, bias_ref, o_ref,
                                                           q_s, k_s, q4_s, k4_s, v4_s, st_s))


def _attn_patterns(row_max, S, s_pre, head0, v_tok, bias_ref, o_ref, q_s, k_s, q4_s, k4_s, v4_s, st_s):
    group = ATTN_GROUP_EXACT if row_max else ATTN_GROUP
    for pi, (window, dil) in enumerate(DILATED_PATTERNS):
        n_sub = S // dil
        n_blk = n_sub // ATTN_BLOCK
        regrouped = dil % PRE_DILATION == 0
        if regrouped:
            stride = dil // PRE_DILATION
            q_src, k_src, v_src = q4_s, k4_s, v4_s
        else:
            stride = dil
            q_src, k_src, v_src = q_s, k_s, v_tok
        state = tuple(st_s.at[3 * pi + j] for j in range(3))
        span = stride * ATTN_BLOCK

        def group_body(gi, carry):
            blocks = []
            for g in range(group):
                i = gi * group + g
                n = i // dil
                res = i % dil
                if regrouped:
                    q_start = (res % PRE_DILATION) * s_pre + res // PRE_DILATION + span * n
                else:
                    q_start = res + span * n
                first = jnp.where(n == 0, 1, 0)
                k_start = q_start - span * (1 - first)
                if stride > 1:
                    q_rows = pl.ds(q_start, ATTN_BLOCK, stride=stride)
                    k_rows = pl.ds(k_start, 2 * ATTN_BLOCK, stride=stride)
                else:
                    q_rows = pl.ds(pl.multiple_of(q_start, ATTN_BLOCK), ATTN_BLOCK)
                    k_rows = pl.ds(pl.multiple_of(k_start, ATTN_BLOCK), 2 * ATTN_BLOCK)
                blocks.append(dict(first=first, q_rows=q_rows, k_rows=k_rows))
            for blk in blocks:
                q2 = q_src[blk["q_rows"], :]
                zero = jnp.zeros_like(q2)
                blk["q"] = jnp.concatenate([jnp.where(head0, q2, zero), jnp.where(head0, zero, q2)],
                                           axis=0).astype(BF16)
                blk["k"] = k_src[blk["k_rows"], :].astype(BF16)
                blk["v"] = v_src[blk["k_rows"], :].astype(BF16)
            for blk in blocks:
                blk["s"] = _dot_nt(blk["q"], blk["k"]) + bias_ref[pi, blk["first"]]
            if row_max:
                for blk in blocks:
                    blk["m"] = jnp.max(blk["s"], axis=-1, keepdims=True)
                for blk in blocks:
                    blk["e"] = jnp.exp(blk["s"] - blk["m"])
            else:
                for blk in blocks:
                    blk["e"] = jnp.exp(blk["s"])
            for blk in blocks:
                blk["l"] = jnp.sum(blk["e"], axis=-1, keepdims=True)
                blk["o"] = _dot(blk["e"].astype(BF16), blk["v"])
            for blk in blocks:
                pair = lambda t: jnp.where(head0, jnp.broadcast_to(t[:ATTN_BLOCK], (ATTN_BLOCK, PAIR)),
                                           jnp.broadcast_to(t[ATTN_BLOCK:], (ATTN_BLOCK, PAIR)))
                blk["out"] = (pair(blk["o"]), pair(blk["m"]) if row_max else None, pair(blk["l"]))
            for blk in blocks:
                for ref, val in zip(state, blk["out"]):
                    if val is not None:
                        ref[blk["q_rows"], :] = val
            return carry

        lax.fori_loop(0, dil * n_blk // group, group_body, 0)

    def finish(i, carry):
        c = i // (s_pre // ATTN_BLOCK)
        r0 = (i % (s_pre // ATTN_BLOCK)) * ATTN_BLOCK
        tok = pl.ds(c + PRE_DILATION * r0, ATTN_BLOCK, stride=PRE_DILATION)
        grp = pl.ds(pl.multiple_of(c * s_pre + r0, ATTN_BLOCK), ATTN_BLOCK)
        total = None
        for pi, (_, dil) in enumerate(DILATED_PATTERNS):
            rows = grp if dil % PRE_DILATION == 0 else tok
            if row_max:
                part = tuple(st_s[3 * pi + j, rows, :] for j in range(3))
                total = part if total is None else _softmax_merge(total, part)
            else:
                part = (st_s[3 * pi, rows, :], None, st_s[3 * pi + 2, rows, :])
                total = part if total is None else (total[0] + part[0], None, total[2] + part[2])
        o_ref[0, tok, :] = total[0] / total[2]
        return carry

    lax.fori_loop(0, S // ATTN_BLOCK, finish, 0)


def _softmax_merge(a, b):
    acc_a, m_a, l_a = a
    acc_b, m_b, l_b = b
    m_new = jnp.maximum(m_a, m_b)
    w_a = jnp.exp(m_a - m_new)
    w_b = jnp.exp(m_b - m_new)
    return acc_a * w_a + acc_b * w_b, m_new, l_a * w_a + l_b * w_b


def _alibi_slopes(n):
    def pow2(m):
        start = 2.0 ** (-8.0 / m)
        return [start ** (i + 1) for i in range(m)]
    if math.log2(n).is_integer():
        return pow2(n)
    c = 2 ** int(math.floor(math.log2(n)))
    return pow2(c) + pow2(2 * c)[0::2][: n - c]


def _attn_bias(n_heads):
    qi = jnp.arange(ATTN_BLOCK)
    ki = jnp.arange(2 * ATTN_BLOCK)
    dist = qi[:, None] + ATTN_BLOCK - ki[None, :]
    slopes = jnp.asarray(_alibi_slopes(n_heads), F32)
    out = []
    for window, dil in DILATED_PATTERNS:
        valid = (dist >= 0) & (dist <= window // dil)
        bias = -slopes[:, None, None] * (dist * dil).astype(F32)[None]
        rest = jnp.where(valid[None], bias, MASK_VALUE)
        first = jnp.concatenate([rest[..., ATTN_BLOCK:], jnp.full_like(rest[..., ATTN_BLOCK:], MASK_VALUE)], -1)
        out.append(jnp.stack([rest, first], 0).reshape(2, n_heads * ATTN_BLOCK, 2 * ATTN_BLOCK))
    return jnp.stack(out, 0)


def _attn_tables(score_bound, n_heads):
    bias = _attn_bias(n_heads)[None] - score_bound[:, None, None, None, None]
    return bias, (score_bound > SAFE_SOFTMAX_SHIFT).astype(jnp.int32)


def _attention(lidx, qkv3, q_norm, k_norm, bias, exact, d_a):
    B, S, _ = qkv3.shape
    n_heads = d_a // HEAD_DIM
    n_pairs = n_heads // 2
    slab = 2 * HEAD_DIM
    hid = jnp.arange(slab) // HEAD_DIM
    bd = (hid[:, None] == hid[None, :]).astype(BF16)
    col = lambda off: pl.BlockSpec((1, S, slab), lambda b, p, l, e: (b, 0, off + p))
    gain = pl.BlockSpec((None, 1, slab), lambda b, p, l, e: (l[0], 0, 0))
    grid_spec = pltpu.PrefetchScalarGridSpec(
        num_scalar_prefetch=2, grid=(B, n_pairs),
        in_specs=[col(0), col(n_pairs), col(2 * n_pairs), gain, gain,
                  pl.BlockSpec(bd.shape, lambda b, p, l, e: (0, 0)),
                  pl.BlockSpec((None, len(DILATED_PATTERNS), 2, 2 * ATTN_BLOCK, 2 * ATTN_BLOCK),
                               lambda b, p, l, e: (l[0], 0, 0, p, 0))],
        out_specs=pl.BlockSpec((1, S, slab), lambda b, p, l, e: (b, 0, p)),
        scratch_shapes=[pltpu.VMEM((S, slab), F32)] * 5
        + [pltpu.VMEM((3 * len(DILATED_PATTERNS), S, slab), F32)])
    return pl.pallas_call(
        _attn_kernel, grid_spec=grid_spec, name="dilated_attn",
        out_shape=jax.ShapeDtypeStruct((B, S, d_a), F32),
        compiler_params=pltpu.CompilerParams(dimension_semantics=("arbitrary", "arbitrary"),
                                             vmem_limit_bytes=V7X_VMEM_LIMIT),
    )(lidx, exact, qkv3, qkv3, qkv3, q_norm, k_norm, bd, bias)


def _pad_lora(w, offset):
    L, r, d = w.shape
    return jnp.zeros((L, LORA_SLAB, d), F32).at[:, offset:offset + r, :].set(w)


def kernel(x, norm_ffn1, ffn1_w_gate, ffn1_w_up, ffn1_w_down, norm_mix, w_in, shift_mu, rwkv_w0, rwkv_w2, rwkv_a0, rwkv_a2, rwkv_g2, rwkv_k_k, rwkv_k_a, rwkv_r_k, rwkv_ln_w, rwkv_ln_b, attn_q_norm, attn_k_norm, conv_dw_w, conv_dw_b, conv_ln_w, conv_ln_b, w_out, norm_ffn2, ffn2_w_gate, ffn2_w_up, ffn2_w_down):
    B, S, D = x.shape
    depth = w_in.shape[0]
    d_r = rwkv_w0.shape[-1]
    d_c = conv_dw_b.shape[-1]
    d_shift = shift_mu.shape[-1]
    d_a = (w_in.shape[-1] - d_shift - 2 * d_c) // 3
    r_w, r_a, r_g = rwkv_w2.shape[1], rwkv_a2.shape[1], rwkv_g2.shape[1]
    assert d_shift == 3 * d_r + LORA_SLAB and r_w + r_a + r_g == LORA_SLAB
    assert S % (2 * ATTN_BLOCK * DILATED_PATTERNS[-1][1]) == 0 and (d_a // HEAD_DIM) % 2 == 0
    assert (S // ATTN_BLOCK) % max(ATTN_GROUP, ATTN_GROUP_EXACT) == 0
    assert S % (PRE_DILATION * REGROUP_ROWS) == 0
    assert all(d < PRE_DILATION or d % PRE_DILATION == 0 for _, d in DILATED_PATTERNS)
    T = B * S
    assert T % FFN_ROWS == 0 and S % RWKV_ROWS == 0

    vec = lambda a: a.reshape(depth, 1, -1)
    bf = lambda a: a.astype(BF16)
    wg1, wu1, wd1 = bf(ffn1_w_gate), bf(ffn1_w_up), bf(ffn1_w_down)
    wg2, wu2, wd2 = bf(ffn2_w_gate), bf(ffn2_w_up), bf(ffn2_w_down)
    win, wo = bf(w_in), bf(w_out)
    g1, gm, g2 = vec(norm_ffn1), vec(norm_mix), vec(norm_ffn2)
    w2p = _pad_lora(rwkv_w2, 0)
    a2p = _pad_lora(rwkv_a2, r_w)
    g2p = _pad_lora(rwkv_g2, r_w + r_a)
    qn = vec(jnp.tile(attn_q_norm, (1, 2)))
    kn = vec(jnp.tile(attn_k_norm, (1, 2)))
    score_bound = (HEAD_DIM ** 0.5) * jnp.max(jnp.abs(attn_q_norm), -1) * jnp.max(jnp.abs(attn_k_norm), -1)
    attn_bias, attn_exact = _attn_tables(score_bound, d_a // HEAD_DIM)

    layer_index = lambda l: jnp.reshape(l, (1,)).astype(jnp.int32)

    def project(l, xf):
        return _ffn_proj(layer_index(l), xf, g1, wg1, wu1, wd1, gm, win, (d_shift, 3 * d_a, 2 * d_c),
                         FFN_ROWS)

    def mix(l, projected):
        lidx = layer_index(l)
        x1, ps, qkv, u = projected
        y_r = _rwkv(lidx, ps.reshape(B, S, d_shift), vec(shift_mu), vec(rwkv_w0), w2p, vec(rwkv_a0),
                    a2p, g2p, vec(rwkv_k_k), vec(rwkv_k_a), vec(rwkv_r_k), vec(rwkv_ln_w),
                    vec(rwkv_ln_b), d_r, RWKV_ROWS)
        y_a = _attention(lidx, qkv.reshape(B, S, 3 * d_a), qn, kn, attn_bias, attn_exact, d_a)
        x2 = _mix_out(lidx, x1, y_r.reshape(T, d_r), y_a.reshape(T, d_a), u, conv_dw_w,
                      vec(conv_dw_b), vec(conv_ln_w), vec(conv_ln_b), wo, MIX_ROWS, S)
        return _ffn(lidx, x2, g2, wg2, wu2, wd2, 2 * FFN_ROWS)

    state = project(jnp.int32(0), x.reshape(T, D))
    state = lax.fori_loop(0, depth - 1, lambda l, st: tuple(project(l + 1, mix(l, st))), tuple(state))
    return mix(jnp.int32(depth - 1), state).reshape(B, S, D)
```

```python
import functools
import math

import jax
import jax.numpy as jnp
from jax import lax
from jax.experimental import pallas as pl
from jax.experimental.pallas import tpu as pltpu

F32 = jnp.float32
BF16 = jnp.bfloat16

LANES = 128
SUBLANES = 8
FFN_ROWS = 512
MIX_ROWS = 1024
RWKV_ROWS = 1024
HEAD_DIM = 64
PAIR = 2 * HEAD_DIM
NORM_EPS = 1e-6
RWKV_GN_EPS = 64e-5
CONV_LN_EPS = 1e-5
KK_EPS = 1e-12
CONV_WIDTH = 31
DILATED_PATTERNS = ((128, 1), (512, 4), (2048, 16))
ATTN_BLOCK = 128
ATTN_GROUP = 16
ATTN_GROUP_EXACT = 4
PRE_DILATION = 4
REGROUP_ROWS = 256
LORA_SLAB = 128
CHUNK = 64
GROUP_CHUNKS = 8
CUMSUM_ROWS = 256
NEUMANN_STEPS = 6
MASK_VALUE = -1e30
SAFE_SOFTMAX_SHIFT = 40.0
EXP_NEG_HALF = math.exp(-0.5)
V7X_VMEM_LIMIT = 56 * 1024 * 1024

NT_DIMS = (((1,), (1,)), ((), ()))
TN_DIMS = (((0,), (0,)), ((), ()))


def _dot(a, b):
    return jnp.dot(a, b, preferred_element_type=F32)


def _dot_nt(a, b):
    return lax.dot_general(a, b, NT_DIMS, preferred_element_type=F32)


def _dot_tn(a, b):
    return lax.dot_general(a, b, TN_DIMS, preferred_element_type=F32)


def _split2(x):
    hi = x.astype(BF16)
    lo = (x - hi.astype(F32)).astype(BF16)
    return hi, lo


def _dot_bf16(x, w):
    return _dot(x.astype(BF16), w.astype(BF16))


def _dot_f32_rhs(w_exact, x):
    hi, lo = _split2(x)
    return _dot(w_exact, hi) + _dot(w_exact, lo)


def _dot_f32(x, w):
    xh, xl = _split2(x)
    wh, wl = _split2(w)
    return _dot(xh, wh) + _dot(xl, wh) + _dot(xh, wl)


def _sigmoid(x):
    return 0.5 + 0.5 * jnp.tanh(0.5 * x)


def _rms_norm(x, g):
    return x * lax.rsqrt(jnp.mean(x * x, axis=-1, keepdims=True) + NORM_EPS) * g


def _swiglu_residual(x, g, wg_ref, wu_ref, wd_ref):
    xn = _rms_norm(x, g).astype(BF16)
    gate = _dot(xn, wg_ref[...])
    up = _dot(xn, wu_ref[...])
    h =(gate * _sigmoid(gate) * up).astype(BF16)
    return x + 0.5 * _dot(h, wd_ref[...])


def _ffn_proj_kernel(l_ref, x_ref, g1_ref, wg_ref, wu_ref, wd_ref, gm_ref, win_ref,
                     x1_ref, ps_ref, qkv_ref, u_ref):
    del l_ref
    x1 = _swiglu_residual(x_ref[...], g1_ref[...], wg_ref, wu_ref, wd_ref)
    x1_ref[...] = x1
    h = _rms_norm(x1, gm_ref[...]).astype(BF16)
    proj = _dot(h, win_ref[...])
    d_shift = ps_ref.shape[-1]
    d_qkv = qkv_ref.shape[-1]
    ps_ref[...] = proj[:, :d_shift]
    qkv_ref[...] = proj[:, d_shift:d_shift + d_qkv]
    u_ref[...] = proj[:, d_shift + d_qkv:]


def _layer_spec(shape, buffered=True):
    nd = len(shape)
    kw = dict(pipeline_mode=pl.Buffered(1)) if buffered else {}
    return pl.BlockSpec((None,) + tuple(shape), lambda *a: (a[-1][0],) + (0,) * nd, **kw)


def _ffn_proj(lidx, x, g1, wg, wu, wd, gm, win, dims, tm):
    T, D = x.shape
    d_shift, d_qkv, d_u = dims
    row = lambda w: pl.BlockSpec((tm, w), lambda i, l: (i, 0))
    grid_spec = pltpu.PrefetchScalarGridSpec(
        num_scalar_prefetch=1, grid=(T // tm,),
        in_specs=[row(D), _layer_spec((1, D)), _layer_spec(wg.shape[1:]), _layer_spec(wu.shape[1:]),
                  _layer_spec(wd.shape[1:]), _layer_spec((1, D)), _layer_spec(win.shape[1:])],
        out_specs=[row(D), row(d_shift), row(d_qkv), row(d_u)])
    return pl.pallas_call(
        _ffn_proj_kernel, grid_spec=grid_spec, name="ffn_proj",
        out_shape=[jax.ShapeDtypeStruct((T, D), F32), jax.ShapeDtypeStruct((T, d_shift), F32),
                   jax.ShapeDtypeStruct((T, d_qkv), F32), jax.ShapeDtypeStruct((T, d_u), F32)],
        compiler_params=pltpu.CompilerParams(dimension_semantics=("arbitrary",),
                                             vmem_limit_bytes=V7X_VMEM_LIMIT),
    )(lidx, x, g1, wg, wu, wd, gm, win)


CONV_PAD = 32
CONV_ROWS = 64


def _conv_rows(z_ref, r0, w_ref, b_ref, lnw_ref, lnb_ref):
    d_c = z_ref.shape[-1]
    shift = CONV_PAD - (CONV_WIDTH - 1)
    n_win = CONV_ROWS + CONV_PAD
    win = z_ref[r0:r0 + n_win, :]
    acc = jnp.zeros((CONV_ROWS, d_c), F32) + b_ref[...]
    for sub in range(SUBLANES):
        rolled = win if sub == 0 else pltpu.roll(win, n_win - sub, 0)
        for j in range(CONV_WIDTH):
            off = shift + j
            if off % SUBLANES == sub:
                base = off - sub
                acc = acc + rolled[base:base + CONV_ROWS, :] * w_ref[j:j + 1, :]
    mean = jnp.mean(acc, axis=-1, keepdims=True)
    d = acc - mean
    var = jnp.mean(d * d, axis=-1, keepdims=True)
    z = d * lax.rsqrt(var + CONV_LN_EPS) * lnw_ref[...] + lnb_ref[...]
    return z * _sigmoid(z)


def _mix_out_kernel(tiles_per_seq, l_ref, x_ref, yr_ref, ya_ref, u_ref, cw_ref, cb_ref, clnw_ref,
                    clnb_ref, wo_ref, o_ref, z_s):
    del l_ref
    i = pl.program_id(0)
    tm = x_ref.shape[0]
    d_c = z_s.shape[-1]

    @pl.when(i == 0)
    def _():
        z_s[...] = jnp.zeros_like(z_s)

    left = z_s[tm:tm + CONV_PAD, :]
    z_s[0:CONV_PAD, :] = jnp.where(i % tiles_per_seq == 0, jnp.zeros_like(left), left)
    z_s[CONV_PAD:, :] = u_ref[:, 0:d_c] * _sigmoid(u_ref[:, d_c:])
    y_c = jnp.concatenate([_conv_rows(z_s, r0, cw_ref, cb_ref, clnw_ref, clnb_ref)
                           for r0 in range(0, tm, CONV_ROWS)], axis=0)
    mix = jnp.concatenate([yr_ref[...].astype(BF16), ya_ref[...].astype(BF16), y_c.astype(BF16)], axis=1)
    o_ref[...] = x_ref[...] + _dot(mix, wo_ref[...])


def _mix_out(lidx, x, yr, ya, u, cw, cb, clnw, clnb, wo, tm, seq_len):
    T, D = x.shape
    d_c = u.shape[1] // 2
    assert seq_len % tm == 0 and tm % CONV_ROWS == 0
    row = lambda w: pl.BlockSpec((tm, w), lambda i, l: (i, 0))
    grid_spec = pltpu.PrefetchScalarGridSpec(
        num_scalar_prefetch=1, grid=(T // tm,),
        in_specs=[row(D), row(yr.shape[1]), row(ya.shape[1]), row(2 * d_c),
                  _layer_spec((CONV_WIDTH, d_c), buffered=False), _layer_spec((1, d_c), buffered=False),
                  _layer_spec((1, d_c), buffered=False), _layer_spec((1, d_c), buffered=False),
                  _layer_spec(wo.shape[1:])],
        out_specs=row(D),
        scratch_shapes=[pltpu.VMEM((CONV_PAD + tm, d_c), F32)])
    return pl.pallas_call(
        functools.partial(_mix_out_kernel, seq_len // tm), grid_spec=grid_spec, name="mix_out",
        out_shape=jax.ShapeDtypeStruct((T, D), F32),
        compiler_params=pltpu.CompilerParams(dimension_semantics=("arbitrary",),
                                             vmem_limit_bytes=V7X_VMEM_LIMIT),
    )(lidx, x, yr, ya, u, cw, cb, clnw, clnb, wo)


def _ffn_kernel(l_ref, x_ref, g_ref, wg_ref, wu_ref, wd_ref, o_ref):
    del l_ref
    o_ref[...] = _swiglu_residual(x_ref[...], g_ref[...], wg_ref, wu_ref, wd_ref)


def _ffn(lidx, x, g, wg, wu, wd, tm):
    T, D = x.shape
    row = pl.BlockSpec((tm, D), lambda i, l: (i, 0))
    grid_spec = pltpu.PrefetchScalarGridSpec(
        num_scalar_prefetch=1, grid=(T // tm,),
        in_specs=[row, _layer_spec((1, D)), _layer_spec(wg.shape[1:]), _layer_spec(wu.shape[1:]),
                  _layer_spec(wd.shape[1:])],
        out_specs=row)
    return pl.pallas_call(
        _ffn_kernel, grid_spec=grid_spec, name="ffn2",
        out_shape=jax.ShapeDtypeStruct((T, D), F32),
        compiler_params=pltpu.CompilerParams(dimension_semantics=("arbitrary",),
                                             vmem_limit_bytes=V7X_VMEM_LIMIT),
    )(lidx, x, g, wg, wu, wd)


def _rwkv_kernel(l_ref, p_ref, mu_ref, w0_ref, w2_ref, a0_ref, a2_ref, g2_ref, kk_ref, ka_ref,
                 rk_ref, lnw_ref, lnb_ref, bd_ref, tri_ref, o_ref,
                 carry_ref, state_ref):
    del l_ref
    tb = p_ref.shape[1]
    d_r = o_ref.shape[-1]

    @pl.when(pl.program_id(1) == 0)
    def _():
        carry_ref[...] = jnp.zeros_like(carry_ref)
        state_ref[...] = jnp.zeros_like(state_ref)

    p = p_ref[0]
    row = lax.broadcasted_iota(jnp.int32, (tb, 1), 0)
    prev = jnp.where(row == 0, carry_ref[...], pltpu.roll(p, 1, 0))
    carry_ref[...] = p[tb - 1:tb, :]
    ps = p + (prev - p) * mu_ref[...]

    r = ps[:, 0:d_r]
    k = ps[:, d_r:2 * d_r]
    v = ps[:, 2 * d_r:3 * d_r]
    xs = ps[:, 3 * d_r:3 * d_r + LORA_SLAB]

    bd = bd_ref[...]
    def seg_sum(t):
        tb16 = t.astype(BF16)
        return jnp.concatenate([_dot(tb16[:, s:s + PAIR], bd) for s in range(0, d_r, PAIR)], axis=1)

    lw = w0_ref[...] + _dot_f32(jnp.tanh(xs), w2_ref[...])
    ld = (-0.5 * EXP_NEG_HALF) * jnp.tanh(0.5 * lw) - 0.5 * EXP_NEG_HALF
    a = _sigmoid(a0_ref[...] + _dot_bf16(xs, a2_ref[...]))
    g = _dot_bf16(_sigmoid(xs), g2_ref[...])
    kk = k * kk_ref[...]
    kk = kk * lax.rsqrt(seg_sum(kk * kk) + KK_EPS)
    ka = ka_ref[...]
    k2 = k * ((1.0 - ka) + a * ka)
    b = kk * a

    lcum = jnp.concatenate(
        [_dot_f32_rhs(tri_ref[...], ld[s:s + CUMSUM_ROWS]) for s in range(0, tb, CUMSUM_ROWS)], axis=0)
    n_chunks = tb // CHUNK
    pc_rows = [jnp.exp(lcum[(c + 1) * CHUNK - 1:(c + 1) * CHUNK, :]) for c in range(n_chunks)]
    at_all = (-kk * jnp.exp(lcum - ld)).astype(BF16)
    rt_all = r * jnp.exp(lcum)
    inv_p = jnp.exp(-lcum)
    bt_all = (b * inv_p).astype(BF16)
    kt_all = (k2 * inv_p).astype(BF16)
    rest = inv_p * jnp.concatenate([jnp.broadcast_to(pc, (CHUNK, d_r)) for pc in pc_rows], axis=0)
    bp_all = (b * rest).astype(BF16)
    kp_all = (k2 * rest).astype(BF16)
    v_all = v.astype(BF16)

    lane = lax.broadcasted_iota(jnp.int32, (CHUNK, PAIR), 1)
    trow = lax.broadcasted_iota(jnp.int32, (CHUNK, PAIR), 0)
    head0 = lane < HEAD_DIM
    scol = lane & (HEAD_DIM - 1)
    strict = trow > scol
    incl = trow >= scol
    eye2 = jnp.where(trow == scol, 1.0, 0.0)
    r2 = lax.broadcasted_iota(jnp.int32, (PAIR, PAIR), 0)
    c2 = lax.broadcasted_iota(jnp.int32, (PAIR, PAIR), 1)
    same_head = (r2 < HEAD_DIM) == (c2 < HEAD_DIM)

    def block_diag(xb):
        zero = jnp.zeros_like(xb)
        return jnp.concatenate([jnp.where(head0, xb, zero), jnp.where(head0, zero, xb)], axis=0)

    n_pairs = d_r // PAIR
    tile = lambda t, c, p: t[c * CHUNK:(c + 1) * CHUNK, p * PAIR:(p + 1) * PAIR]
    inv_n = 1.0 / HEAD_DIM

    for c_lo in range(0, n_chunks, GROUP_CHUNKS):
        group_chunks = range(c_lo, min(c_lo + GROUP_CHUNKS, n_chunks))
        units = [(c, p) for c in group_chunks for p in range(n_pairs)]
        y = _rwkv_group(units, group_chunks, n_pairs, tile, block_diag, state_ref, pc_rows,
                        (at_all, rt_all, bt_all, kt_all, bp_all, kp_all, v_all),
                        (strict, incl, eye2, same_head))
        rows = slice(group_chunks[0] * CHUNK, (group_chunks[-1] + 1) * CHUNK)
        mean = seg_sum(y) * inv_n
        d = y - mean
        var = seg_sum(d * d) * inv_n
        yn = d * lax.rsqrt(var + RWKV_GN_EPS) * lnw_ref[...] + lnb_ref[...]
        bonus = seg_sum(r[rows] * k2[rows] * rk_ref[...]) * v[rows]
        o_ref[0, rows, :] = (yn + bonus) * g[rows]


def _rwkv_group(units, group_chunks, n_pairs, tile, block_diag, state_ref, pc_rows, scaled, masks):
    at_all, rt_all, bt_all, kt_all, bp_all, kp_all, v_all = scaled
    strict, incl, eye2, same_head = masks
    a_ab, a_ak, a_rb, a_rk = {}, {}, {}, {}
    for u in units:
        ar = jnp.concatenate([tile(at_all, *u), tile(rt_all, *u).astype(BF16)], axis=0)
        bk = jnp.concatenate([block_diag(tile(bt_all, *u)), block_diag(tile(kt_all, *u))], axis=0)
        g2 = _dot_nt(ar, bk)
        a_ab[u] = jnp.where(strict, g2[:CHUNK, :PAIR], 0.0)
        a_ak[u] = jnp.where(strict, g2[:CHUNK, PAIR:], 0.0).astype(BF16)
        a_rb[u] = jnp.where(incl, g2[CHUNK:, :PAIR], 0.0).astype(BF16)
        a_rk[u] = jnp.where(incl, g2[CHUNK:, PAIR:], 0.0).astype(BF16)
    tinv = {u: eye2 + a_ab[u] for u in units}
    apow = {}
    for u in units:
        ab = a_ab[u].astype(BF16)
        apow[u] = _dot(ab, block_diag(ab))
    for step in range(1, NEUMANN_STEPS):
        last = step == NEUMANN_STEPS - 1
        for u in units:
            ab = apow[u].astype(BF16)
            lhs = tinv[u].astype(BF16) if last else jnp.concatenate([tinv[u].astype(BF16), ab], axis=0)
            prod = _dot(lhs, block_diag(ab))
            tinv[u] = tinv[u] + prod[:CHUNK]
            if not last:
                apow[u] = prod[CHUNK:]
    av, y1v, w1, u1, rq, y1, mx, gx = {}, {}, {}, {}, {}, {}, {}, {}
    for u in units:
        avk = _dot(jnp.concatenate([a_ak[u], a_rk[u]], axis=0), block_diag(tile(v_all, *u)))
        av[u], y1v[u] = avk[:CHUNK].astype(BF16), avk[CHUNK:]
    for u in units:
        rhs = jnp.concatenate([block_diag(tile(at_all, *u)), block_diag(av[u])], axis=1)
        wu = _dot(tinv[u].astype(BF16), rhs).astype(BF16)
        w1[u], u1[u] = wu[:, :PAIR], wu[:, PAIR:]
    for u in units:
        vc = tile(v_all, *u)
        bpkp = jnp.concatenate([tile(bp_all, *u), tile(kp_all, *u)], axis=0)
        ry = _dot(a_rb[u], jnp.concatenate([block_diag(w1[u]), block_diag(u1[u])], axis=1))
        rq[u] = (tile(rt_all, *u) + ry[:, :PAIR]).astype(BF16)
        y1[u] = ry[:, PAIR:] + y1v[u]
        lhs = jnp.concatenate([jnp.concatenate([w1[u], u1[u]], axis=1),
                               jnp.concatenate([jnp.zeros_like(vc), vc], axis=1)], axis=0)
        mg = _dot_tn(lhs, bpkp)
        mx[u] = jnp.where(same_head, mg[:PAIR], 0.0).astype(BF16)
        gx[u] = jnp.where(same_head, mg[PAIR:], 0.0)

    y_rows = []
    for c in group_chunks:
        y_parts = []
        for p in range(n_pairs):
            u = (c, p)
            s0 = state_ref[p]
            s0b = s0.astype(BF16)
            y_parts.append(_dot_nt(rq[u], s0b) + y1[u])
            pc = pc_rows[c][:, p * PAIR:(p + 1) * PAIR]
            state_ref[p] = s0 * pc + _dot(s0b, mx[u]) + gx[u]
        y_rows.append(jnp.concatenate(y_parts, axis=1))
    return jnp.concatenate(y_rows, axis=0)


def _rwkv(lidx, ps3, mu, w0, w2p, a0, a2p, g2p, k_k, k_a, r_k, ln_w, ln_b, d_r, tb):
    B, S, d_shift = ps3.shape
    hid = jnp.arange(PAIR) // HEAD_DIM
    bd = (hid[:, None] == hid[None, :]).astype(BF16)
    assert tb % CUMSUM_ROWS == 0
    ci = jnp.arange(CUMSUM_ROWS)
    same = (ci[:, None] // CHUNK) == (ci[None, :] // CHUNK)
    tri = (same & (ci[:, None] >= ci[None, :])).astype(BF16)
    const = lambda a: pl.BlockSpec(a.shape, lambda b, t, l: (0,) * a.ndim)
    vec = _layer_spec((1, d_r), buffered=False)
    lora = _layer_spec((LORA_SLAB, d_r), buffered=False)
    grid_spec = pltpu.PrefetchScalarGridSpec(
        num_scalar_prefetch=1, grid=(B, S // tb),
        in_specs=[pl.BlockSpec((1, tb, d_shift), lambda b, t, l: (b, t, 0)),
                  _layer_spec((1, d_shift), buffered=False),
                  vec, lora, vec, lora, lora, vec, vec, vec, vec, vec,
                  const(bd), const(tri)],
        out_specs=pl.BlockSpec((1, tb, d_r), lambda b, t, l: (b, t, 0)),
        scratch_shapes=[pltpu.VMEM((1, d_shift), F32),
                        pltpu.VMEM((d_r // PAIR, PAIR, PAIR), F32)])
    return pl.pallas_call(
        _rwkv_kernel, grid_spec=grid_spec, name="rwkv7",
        out_shape=jax.ShapeDtypeStruct((B, S, d_r), F32),
        compiler_params=pltpu.CompilerParams(dimension_semantics=("arbitrary", "arbitrary"),
                                             vmem_limit_bytes=V7X_VMEM_LIMIT),
    )(lidx, ps3, mu, w0, w2p, a0, a2p, g2p, k_k, k_a, r_k, ln_w, ln_b, bd, tri)


def _attn_kernel(l_ref, exact_ref, q_ref, k_ref, v_ref, qn_ref, kn_ref, bd_ref, bias_ref, o_ref,
                 q_s, k_s, q4_s, k4_s, v4_s, st_s):
    S = q_ref.shape[1]
    bd = bd_ref[...]
    inv_n = 1.0 / HEAD_DIM

    def head_rms(t, gain):
        ms = _dot_bf16(t * t, bd) * inv_n
        return t * lax.rsqrt(ms + NORM_EPS) * gain

    q_s[...] = head_rms(q_ref[0], qn_ref[...]) * (HEAD_DIM ** -0.5)
    k_s[...] = head_rms(k_ref[0], kn_ref[...])
    head0 = lax.broadcasted_iota(jnp.int32, (ATTN_BLOCK, PAIR), 1) < HEAD_DIM

    s_pre = S // PRE_DILATION
    v_tok = v_ref.at[0]

    def regroup(i, carry):
        c = i // (s_pre // REGROUP_ROWS)
        r0 = (i % (s_pre // REGROUP_ROWS)) * REGROUP_ROWS
        src = pl.ds(c + PRE_DILATION * r0, REGROUP_ROWS, stride=PRE_DILATION)
        dst = pl.ds(pl.multiple_of(c * s_pre + r0, REGROUP_ROWS), REGROUP_ROWS)
        q4_s[dst, :] = q_s[src, :]
        k4_s[dst, :] = k_s[src, :]
        v4_s[dst, :] = v_tok[src, :]
        return carry

    lax.fori_loop(0, S // REGROUP_ROWS, regroup, 0)

    exact = exact_ref[l_ref[0]] == 1
    pl.when(exact)(lambda: _attn_patterns(True, S, s_pre, head0, v_tok, bias_ref, o_ref,
                                          q_s, k_s, q4_s, k4_s, v4_s, st_s))
    pl.when(jnp.logical_not(exact))(lambda: _attn_patterns(False, S, s_pre, head0, v_tok, bias_ref, o_ref,
                                                           q_s, k_s, q4_s, k4_s, v4_s, st_s))


def _attn_patterns(row_max, S, s_pre, head0, v_tok, bias_ref, o_ref, q_s, k_s, q4_s, k4_s, v4_s, st_s):
    group = ATTN_GROUP_EXACT if row_max else ATTN_GROUP
    for pi, (window, dil) in enumerate(DILATED_PATTERNS):
        n_sub = S // dil
        n_blk = n_sub // ATTN_BLOCK
        regrouped = dil % PRE_DILATION == 0
        if regrouped:
            stride = dil // PRE_DILATION
            q_src, k_src, v_src = q4_s, k4_s, v4_s
        else:
            stride = dil
            q_src, k_src, v_src = q_s, k_s, v_tok
        state = tuple(st_s.at[3 * pi + j] for j in range(3))
        span = stride * ATTN_BLOCK

        def group_body(gi, carry):
            blocks = []
            for g in range(group):
                i = gi * group + g
                n = i // dil
                res = i % dil
                if regrouped:
                    q_start = (res % PRE_DILATION) * s_pre + res // PRE_DILATION + span * n
                else:
                    q_start = res + span * n
                first = jnp.where(n == 0, 1, 0)
                k_start = q_start - span * (1 - first)
                if stride > 1:
                    q_rows = pl.ds(q_start, ATTN_BLOCK, stride=stride)
                    k_rows = pl.ds(k_start, 2 * ATTN_BLOCK, stride=stride)
                else:
                    q_rows = pl.ds(pl.multiple_of(q_start, ATTN_BLOCK), ATTN_BLOCK)
                    k_rows = pl.ds(pl.multiple_of(k_start, ATTN_BLOCK), 2 * ATTN_BLOCK)
                blocks.append(dict(first=first, q_rows=q_rows, k_rows=k_rows))
            for blk in blocks:
                q2 = q_src[blk["q_rows"], :]
                zero = jnp.zeros_like(q2)
                blk["q"] = jnp.concatenate([jnp.where(head0, q2, zero), jnp.where(head0, zero, q2)],
                                           axis=0).astype(BF16)
                blk["k"] = k_src[blk["k_rows"], :].astype(BF16)
                blk["v"] = v_src[blk["k_rows"], :].astype(BF16)
            for blk in blocks:
                blk["s"] = _dot_nt(blk["q"], blk["k"]) + bias_ref[pi, blk["first"]]
            if row_max:
                for blk in blocks:
                    blk["m"] = jnp.max(blk["s"], axis=-1, keepdims=True)
                for blk in blocks:
                    blk["e"] = jnp.exp(blk["s"] - blk["m"])
            else:
                for blk in blocks:
                    blk["e"] = jnp.exp(blk["s"])
            for blk in blocks:
                blk["l"] = jnp.sum(blk["e"], axis=-1, keepdims=True)
                blk["o"] = _dot(blk["e"].astype(BF16), blk["v"])
            for blk in blocks:
                pair = lambda t: jnp.where(head0, jnp.broadcast_to(t[:ATTN_BLOCK], (ATTN_BLOCK, PAIR)),
                                           jnp.broadcast_to(t[ATTN_BLOCK:], (ATTN_BLOCK, PAIR)))
                blk["out"] = (pair(blk["o"]), pair(blk["m"]) if row_max else None, pair(blk["l"]))
            for blk in blocks:
                for ref, val in zip(state, blk["out"]):
                    if val is not None:
                        ref[blk["q_rows"], :] = val
            return carry

        lax.fori_loop(0, dil * n_blk // group, group_body, 0)

    def finish(i, carry):
        c = i // (s_pre // ATTN_BLOCK)
        r0 = (i % (s_pre // ATTN_BLOCK)) * ATTN_BLOCK
        tok = pl.ds(c + PRE_DILATION * r0, ATTN_BLOCK, stride=PRE_DILATION)
        grp = pl.ds(pl.multiple_of(c * s_pre + r0, ATTN_BLOCK), ATTN_BLOCK)
        total = None
        for pi, (_, dil) in enumerate(DILATED_PATTERNS):
            rows = grp if dil % PRE_DILATION == 0 else tok
            if row_max:
                part = tuple(st_s[3 * pi + j, rows, :] for j in range(3))
                total = part if total is None else _softmax_merge(total, part)
            else:
                part = (st_s[3 * pi, rows, :], None, st_s[3 * pi + 2, rows, :])
                total = part if total is None else (total[0] + part[0], None, total[2] + part[2])
        o_ref[0, tok, :] = total[0] / total[2]
        return carry

    lax.fori_loop(0, S // ATTN_BLOCK, finish, 0)


def _softmax_merge(a, b):
    acc_a, m_a, l_a = a
    acc_b, m_b, l_b = b
    m_new = jnp.maximum(m_a, m_b)
    w_a = jnp.exp(m_a - m_new)
    w_b = jnp.exp(m_b - m_new)
    return acc_a * w_a + acc_b * w_b, m_new, l_a * w_a + l_b * w_b


def _alibi_slopes(n):
    def pow2(m):
        start = 2.0 ** (-8.0 / m)
        return [start ** (i + 1) for i in range(m)]
    if math.log2(n).is_integer():
        return pow2(n)
    c = 2 ** int(math.floor(math.log2(n)))
    return pow2(c) + pow2(2 * c)[0::2][: n - c]


def _attn_bias(n_heads):
    qi = jnp.arange(ATTN_BLOCK)
    ki = jnp.arange(2 * ATTN_BLOCK)
    dist = qi[:, None] + ATTN_BLOCK - ki[None, :]
    slopes = jnp.asarray(_alibi_slopes(n_heads), F32)
    out = []
    for window, dil in DILATED_PATTERNS:
        valid = (dist >= 0) & (dist <= window // dil)
        bias = -slopes[:, None, None] * (dist * dil).astype(F32)[None]
        rest = jnp.where(valid[None], bias, MASK_VALUE)
        first = jnp.concatenate([rest[..., ATTN_BLOCK:], jnp.full_like(rest[..., ATTN_BLOCK:], MASK_VALUE)], -1)
        out.append(jnp.stack([rest, first], 0).reshape(2, n_heads * ATTN_BLOCK, 2 * ATTN_BLOCK))
    return jnp.stack(out, 0)


def _attn_tables(score_bound, n_heads):
    bias = _attn_bias(n_heads)[None] - score_bound[:, None, None, None, None]
    return bias, (score_bound > SAFE_SOFTMAX_SHIFT).astype(jnp.int32)


def _attention(lidx, qkv3, q_norm, k_norm, bias, exact, d_a):
    B, S, _ = qkv3.shape
    n_heads = d_a // HEAD_DIM
    n_pairs = n_heads // 2
    slab = 2 * HEAD_DIM
    hid = jnp.arange(slab) // HEAD_DIM
    bd = (hid[:, None] == hid[None, :]).astype(BF16)
    col = lambda off: pl.BlockSpec((1, S, slab), lambda b, p, l, e: (b, 0, off + p))
    gain = pl.BlockSpec((None, 1, slab), lambda b, p, l, e: (l[0], 0, 0))
    grid_spec = pltpu.PrefetchScalarGridSpec(
        num_scalar_prefetch=2, grid=(B, n_pairs),
        in_specs=[col(0), col(n_pairs), col(2 * n_pairs), gain, gain,
                  pl.BlockSpec(bd.shape, lambda b, p, l, e: (0, 0)),
                  pl.BlockSpec((None, len(DILATED_PATTERNS), 2, 2 * ATTN_BLOCK, 2 * ATTN_BLOCK),
                               lambda b, p, l, e: (l[0], 0, 0, p, 0))],
        out_specs=pl.BlockSpec((1, S, slab), lambda b, p, l, e: (b, 0, p)),
        scratch_shapes=[pltpu.VMEM((S, slab), F32)] * 5
        + [pltpu.VMEM((3 * len(DILATED_PATTERNS), S, slab), F32)])
    return pl.pallas_call(
        _attn_kernel, grid_spec=grid_spec, name="dilated_attn",
        out_shape=jax.ShapeDtypeStruct((B, S, d_a), F32),
        compiler_params=pltpu.CompilerParams(dimension_semantics=("arbitrary", "arbitrary"),
                                             vmem_limit_bytes=V7X_VMEM_LIMIT),
    )(lidx, exact, qkv3, qkv3, qkv3, q_norm, k_norm, bd, bias)


def _pad_lora(w, offset):
    L, r, d = w.shape
    return jnp.zeros((L, LORA_SLAB, d), F32).at[:, offset:offset + r, :].set(w)


def kernel(x, norm_ffn1, ffn1_w_gate, ffn1_w_up, ffn1_w_down, norm_mix, w_in, shift_mu, rwkv_w0, rwkv_w2, rwkv_a0, rwkv_a2, rwkv_g2, rwkv_k_k, rwkv_k_a, rwkv_r_k, rwkv_ln_w, rwkv_ln_b, attn_q_norm, attn_k_norm, conv_dw_w, conv_dw_b, conv_ln_w, conv_ln_b, w_out, norm_ffn2, ffn2_w_gate, ffn2_w_up, ffn2_w_down):
    B, S, D = x.shape
    depth = w_in.shape[0]
    d_r = rwkv_w0.shape[-1]
    d_c = conv_dw_b.shape[-1]
    d_shift = shift_mu.shape[-1]
    d_a = (w_in.shape[-1] - d_shift - 2 * d_c) // 3
    r_w, r_a, r_g = rwkv_w2.shape[1], rwkv_a2.shape[1], rwkv_g2.shape[1]
    assert d_shift == 3 * d_r + LORA_SLAB and r_w + r_a + r_g == LORA_SLAB
    assert S % (2 * ATTN_BLOCK * DILATED_PATTERNS[-1][1]) == 0 and (d_a // HEAD_DIM) % 2 == 0
    assert (S // ATTN_BLOCK) % max(ATTN_GROUP, ATTN_GROUP_EXACT) == 0
    assert S % (PRE_DILATION * REGROUP_ROWS) == 0
    assert all(d < PRE_DILATION or d % PRE_DILATION == 0 for _, d in DILATED_PATTERNS)
    T = B * S
    assert T % FFN_ROWS == 0 and S % RWKV_ROWS == 0

    vec = lambda a: a.reshape(depth, 1, -1)
    bf = lambda a: a.astype(BF16)
    wg1, wu1, wd1 = bf(ffn1_w_gate), bf(ffn1_w_up), bf(ffn1_w_down)
    wg2, wu2, wd2 = bf(ffn2_w_gate), bf(ffn2_w_up), bf(ffn2_w_down)
    win, wo = bf(w_in), bf(w_out)
    g1, gm, g2 = vec(norm_ffn1), vec(norm_mix), vec(norm_ffn2)
    w2p = _pad_lora(rwkv_w2, 0)
    a2p = _pad_lora(rwkv_a2, r_w)
    g2p = _pad_lora(rwkv_g2, r_w + r_a)
    qn = vec(jnp.tile(attn_q_norm, (1, 2)))
    kn = vec(jnp.tile(attn_k_norm, (1, 2)))
    score_bound = (HEAD_DIM ** 0.5) * jnp.max(jnp.abs(attn_q_norm), -1) * jnp.max(jnp.abs(attn_k_norm), -1)
    attn_bias, attn_exact = lax.optimization_barrier(_attn_tables(score_bound, d_a // HEAD_DIM))

    layer_index = lambda l: jnp.reshape(l, (1,)).astype(jnp.int32)

    def project(l, xf):
        return _ffn_proj(layer_index(l), xf, g1, wg1, wu1, wd1, gm, win, (d_shift, 3 * d_a, 2 * d_c),
                         FFN_ROWS)

    def mix(l, projected):
        lidx = layer_index(l)
        x1, ps, qkv, u = projected
        y_r = _rwkv(lidx, ps.reshape(B, S, d_shift), vec(shift_mu), vec(rwkv_w0), w2p, vec(rwkv_a0),
                    a2p, g2p, vec(rwkv_k_k), vec(rwkv_k_a), vec(rwkv_r_k), vec(rwkv_ln_w),
                    vec(rwkv_ln_b), d_r, RWKV_ROWS)
        y_a = _attention(lidx, qkv.reshape(B, S, 3 * d_a), qn, kn, attn_bias, attn_exact, d_a)
        x2 = _mix_out(lidx, x1, y_r.reshape(T, d_r), y_a.reshape(T, d_a), u, conv_dw_w,
                      vec(conv_dw_b), vec(conv_ln_w), vec(conv_ln_b), wo, MIX_ROWS, S)
        return _ffn(lidx, x2, g2, wg2, wu2, wd2, 2 * FFN_ROWS)

    state = project(jnp.int32(0), x.reshape(T, D))
    state = lax.fori_loop(0, depth - 1, lambda l, st: tuple(project(l + 1, mix(l, st))), tuple(state))
    return mix(jnp.int32(depth - 1), state).reshape(B, S, D)
```

```python
import functools
import math

import jax
import jax.numpy as jnp
from jax import lax
from jax.experimental import pallas as pl
from jax.experimental.pallas import tpu as pltpu

F32 = jnp.float32
BF16 = jnp.bfloat16

LANES = 128
SUBLANES = 8
FFN1_ROWS = 512
FFN2_ROWS = 1024
MIX_ROWS = 1024
RWKV_ROWS = 1024
HEAD_DIM = 64
PAIR = 2 * HEAD_DIM
NORM_EPS = 1e-6
RWKV_GN_EPS = 64e-5
CONV_LN_EPS = 1e-5
KK_EPS = 1e-12
CONV_WIDTH = 31
DILATED_PATTERNS = ((128, 1), (512, 4), (2048, 16))
ATTN_BLOCK = 128
ATTN_GROUP = 16
ATTN_GROUP_EXACT = 4
PRE_DILATION = 4
REGROUP_ROWS = 256
LORA_SLAB = 128
CHUNK = 64
GROUP_CHUNKS = 8
CUMSUM_ROWS = 256
NEUMANN_STEPS = 6
MASK_VALUE = -1e30
SAFE_SOFTMAX_SHIFT = 40.0
EXP_NEG_HALF = math.exp(-0.5)
V7X_VMEM_LIMIT = 56 * 1024 * 1024

NT_DIMS = (((1,), (1,)), ((), ()))
TN_DIMS = (((0,), (0,)), ((), ()))


def _dot(a, b):
    return jnp.dot(a, b, preferred_element_type=F32)


def _dot_nt(a, b):
    return lax.dot_general(a, b, NT_DIMS, preferred_element_type=F32)


def _dot_tn(a, b):
    return lax.dot_general(a, b, TN_DIMS, preferred_element_type=F32)


def _split2(x):
    hi = x.astype(BF16)
    lo = (x - hi.astype(F32)).astype(BF16)
    return hi, lo


def _dot_bf16(x, w):
    return _dot(x.astype(BF16), w.astype(BF16))


def _dot_f32_rhs(w_exact, x):
    hi, lo = _split2(x)
    return _dot(w_exact, hi) + _dot(w_exact, lo)


def _dot_f32(x, w):
    xh, xl = _split2(x)
    wh, wl = _split2(w)
    return _dot(xh, wh) + _dot(xl, wh) + _dot(xh, wl)


def _sigmoid(x):
    return 0.5 + 0.5 * jnp.tanh(0.5 * x)


def _rms_norm(x, g):
    return x * lax.rsqrt(jnp.mean(x * x, axis=-1, keepdims=True) + NORM_EPS) * g


def _swiglu_residual(x, g, wg_ref, wu_ref, wd_ref):
    xn = _rms_norm(x, g).astype(BF16)
    gate = _dot(xn, wg_ref[...])
    up = _dot(xn, wu_ref[...])
    h =(gate * _sigmoid(gate) * up).astype(BF16)
    return x + 0.5 * _dot(h, wd_ref[...])


def _ffn_proj_kernel(l_ref, x_ref, g1_ref, wg_ref, wu_ref, wd_ref, gm_ref, win_ref,
                     x1_ref, ps_ref, qkv_ref, u_ref):
    del l_ref
    x1 = _swiglu_residual(x_ref[...], g1_ref[...], wg_ref, wu_ref, wd_ref)
    x1_ref[...] = x1
    h = _rms_norm(x1, gm_ref[...]).astype(BF16)
    proj = _dot(h, win_ref[...])
    d_shift = ps_ref.shape[-1]
    d_qkv = qkv_ref.shape[-1]
    ps_ref[...] = proj[:, :d_shift]
    qkv_ref[...] = proj[:, d_shift:d_shift + d_qkv]
    u_ref[...] = proj[:, d_shift + d_qkv:]


def _layer_spec(shape, buffered=True):
    nd = len(shape)
    kw = dict(pipeline_mode=pl.Buffered(1)) if buffered else {}
    return pl.BlockSpec((None,) + tuple(shape), lambda *a: (a[-1][0],) + (0,) * nd, **kw)


def _ffn_proj(lidx, x, g1, wg, wu, wd, gm, win, dims, tm):
    T, D = x.shape
    d_shift, d_qkv, d_u = dims
    row = lambda w: pl.BlockSpec((tm, w), lambda i, l: (i, 0))
    grid_spec = pltpu.PrefetchScalarGridSpec(
        num_scalar_prefetch=1, grid=(T // tm,),
        in_specs=[row(D), _layer_spec((1, D)), _layer_spec(wg.shape[1:]), _layer_spec(wu.shape[1:]),
                  _layer_spec(wd.shape[1:]), _layer_spec((1, D)), _layer_spec(win.shape[1:])],
        out_specs=[row(D), row(d_shift), row(d_qkv), row(d_u)])
    return pl.pallas_call(
        _ffn_proj_kernel, grid_spec=grid_spec, name="ffn_proj",
        out_shape=[jax.ShapeDtypeStruct((T, D), F32), jax.ShapeDtypeStruct((T, d_shift), F32),
                   jax.ShapeDtypeStruct((T, d_qkv), F32), jax.ShapeDtypeStruct((T, d_u), F32)],
        compiler_params=pltpu.CompilerParams(dimension_semantics=("arbitrary",),
                                             vmem_limit_bytes=V7X_VMEM_LIMIT),
    )(lidx, x, g1, wg, wu, wd, gm, win)


CONV_PAD = 32
CONV_ROWS = 64


def _conv_rows(z_ref, r0, w_ref, b_ref, lnw_ref, lnb_ref):
    d_c = z_ref.shape[-1]
    shift = CONV_PAD - (CONV_WIDTH - 1)
    n_win = CONV_ROWS + CONV_PAD
    win = z_ref[r0:r0 + n_win, :]
    acc = jnp.zeros((CONV_ROWS, d_c), F32) + b_ref[...]
    for sub in range(SUBLANES):
        rolled = win if sub == 0 else pltpu.roll(win, n_win - sub, 0)
        for j in range(CONV_WIDTH):
            off = shift + j
            if off % SUBLANES == sub:
                base = off - sub
                acc = acc + rolled[base:base + CONV_ROWS, :] * w_ref[j:j + 1, :]
    mean = jnp.mean(acc, axis=-1, keepdims=True)
    d = acc - mean
    var = jnp.mean(d * d, axis=-1, keepdims=True)
    z = d * lax.rsqrt(var + CONV_LN_EPS) * lnw_ref[...] + lnb_ref[...]
    return z * _sigmoid(z)


def _mix_out_kernel(tiles_per_seq, l_ref, x_ref, yr_ref, ya_ref, u_ref, cw_ref, cb_ref, clnw_ref,
                    clnb_ref, wo_ref, o_ref, z_s):
    del l_ref
    i = pl.program_id(0)
    tm = x_ref.shape[0]
    d_c = z_s.shape[-1]

    @pl.when(i == 0)
    def _():
        z_s[...] = jnp.zeros_like(z_s)

    left = z_s[tm:tm + CONV_PAD, :]
    z_s[0:CONV_PAD, :] = jnp.where(i % tiles_per_seq == 0, jnp.zeros_like(left), left)
    z_s[CONV_PAD:, :] = u_ref[:, 0:d_c] * _sigmoid(u_ref[:, d_c:])
    y_c = jnp.concatenate([_conv_rows(z_s, r0, cw_ref, cb_ref, clnw_ref, clnb_ref)
                           for r0 in range(0, tm, CONV_ROWS)], axis=0)
    mix = jnp.concatenate([yr_ref[...].astype(BF16), ya_ref[...].astype(BF16), y_c.astype(BF16)], axis=1)
    o_ref[...] = x_ref[...] + _dot(mix, wo_ref[...])


def _mix_out(lidx, x, yr, ya, u, cw, cb, clnw, clnb, wo, tm, seq_len):
    T, D = x.shape
    d_c = u.shape[1] // 2
    assert seq_len % tm == 0 and tm % CONV_ROWS == 0
    row = lambda w: pl.BlockSpec((tm, w), lambda i, l: (i, 0))
    grid_spec = pltpu.PrefetchScalarGridSpec(
        num_scalar_prefetch=1, grid=(T // tm,),
        in_specs=[row(D), row(yr.shape[1]), row(ya.shape[1]), row(2 * d_c),
                  _layer_spec((CONV_WIDTH, d_c), buffered=False), _layer_spec((1, d_c), buffered=False),
                  _layer_spec((1, d_c), buffered=False), _layer_spec((1, d_c), buffered=False),
                  _layer_spec(wo.shape[1:])],
        out_specs=row(D),
        scratch_shapes=[pltpu.VMEM((CONV_PAD + tm, d_c), F32)])
    return pl.pallas_call(
        functools.partial(_mix_out_kernel, seq_len // tm), grid_spec=grid_spec, name="mix_out",
        out_shape=jax.ShapeDtypeStruct((T, D), F32),
        compiler_params=pltpu.CompilerParams(dimension_semantics=("arbitrary",),
                                             vmem_limit_bytes=V7X_VMEM_LIMIT),
    )(lidx, x, yr, ya, u, cw, cb, clnw, clnb, wo)


def _ffn_kernel(l_ref, x_ref, g_ref, wg_ref, wu_ref, wd_ref, o_ref):
    del l_ref
    o_ref[...] = _swiglu_residual(x_ref[...], g_ref[...], wg_ref, wu_ref, wd_ref)


def _ffn(lidx, x, g, wg, wu, wd, tm):
    T, D = x.shape
    row = pl.BlockSpec((tm, D), lambda i, l: (i, 0))
    grid_spec = pltpu.PrefetchScalarGridSpec(
        num_scalar_prefetch=1, grid=(T // tm,),
        in_specs=[row, _layer_spec((1, D)), _layer_spec(wg.shape[1:]), _layer_spec(wu.shape[1:]),
                  _layer_spec(wd.shape[1:])],
        out_specs=row)
    return pl.pallas_call(
        _ffn_kernel, grid_spec=grid_spec, name="ffn2",
        out_shape=jax.ShapeDtypeStruct((T, D), F32),
        compiler_params=pltpu.CompilerParams(dimension_semantics=("arbitrary",),
                                             vmem_limit_bytes=V7X_VMEM_LIMIT),
    )(lidx, x, g, wg, wu, wd)


def _rwkv_kernel(l_ref, p_ref, mu_ref, w0_ref, w2_ref, a0_ref, a2_ref, g2_ref, kk_ref, ka_ref,
                 rk_ref, lnw_ref, lnb_ref, bd_ref, tri_ref, o_ref,
                 carry_ref, state_ref):
    del l_ref
    tb = p_ref.shape[1]
    d_r = o_ref.shape[-1]

    @pl.when(pl.program_id(1) == 0)
    def _():
        carry_ref[...] = jnp.zeros_like(carry_ref)
        state_ref[...] = jnp.zeros_like(state_ref)

    p = p_ref[0]
    row = lax.broadcasted_iota(jnp.int32, (tb, 1), 0)
    prev = jnp.where(row == 0, carry_ref[...], pltpu.roll(p, 1, 0))
    carry_ref[...] = p[tb - 1:tb, :]
    ps = p + (prev - p) * mu_ref[...]

    r = ps[:, 0:d_r]
    k = ps[:, d_r:2 * d_r]
    v = ps[:, 2 * d_r:3 * d_r]
    xs = ps[:, 3 * d_r:3 * d_r + LORA_SLAB]

    bd = bd_ref[...]
    def seg_sum(t):
        tb16 = t.astype(BF16)
        return jnp.concatenate([_dot(tb16[:, s:s + PAIR], bd) for s in range(0, d_r, PAIR)], axis=1)

    lw = w0_ref[...] + _dot_f32(jnp.tanh(xs), w2_ref[...])
    ld = (-0.5 * EXP_NEG_HALF) * jnp.tanh(0.5 * lw) - 0.5 * EXP_NEG_HALF
    a = _sigmoid(a0_ref[...] + _dot_bf16(xs, a2_ref[...]))
    g = _dot_bf16(_sigmoid(xs), g2_ref[...])
    kk = k * kk_ref[...]
    kk = kk * lax.rsqrt(seg_sum(kk * kk) + KK_EPS)
    ka = ka_ref[...]
    k2 = k * ((1.0 - ka) + a * ka)
    b = kk * a

    lcum = jnp.concatenate(
        [_dot_f32_rhs(tri_ref[...], ld[s:s + CUMSUM_ROWS]) for s in range(0, tb, CUMSUM_ROWS)], axis=0)
    n_chunks = tb // CHUNK
    pc_rows = [jnp.exp(lcum[(c + 1) * CHUNK - 1:(c + 1) * CHUNK, :]) for c in range(n_chunks)]
    at_all = (-kk * jnp.exp(lcum - ld)).astype(BF16)
    rt_all = r * jnp.exp(lcum)
    inv_p = jnp.exp(-lcum)
    bt_all = (b * inv_p).astype(BF16)
    kt_all = (k2 * inv_p).astype(BF16)
    rest = inv_p * jnp.concatenate([jnp.broadcast_to(pc, (CHUNK, d_r)) for pc in pc_rows], axis=0)
    bp_all = (b * rest).astype(BF16)
    kp_all = (k2 * rest).astype(BF16)
    v_all = v.astype(BF16)

    lane = lax.broadcasted_iota(jnp.int32, (CHUNK, PAIR), 1)
    trow = lax.broadcasted_iota(jnp.int32, (CHUNK, PAIR), 0)
    head0 = lane < HEAD_DIM
    scol = lane & (HEAD_DIM - 1)
    strict = trow > scol
    incl = trow >= scol
    eye2 = jnp.where(trow == scol, 1.0, 0.0)
    r2 = lax.broadcasted_iota(jnp.int32, (PAIR, PAIR), 0)
    c2 = lax.broadcasted_iota(jnp.int32, (PAIR, PAIR), 1)
    same_head = (r2 < HEAD_DIM) == (c2 < HEAD_DIM)

    def block_diag(xb):
        zero = jnp.zeros_like(xb)
        return jnp.concatenate([jnp.where(head0, xb, zero), jnp.where(head0, zero, xb)], axis=0)

    n_pairs = d_r // PAIR
    tile = lambda t, c, p: t[c * CHUNK:(c + 1) * CHUNK, p * PAIR:(p + 1) * PAIR]
    inv_n = 1.0 / HEAD_DIM

    for c_lo in range(0, n_chunks, GROUP_CHUNKS):
        group_chunks = range(c_lo, min(c_lo + GROUP_CHUNKS, n_chunks))
        units = [(c, p) for c in group_chunks for p in range(n_pairs)]
        y = _rwkv_group(units, group_chunks, n_pairs, tile, block_diag, state_ref, pc_rows,
                        (at_all, rt_all, bt_all, kt_all, bp_all, kp_all, v_all),
                        (strict, incl, eye2, same_head))
        rows = slice(group_chunks[0] * CHUNK, (group_chunks[-1] + 1) * CHUNK)
        mean = seg_sum(y) * inv_n
        d = y - mean
        var = seg_sum(d * d) * inv_n
        yn = d * lax.rsqrt(var + RWKV_GN_EPS) * lnw_ref[...] + lnb_ref[...]
        bonus = seg_sum(r[rows] * k2[rows] * rk_ref[...]) * v[rows]
        o_ref[0, rows, :] = (yn + bonus) * g[rows]


def _rwkv_group(units, group_chunks, n_pairs, tile, block_diag, state_ref, pc_rows, scaled, masks):
    at_all, rt_all, bt_all, kt_all, bp_all, kp_all, v_all = scaled
    strict, incl, eye2, same_head = masks
    a_ab, a_ak, a_rb, a_rk = {}, {}, {}, {}
    for u in units:
        ar = jnp.concatenate([tile(at_all, *u), tile(rt_all, *u).astype(BF16)], axis=0)
        bk = jnp.concatenate([block_diag(tile(bt_all, *u)), block_diag(tile(kt_all, *u))], axis=0)
        g2 = _dot_nt(ar, bk)
        a_ab[u] = jnp.where(strict, g2[:CHUNK, :PAIR], 0.0)
        a_ak[u] = jnp.where(strict, g2[:CHUNK, PAIR:], 0.0).astype(BF16)
        a_rb[u] = jnp.where(incl, g2[CHUNK:, :PAIR], 0.0).astype(BF16)
        a_rk[u] = jnp.where(incl, g2[CHUNK:, PAIR:], 0.0).astype(BF16)
    tinv = {u: eye2 + a_ab[u] for u in units}
    apow = {}
    for u in units:
        ab = a_ab[u].astype(BF16)
        apow[u] = _dot(ab, block_diag(ab))
    for step in range(1, NEUMANN_STEPS):
        last = step == NEUMANN_STEPS - 1
        for u in units:
            ab = apow[u].astype(BF16)
            lhs = tinv[u].astype(BF16) if last else jnp.concatenate([tinv[u].astype(BF16), ab], axis=0)
            prod = _dot(lhs, block_diag(ab))
            tinv[u] = tinv[u] + prod[:CHUNK]
            if not last:
                apow[u] = prod[CHUNK:]
    av, y1v, w1, u1, rq, y1, mx, gx = {}, {}, {}, {}, {}, {}, {}, {}
    for u in units:
        avk = _dot(jnp.concatenate([a_ak[u], a_rk[u]], axis=0), block_diag(tile(v_all, *u)))
        av[u], y1v[u] = avk[:CHUNK].astype(BF16), avk[CHUNK:]
    for u in units:
        rhs = jnp.concatenate([block_diag(tile(at_all, *u)), block_diag(av[u])], axis=1)
        wu = _dot(tinv[u].astype(BF16), rhs).astype(BF16)
        w1[u], u1[u] = wu[:, :PAIR], wu[:, PAIR:]
    for u in units:
        vc = tile(v_all, *u)
        bpkp = jnp.concatenate([tile(bp_all, *u), tile(kp_all, *u)], axis=0)
        ry = _dot(a_rb[u], jnp.concatenate([block_diag(w1[u]), block_diag(u1[u])], axis=1))
        rq[u] = (tile(rt_all, *u) + ry[:, :PAIR]).astype(BF16)
        y1[u] = ry[:, PAIR:] + y1v[u]
        lhs = jnp.concatenate([jnp.concatenate([w1[u], u1[u]], axis=1),
                               jnp.concatenate([jnp.zeros_like(vc), vc], axis=1)], axis=0)
        mg = _dot_tn(lhs, bpkp)
        mx[u] = jnp.where(same_head, mg[:PAIR], 0.0).astype(BF16)
        gx[u] = jnp.where(same_head, mg[PAIR:], 0.0)

    y_rows = []
    for c in group_chunks:
        y_parts = []
        for p in range(n_pairs):
            u = (c, p)
            s0 = state_ref[p]
            s0b = s0.astype(BF16)
            y_parts.append(_dot_nt(rq[u], s0b) + y1[u])
            pc = pc_rows[c][:, p * PAIR:(p + 1) * PAIR]
            state_ref[p] = s0 * pc + _dot(s0b, mx[u]) + gx[u]
        y_rows.append(jnp.concatenate(y_parts, axis=1))
    return jnp.concatenate(y_rows, axis=0)


def _rwkv(lidx, ps3, mu, w0, w2p, a0, a2p, g2p, k_k, k_a, r_k, ln_w, ln_b, d_r, tb):
    B, S, d_shift = ps3.shape
    hid = jnp.arange(PAIR) // HEAD_DIM
    bd = (hid[:, None] == hid[None, :]).astype(BF16)
    assert tb % CUMSUM_ROWS == 0
    ci = jnp.arange(CUMSUM_ROWS)
    same = (ci[:, None] // CHUNK) == (ci[None, :] // CHUNK)
    tri = (same & (ci[:, None] >= ci[None, :])).astype(BF16)
    const = lambda a: pl.BlockSpec(a.shape, lambda b, t, l: (0,) * a.ndim)
    vec = _layer_spec((1, d_r), buffered=False)
    lora = _layer_spec((LORA_SLAB, d_r), buffered=False)
    grid_spec = pltpu.PrefetchScalarGridSpec(
        num_scalar_prefetch=1, grid=(B, S // tb),
        in_specs=[pl.BlockSpec((1, tb, d_shift), lambda b, t, l: (b, t, 0)),
                  _layer_spec((1, d_shift), buffered=False),
                  vec, lora, vec, lora, lora, vec, vec, vec, vec, vec,
                  const(bd), const(tri)],
        out_specs=pl.BlockSpec((1, tb, d_r), lambda b, t, l: (b, t, 0)),
        scratch_shapes=[pltpu.VMEM((1, d_shift), F32),
                        pltpu.VMEM((d_r // PAIR, PAIR, PAIR), F32)])
    return pl.pallas_call(
        _rwkv_kernel, grid_spec=grid_spec, name="rwkv7",
        out_shape=jax.ShapeDtypeStruct((B, S, d_r), F32),
        compiler_params=pltpu.CompilerParams(dimension_semantics=("arbitrary", "arbitrary"),
                                             vmem_limit_bytes=V7X_VMEM_LIMIT),
    )(lidx, ps3, mu, w0, w2p, a0, a2p, g2p, k_k, k_a, r_k, ln_w, ln_b, bd, tri)


def _attn_kernel(l_ref, exact_ref, q_ref, k_ref, v_ref, qn_ref, kn_ref, bd_ref, bias_ref, o_ref,
                 q_s, k_s, q4_s, k4_s, v4_s, st_s):
    S = q_ref.shape[1]
    bd = bd_ref[...]
    inv_n = 1.0 / HEAD_DIM

    def head_rms(t, gain):
        ms = _dot_bf16(t * t, bd) * inv_n
        return t * lax.rsqrt(ms + NORM_EPS) * gain

    q_s[...] = head_rms(q_ref[0], qn_ref[...]) * (HEAD_DIM ** -0.5)
    k_s[...] = head_rms(k_ref[0], kn_ref[...])
    head0 = lax.broadcasted_iota(jnp.int32, (ATTN_BLOCK, PAIR), 1) < HEAD_DIM

    s_pre = S // PRE_DILATION
    v_tok = v_ref.at[0]

    def regroup(i, carry):
        c = i // (s_pre // REGROUP_ROWS)
        r0 = (i % (s_pre // REGROUP_ROWS)) * REGROUP_ROWS
        src = pl.ds(c + PRE_DILATION * r0, REGROUP_ROWS, stride=PRE_DILATION)
        dst = pl.ds(pl.multiple_of(c * s_pre + r0, REGROUP_ROWS), REGROUP_ROWS)
        q4_s[dst, :] = q_s[src, :]
        k4_s[dst, :] = k_s[src, :]
        v4_s[dst, :] = v_tok[src, :]
        return carry

    lax.fori_loop(0, S // REGROUP_ROWS, regroup, 0)

    exact = exact_ref[l_ref[0]] == 1
    pl.when(exact)(lambda: _attn_patterns(True, S, s_pre, head0, v_tok, bias_ref, o_ref,
                                          q_s, k_s, q4_s, k4_s, v4_s, st_s))
    pl.when(jnp.logical_not(exact))(lambda: _attn_patterns(False, S, s_pre, head0, v_tok, bias_ref, o_ref,
                                                           q_s, k_s, q4_s, k4_s, v4_s, st_s))


def _attn_patterns(row_max, S, s_pre, head0, v_tok, bias_ref, o_ref, q_s, k_s, q4_s, k4_s, v4_s, st_s):
    group = ATTN_GROUP_EXACT if row_max else ATTN_GROUP
    for pi, (window, dil) in enumerate(DILATED_PATTERNS):
        n_sub = S // dil
        n_blk = n_sub // ATTN_BLOCK
        regrouped = dil % PRE_DILATION == 0
        if regrouped:
            stride = dil // PRE_DILATION
            q_src, k_src, v_src = q4_s, k4_s, v4_s
        else:
            stride = dil
            q_src, k_src, v_src = q_s, k_s, v_tok
        state = tuple(st_s.at[3 * pi + j] for j in range(3))
        span = stride * ATTN_BLOCK

        def group_body(gi, carry):
            blocks = []
            for g in range(group):
                i = gi * group + g
                n = i // dil
                res = i % dil
                if regrouped:
                    q_start = (res % PRE_DILATION) * s_pre + res // PRE_DILATION + span * n
                else:
                    q_start = res + span * n
                first = jnp.where(n == 0, 1, 0)
                k_start = q_start - span * (1 - first)
                if stride > 1:
                    q_rows = pl.ds(q_start, ATTN_BLOCK, stride=stride)
                    k_rows = pl.ds(k_start, 2 * ATTN_BLOCK, stride=stride)
                else:
                    q_rows = pl.ds(pl.multiple_of(q_start, ATTN_BLOCK), ATTN_BLOCK)
                    k_rows = pl.ds(pl.multiple_of(k_start, ATTN_BLOCK), 2 * ATTN_BLOCK)
                blocks.append(dict(first=first, q_rows=q_rows, k_rows=k_rows))
            for blk in blocks:
                q2 = q_src[blk["q_rows"], :]
                zero = jnp.zeros_like(q2)
                blk["q"] = jnp.concatenate([jnp.where(head0, q2, zero), jnp.where(head0, zero, q2)],
                                           axis=0).astype(BF16)
                blk["k"] = k_src[blk["k_rows"], :].astype(BF16)
                blk["v"] = v_src[blk["k_rows"], :].astype(BF16)
            for blk in blocks:
                blk["s"] = _dot_nt(blk["q"], blk["k"]) + bias_ref[pi, blk["first"]]
            if row_max:
                for blk in blocks:
                    blk["m"] = jnp.max(blk["s"], axis=-1, keepdims=True)
                for blk in blocks:
                    blk["e"] = jnp.exp(blk["s"] - blk["m"])
            else:
                for blk in blocks:
                    blk["e"] = jnp.exp(blk["s"])
            for blk in blocks:
                blk["l"] = jnp.sum(blk["e"], axis=-1, keepdims=True)
                blk["o"] = _dot(blk["e"].astype(BF16), blk["v"])
            for blk in blocks:
                pair = lambda t: jnp.where(head0, jnp.broadcast_to(t[:ATTN_BLOCK], (ATTN_BLOCK, PAIR)),
                                           jnp.broadcast_to(t[ATTN_BLOCK:], (ATTN_BLOCK, PAIR)))
                blk["out"] = (pair(blk["o"]), pair(blk["m"]) if row_max else None, pair(blk["l"]))
            for blk in blocks:
                for ref, val in zip(state, blk["out"]):
                    if val is not None:
                        ref[blk["q_rows"], :] = val
            return carry

        lax.fori_loop(0, dil * n_blk // group, group_body, 0)

    def finish(i, carry):
        c = i // (s_pre // ATTN_BLOCK)
        r0 = (i % (s_pre // ATTN_BLOCK)) * ATTN_BLOCK
        tok = pl.ds(c + PRE_DILATION * r0, ATTN_BLOCK, stride=PRE_DILATION)
        grp = pl.ds(pl.multiple_of(c * s_pre + r0, ATTN_BLOCK), ATTN_BLOCK)
        total = None
        for pi, (_, dil) in enumerate(DILATED_PATTERNS):
            rows = grp if dil % PRE_DILATION == 0 else tok
            if row_max:
                part = tuple(st_s[3 * pi + j, rows, :] for j in range(3))
                total = part if total is None else _softmax_merge(total, part)
            else:
                part = (st_s[3 * pi, rows, :], None, st_s[3 * pi + 2, rows, :])
                total = part if total is None else (total[0] + part[0], None, total[2] + part[2])
        o_ref[0, tok, :] = total[0] / total[2]
        return carry

    lax.fori_loop(0, S // ATTN_BLOCK, finish, 0)


def _softmax_merge(a, b):
    acc_a, m_a, l_a = a
    acc_b, m_b, l_b = b
    m_new = jnp.maximum(m_a, m_b)
    w_a = jnp.exp(m_a - m_new)
    w_b = jnp.exp(m_b - m_new)
    return acc_a * w_a + acc_b * w_b, m_new, l_a * w_a + l_b * w_b


def _alibi_slopes(n):
    def pow2(m):
        start = 2.0 ** (-8.0 / m)
        return [start ** (i + 1) for i in range(m)]
    if math.log2(n).is_integer():
        return pow2(n)
    c = 2 ** int(math.floor(math.log2(n)))
    return pow2(c) + pow2(2 * c)[0::2][: n - c]


def _attn_bias(n_heads):
    qi = jnp.arange(ATTN_BLOCK)
    ki = jnp.arange(2 * ATTN_BLOCK)
    dist = qi[:, None] + ATTN_BLOCK - ki[None, :]
    slopes = jnp.asarray(_alibi_slopes(n_heads), F32)
    out = []
    for window, dil in DILATED_PATTERNS:
        valid = (dist >= 0) & (dist <= window // dil)
        bias = -slopes[:, None, None] * (dist * dil).astype(F32)[None]
        rest = jnp.where(valid[None], bias, MASK_VALUE)
        first = jnp.concatenate([rest[..., ATTN_BLOCK:], jnp.full_like(rest[..., ATTN_BLOCK:], MASK_VALUE)], -1)
        out.append(jnp.stack([rest, first], 0).reshape(2, n_heads * ATTN_BLOCK, 2 * ATTN_BLOCK))
    return jnp.stack(out, 0)


def _attn_tables(score_bound, n_heads):
    bias = _attn_bias(n_heads)[None] - score_bound[:, None, None, None, None]
    return bias, (score_bound > SAFE_SOFTMAX_SHIFT).astype(jnp.int32)


def _attention(lidx, qkv3, q_norm, k_norm, bias, exact, d_a):
    B, S, _ = qkv3.shape
    n_heads = d_a // HEAD_DIM
    n_pairs = n_heads // 2
    slab = 2 * HEAD_DIM
    hid = jnp.arange(slab) // HEAD_DIM
    bd = (hid[:, None] == hid[None, :]).astype(BF16)
    col = lambda off: pl.BlockSpec((1, S, slab), lambda b, p, l, e: (b, 0, off + p))
    gain = pl.BlockSpec((None, 1, slab), lambda b, p, l, e: (l[0], 0, 0))
    grid_spec = pltpu.PrefetchScalarGridSpec(
        num_scalar_prefetch=2, grid=(B, n_pairs),
        in_specs=[col(0), col(n_pairs), col(2 * n_pairs), gain, gain,
                  pl.BlockSpec(bd.shape, lambda b, p, l, e: (0, 0)),
                  pl.BlockSpec((None, len(DILATED_PATTERNS), 2, 2 * ATTN_BLOCK, 2 * ATTN_BLOCK),
                               lambda b, p, l, e: (l[0], 0, 0, p, 0))],
        out_specs=pl.BlockSpec((1, S, slab), lambda b, p, l, e: (b, 0, p)),
        scratch_shapes=[pltpu.VMEM((S, slab), F32)] * 5
        + [pltpu.VMEM((3 * len(DILATED_PATTERNS), S, slab), F32)])
    return pl.pallas_call(
        _attn_kernel, grid_spec=grid_spec, name="dilated_attn",
        out_shape=jax.ShapeDtypeStruct((B, S, d_a), F32),
        compiler_params=pltpu.CompilerParams(dimension_semantics=("arbitrary", "arbitrary"),
                                             vmem_limit_bytes=V7X_VMEM_LIMIT),
    )(lidx, exact, qkv3, qkv3, qkv3, q_norm, k_norm, bd, bias)


def _pad_lora(w, offset):
    L, r, d = w.shape
    return jnp.zeros((L, LORA_SLAB, d), F32).at[:, offset:offset + r, :].set(w)


def kernel(x, norm_ffn1, ffn1_w_gate, ffn1_w_up, ffn1_w_down, norm_mix, w_in, shift_mu, rwkv_w0, rwkv_w2, rwkv_a0, rwkv_a2, rwkv_g2, rwkv_k_k, rwkv_k_a, rwkv_r_k, rwkv_ln_w, rwkv_ln_b, attn_q_norm, attn_k_norm, conv_dw_w, conv_dw_b, conv_ln_w, conv_ln_b, w_out, norm_ffn2, ffn2_w_gate, ffn2_w_up, ffn2_w_down):
    B, S, D = x.shape
    depth = w_in.shape[0]
    d_r = rwkv_w0.shape[-1]
    d_c = conv_dw_b.shape[-1]
    d_shift = shift_mu.shape[-1]
    d_a = (w_in.shape[-1] - d_shift - 2 * d_c) // 3
    r_w, r_a, r_g = rwkv_w2.shape[1], rwkv_a2.shape[1], rwkv_g2.shape[1]
    assert d_shift == 3 * d_r + LORA_SLAB and r_w + r_a + r_g == LORA_SLAB
    assert S % (2 * ATTN_BLOCK * DILATED_PATTERNS[-1][1]) == 0 and (d_a // HEAD_DIM) % 2 == 0
    assert (S // ATTN_BLOCK) % max(ATTN_GROUP, ATTN_GROUP_EXACT) == 0
    assert S % (PRE_DILATION * REGROUP_ROWS) == 0
    assert all(d < PRE_DILATION or d % PRE_DILATION == 0 for _, d in DILATED_PATTERNS)
    T = B * S
    assert T % FFN1_ROWS == 0 and T % FFN2_ROWS == 0 and S % RWKV_ROWS == 0 and S % MIX_ROWS == 0

    vec = lambda a: a.reshape(depth, 1, -1)
    bf = lambda a: a.astype(BF16)
    wg1, wu1, wd1 = bf(ffn1_w_gate), bf(ffn1_w_up), bf(ffn1_w_down)
    wg2, wu2, wd2 = bf(ffn2_w_gate), bf(ffn2_w_up), bf(ffn2_w_down)
    win, wo = bf(w_in), bf(w_out)
    g1, gm, g2 = vec(norm_ffn1), vec(norm_mix), vec(norm_ffn2)
    w2p = _pad_lora(rwkv_w2, 0)
    a2p = _pad_lora(rwkv_a2, r_w)
    g2p = _pad_lora(rwkv_g2, r_w + r_a)
    qn = vec(jnp.tile(attn_q_norm, (1, 2)))
    kn = vec(jnp.tile(attn_k_norm, (1, 2)))
    score_bound = (HEAD_DIM ** 0.5) * jnp.max(jnp.abs(attn_q_norm), -1) * jnp.max(jnp.abs(attn_k_norm), -1)
    attn_bias, attn_exact = lax.optimization_barrier(_attn_tables(score_bound, d_a // HEAD_DIM))

    layer_index = lambda l: jnp.reshape(l, (1,)).astype(jnp.int32)

    def project(l, xf):
        return _ffn_proj(layer_index(l), xf, g1, wg1, wu1, wd1, gm, win, (d_shift, 3 * d_a, 2 * d_c),
                         FFN1_ROWS)

    def mix(l, projected):
        lidx = layer_index(l)
        x1, ps, qkv, u = projected
        y_r = _rwkv(lidx, ps.reshape(B, S, d_shift), vec(shift_mu), vec(rwkv_w0), w2p, vec(rwkv_a0),
                    a2p, g2p, vec(rwkv_k_k), vec(rwkv_k_a), vec(rwkv_r_k), vec(rwkv_ln_w),
                    vec(rwkv_ln_b), d_r, RWKV_ROWS)
        y_a = _attention(lidx, qkv.reshape(B, S, 3 * d_a), qn, kn, attn_bias, attn_exact, d_a)
        x2 = _mix_out(lidx, x1, y_r.reshape(T, d_r), y_a.reshape(T, d_a), u, conv_dw_w,
                      vec(conv_dw_b), vec(conv_ln_w), vec(conv_ln_b), wo, MIX_ROWS, S)
        return _ffn(lidx, x2, g2, wg2, wu2, wd2, FFN2_ROWS)

    state = project(jnp.int32(0), x.reshape(T, D))
    state = lax.fori_loop(0, depth - 1, lambda l, st: tuple(project(l + 1, mix(l, st))), tuple(state))
    return mix(jnp.int32(depth - 1), state).reshape(B, S, D)
```

```python
import functools
import math

import jax
import jax.numpy as jnp
from jax import lax
from jax.experimental import pallas as pl
from jax.experimental.pallas import tpu as pltpu

F32 = jnp.float32
BF16 = jnp.bfloat16

LANES = 128
SUBLANES = 8
FFN1_ROWS = 512
FFN2_ROWS = 1024
MIX_ROWS = 1024
RWKV_ROWS = 1024
HEAD_DIM = 64
PAIR = 2 * HEAD_DIM
NORM_EPS = 1e-6
RWKV_GN_EPS = 64e-5
CONV_LN_EPS = 1e-5
KK_EPS = 1e-12
CONV_WIDTH = 31
DILATED_PATTERNS = ((128, 1), (512, 4), (2048, 16))
ATTN_BLOCK = 128
ATTN_GROUP = 32
ATTN_GROUP_EXACT = 4
PRE_DILATION = 4
REGROUP_ROWS = 256
LORA_SLAB = 128
CHUNK = 64
GROUP_CHUNKS = 8
CUMSUM_ROWS = 256
NEUMANN_STEPS = 6
MASK_VALUE = -1e30
SAFE_SOFTMAX_SHIFT = 40.0
EXP_NEG_HALF = math.exp(-0.5)
V7X_VMEM_LIMIT = 56 * 1024 * 1024

NT_DIMS = (((1,), (1,)), ((), ()))
TN_DIMS = (((0,), (0,)), ((), ()))


def _dot(a, b):
    return jnp.dot(a, b, preferred_element_type=F32)


def _dot_nt(a, b):
    return lax.dot_general(a, b, NT_DIMS, preferred_element_type=F32)


def _dot_tn(a, b):
    return lax.dot_general(a, b, TN_DIMS, preferred_element_type=F32)


def _split2(x):
    hi = x.astype(BF16)
    lo = (x - hi.astype(F32)).astype(BF16)
    return hi, lo


def _dot_bf16(x, w):
    return _dot(x.astype(BF16), w.astype(BF16))


def _dot_f32_rhs(w_exact, x):
    hi, lo = _split2(x)
    return _dot(w_exact, hi) + _dot(w_exact, lo)


def _dot_f32(x, w):
    xh, xl = _split2(x)
    wh, wl = _split2(w)
    return _dot(xh, wh) + _dot(xl, wh) + _dot(xh, wl)


def _sigmoid(x):
    return 0.5 + 0.5 * jnp.tanh(0.5 * x)


def _rms_norm(x, g):
    return x * lax.rsqrt(jnp.mean(x * x, axis=-1, keepdims=True) + NORM_EPS) * g


def _swiglu_residual(x, g, wg_ref, wu_ref, wd_ref):
    xn = _rms_norm(x, g).astype(BF16)
    gate = _dot(xn, wg_ref[...])
    up = _dot(xn, wu_ref[...])
    h =(gate * _sigmoid(gate) * up).astype(BF16)
    return x + 0.5 * _dot(h, wd_ref[...])


def _ffn_proj_kernel(l_ref, x_ref, g1_ref, wg_ref, wu_ref, wd_ref, gm_ref, win_ref,
                     x1_ref, ps_ref, qkv_ref, u_ref):
    del l_ref
    x1 = _swiglu_residual(x_ref[...], g1_ref[...], wg_ref, wu_ref, wd_ref)
    x1_ref[...] = x1
    h = _rms_norm(x1, gm_ref[...]).astype(BF16)
    proj = _dot(h, win_ref[...])
    d_shift = ps_ref.shape[-1]
    d_qkv = qkv_ref.shape[-1]
    ps_ref[...] = proj[:, :d_shift]
    qkv_ref[...] = proj[:, d_shift:d_shift + d_qkv]
    u_ref[...] = proj[:, d_shift + d_qkv:]


def _layer_spec(shape, buffered=True):
    nd = len(shape)
    kw = dict(pipeline_mode=pl.Buffered(1)) if buffered else {}
    return pl.BlockSpec((None,) + tuple(shape), lambda *a: (a[-1][0],) + (0,) * nd, **kw)


def _ffn_proj(lidx, x, g1, wg, wu, wd, gm, win, dims, tm):
    T, D = x.shape
    d_shift, d_qkv, d_u = dims
    row = lambda w: pl.BlockSpec((tm, w), lambda i, l: (i, 0))
    grid_spec = pltpu.PrefetchScalarGridSpec(
        num_scalar_prefetch=1, grid=(T // tm,),
        in_specs=[row(D), _layer_spec((1, D)), _layer_spec(wg.shape[1:]), _layer_spec(wu.shape[1:]),
                  _layer_spec(wd.shape[1:]), _layer_spec((1, D)), _layer_spec(win.shape[1:])],
        out_specs=[row(D), row(d_shift), row(d_qkv), row(d_u)])
    return pl.pallas_call(
        _ffn_proj_kernel, grid_spec=grid_spec, name="ffn_proj",
        out_shape=[jax.ShapeDtypeStruct((T, D), F32), jax.ShapeDtypeStruct((T, d_shift), F32),
                   jax.ShapeDtypeStruct((T, d_qkv), F32), jax.ShapeDtypeStruct((T, d_u), F32)],
        compiler_params=pltpu.CompilerParams(dimension_semantics=("arbitrary",),
                                             vmem_limit_bytes=V7X_VMEM_LIMIT),
    )(lidx, x, g1, wg, wu, wd, gm, win)


CONV_PAD = 32
CONV_ROWS = 64


def _conv_rows(z_ref, r0, w_ref, b_ref, lnw_ref, lnb_ref):
    d_c = z_ref.shape[-1]
    shift = CONV_PAD - (CONV_WIDTH - 1)
    n_win = CONV_ROWS + CONV_PAD
    win = z_ref[r0:r0 + n_win, :]
    acc = jnp.zeros((CONV_ROWS, d_c), F32) + b_ref[...]
    for sub in range(SUBLANES):
        rolled = win if sub == 0 else pltpu.roll(win, n_win - sub, 0)
        for j in range(CONV_WIDTH):
            off = shift + j
            if off % SUBLANES == sub:
                base = off - sub
                acc = acc + rolled[base:base + CONV_ROWS, :] * w_ref[j:j + 1, :]
    mean = jnp.mean(acc, axis=-1, keepdims=True)
    d = acc - mean
    var = jnp.mean(d * d, axis=-1, keepdims=True)
    z = d * lax.rsqrt(var + CONV_LN_EPS) * lnw_ref[...] + lnb_ref[...]
    return z * _sigmoid(z)


def _mix_out_kernel(tiles_per_seq, l_ref, x_ref, yr_ref, ya_ref, u_ref, cw_ref, cb_ref, clnw_ref,
                    clnb_ref, wo_ref, o_ref, z_s):
    del l_ref
    i = pl.program_id(0)
    tm = x_ref.shape[0]
    d_c = z_s.shape[-1]

    @pl.when(i == 0)
    def _():
        z_s[...] = jnp.zeros_like(z_s)

    left = z_s[tm:tm + CONV_PAD, :]
    z_s[0:CONV_PAD, :] = jnp.where(i % tiles_per_seq == 0, jnp.zeros_like(left), left)
    z_s[CONV_PAD:, :] = u_ref[:, 0:d_c] * _sigmoid(u_ref[:, d_c:])
    y_c = jnp.concatenate([_conv_rows(z_s, r0, cw_ref, cb_ref, clnw_ref, clnb_ref)
                           for r0 in range(0, tm, CONV_ROWS)], axis=0)
    mix = jnp.concatenate([yr_ref[...].astype(BF16), ya_ref[...].astype(BF16), y_c.astype(BF16)], axis=1)
    o_ref[...] = x_ref[...] + _dot(mix, wo_ref[...])


def _mix_out(lidx, x, yr, ya, u, cw, cb, clnw, clnb, wo, tm, seq_len):
    T, D = x.shape
    d_c = u.shape[1] // 2
    assert seq_len % tm == 0 and tm % CONV_ROWS == 0
    row = lambda w: pl.BlockSpec((tm, w), lambda i, l: (i, 0))
    grid_spec = pltpu.PrefetchScalarGridSpec(
        num_scalar_prefetch=1, grid=(T // tm,),
        in_specs=[row(D), row(yr.shape[1]), row(ya.shape[1]), row(2 * d_c),
                  _layer_spec((CONV_WIDTH, d_c), buffered=False), _layer_spec((1, d_c), buffered=False),
                  _layer_spec((1, d_c), buffered=False), _layer_spec((1, d_c), buffered=False),
                  _layer_spec(wo.shape[1:])],
        out_specs=row(D),
        scratch_shapes=[pltpu.VMEM((CONV_PAD + tm, d_c), F32)])
    return pl.pallas_call(
        functools.partial(_mix_out_kernel, seq_len // tm), grid_spec=grid_spec, name="mix_out",
        out_shape=jax.ShapeDtypeStruct((T, D), F32),
        compiler_params=pltpu.CompilerParams(dimension_semantics=("arbitrary",),
                                             vmem_limit_bytes=V7X_VMEM_LIMIT),
    )(lidx, x, yr, ya, u, cw, cb, clnw, clnb, wo)


def _ffn_kernel(l_ref, x_ref, g_ref, wg_ref, wu_ref, wd_ref, o_ref):
    del l_ref
    o_ref[...] = _swiglu_residual(x_ref[...], g_ref[...], wg_ref, wu_ref, wd_ref)


def _ffn(lidx, x, g, wg, wu, wd, tm):
    T, D = x.shape
    row = pl.BlockSpec((tm, D), lambda i, l: (i, 0))
    grid_spec = pltpu.PrefetchScalarGridSpec(
        num_scalar_prefetch=1, grid=(T // tm,),
        in_specs=[row, _layer_spec((1, D)), _layer_spec(wg.shape[1:]), _layer_spec(wu.shape[1:]),
                  _layer_spec(wd.shape[1:])],
        out_specs=row)
    return pl.pallas_call(
        _ffn_kernel, grid_spec=grid_spec, name="ffn2",
        out_shape=jax.ShapeDtypeStruct((T, D), F32),
        compiler_params=pltpu.CompilerParams(dimension_semantics=("arbitrary",),
                                             vmem_limit_bytes=V7X_VMEM_LIMIT),
    )(lidx, x, g, wg, wu, wd)


def _rwkv_kernel(l_ref, p_ref, mu_ref, w0_ref, w2_ref, a0_ref, a2_ref, g2_ref, kk_ref, ka_ref,
                 rk_ref, lnw_ref, lnb_ref, bd_ref, tri_ref, o_ref,
                 carry_ref, state_ref):
    del l_ref
    tb = p_ref.shape[1]
    d_r = o_ref.shape[-1]

    @pl.when(pl.program_id(1) == 0)
    def _():
        carry_ref[...] = jnp.zeros_like(carry_ref)
        state_ref[...] = jnp.zeros_like(state_ref)

    p = p_ref[0]
    row = lax.broadcasted_iota(jnp.int32, (tb, 1), 0)
    prev = jnp.where(row == 0, carry_ref[...], pltpu.roll(p, 1, 0))
    carry_ref[...] = p[tb - 1:tb, :]
    ps = p + (prev - p) * mu_ref[...]

    r = ps[:, 0:d_r]
    k = ps[:, d_r:2 * d_r]
    v = ps[:, 2 * d_r:3 * d_r]
    xs = ps[:, 3 * d_r:3 * d_r + LORA_SLAB]

    bd = bd_ref[...]
    def seg_sum(t):
        tb16 = t.astype(BF16)
        return jnp.concatenate([_dot(tb16[:, s:s + PAIR], bd) for s in range(0, d_r, PAIR)], axis=1)

    lw = w0_ref[...] + _dot_f32(jnp.tanh(xs), w2_ref[...])
    ld = (-0.5 * EXP_NEG_HALF) * jnp.tanh(0.5 * lw) - 0.5 * EXP_NEG_HALF
    a = _sigmoid(a0_ref[...] + _dot_bf16(xs, a2_ref[...]))
    g = _dot_bf16(_sigmoid(xs), g2_ref[...])
    kk = k * kk_ref[...]
    kk = kk * lax.rsqrt(seg_sum(kk * kk) + KK_EPS)
    ka = ka_ref[...]
    k2 = k * ((1.0 - ka) + a * ka)
    b = kk * a

    lcum = jnp.concatenate(
        [_dot_f32_rhs(tri_ref[...], ld[s:s + CUMSUM_ROWS]) for s in range(0, tb, CUMSUM_ROWS)], axis=0)
    n_chunks = tb // CHUNK
    pc_rows = [jnp.exp(lcum[(c + 1) * CHUNK - 1:(c + 1) * CHUNK, :]) for c in range(n_chunks)]
    at_all = (-kk * jnp.exp(lcum - ld)).astype(BF16)
    rt_all = r * jnp.exp(lcum)
    inv_p = jnp.exp(-lcum)
    bt_all = (b * inv_p).astype(BF16)
    kt_all = (k2 * inv_p).astype(BF16)
    rest = inv_p * jnp.concatenate([jnp.broadcast_to(pc, (CHUNK, d_r)) for pc in pc_rows], axis=0)
    bp_all = (b * rest).astype(BF16)
    kp_all = (k2 * rest).astype(BF16)
    v_all = v.astype(BF16)

    lane = lax.broadcasted_iota(jnp.int32, (CHUNK, PAIR), 1)
    trow = lax.broadcasted_iota(jnp.int32, (CHUNK, PAIR), 0)
    head0 = lane < HEAD_DIM
    scol = lane & (HEAD_DIM - 1)
    strict = trow > scol
    incl = trow >= scol
    eye2 = jnp.where(trow == scol, 1.0, 0.0)
    r2 = lax.broadcasted_iota(jnp.int32, (PAIR, PAIR), 0)
    c2 = lax.broadcasted_iota(jnp.int32, (PAIR, PAIR), 1)
    same_head = (r2 < HEAD_DIM) == (c2 < HEAD_DIM)

    def block_diag(xb):
        zero = jnp.zeros_like(xb)
        return jnp.concatenate([jnp.where(head0, xb, zero), jnp.where(head0, zero, xb)], axis=0)

    n_pairs = d_r // PAIR
    tile = lambda t, c, p: t[c * CHUNK:(c + 1) * CHUNK, p * PAIR:(p + 1) * PAIR]
    inv_n = 1.0 / HEAD_DIM

    for c_lo in range(0, n_chunks, GROUP_CHUNKS):
        group_chunks = range(c_lo, min(c_lo + GROUP_CHUNKS, n_chunks))
        units = [(c, p) for c in group_chunks for p in range(n_pairs)]
        y = _rwkv_group(units, group_chunks, n_pairs, tile, block_diag, state_ref, pc_rows,
                        (at_all, rt_all, bt_all, kt_all, bp_all, kp_all, v_all),
                        (strict, incl, eye2, same_head))
        rows = slice(group_chunks[0] * CHUNK, (group_chunks[-1] + 1) * CHUNK)
        mean = seg_sum(y) * inv_n
        d = y - mean
        var = seg_sum(d * d) * inv_n
        yn = d * lax.rsqrt(var + RWKV_GN_EPS) * lnw_ref[...] + lnb_ref[...]
        bonus = seg_sum(r[rows] * k2[rows] * rk_ref[...]) * v[rows]
        o_ref[0, rows, :] = (yn + bonus) * g[rows]


def _rwkv_group(units, group_chunks, n_pairs, tile, block_diag, state_ref, pc_rows, scaled, masks):
    at_all, rt_all, bt_all, kt_all, bp_all, kp_all, v_all = scaled
    strict, incl, eye2, same_head = masks
    a_ab, a_ak, a_rb, a_rk = {}, {}, {}, {}
    for u in units:
        ar = jnp.concatenate([tile(at_all, *u), tile(rt_all, *u).astype(BF16)], axis=0)
        bk = jnp.concatenate([block_diag(tile(bt_all, *u)), block_diag(tile(kt_all, *u))], axis=0)
        g2 = _dot_nt(ar, bk)
        a_ab[u] = jnp.where(strict, g2[:CHUNK, :PAIR], 0.0)
        a_ak[u] = jnp.where(strict, g2[:CHUNK, PAIR:], 0.0).astype(BF16)
        a_rb[u] = jnp.where(incl, g2[CHUNK:, :PAIR], 0.0).astype(BF16)
        a_rk[u] = jnp.where(incl, g2[CHUNK:, PAIR:], 0.0).astype(BF16)
    tinv = {u: eye2 + a_ab[u] for u in units}
    apow = {}
    for u in units:
        ab = a_ab[u].astype(BF16)
        apow[u] = _dot(ab, block_diag(ab))
    for step in range(1, NEUMANN_STEPS):
        last = step == NEUMANN_STEPS - 1
        for u in units:
            ab = apow[u].astype(BF16)
            lhs = tinv[u].astype(BF16) if last else jnp.concatenate([tinv[u].astype(BF16), ab], axis=0)
            prod = _dot(lhs, block_diag(ab))
            tinv[u] = tinv[u] + prod[:CHUNK]
            if not last:
                apow[u] = prod[CHUNK:]
    av, y1v, w1, u1, rq, y1, mx, gx = {}, {}, {}, {}, {}, {}, {}, {}
    for u in units:
        avk = _dot(jnp.concatenate([a_ak[u], a_rk[u]], axis=0), block_diag(tile(v_all, *u)))
        av[u], y1v[u] = avk[:CHUNK].astype(BF16), avk[CHUNK:]
    for u in units:
        rhs = jnp.concatenate([block_diag(tile(at_all, *u)), block_diag(av[u])], axis=1)
        wu = _dot(tinv[u].astype(BF16), rhs).astype(BF16)
        w1[u], u1[u] = wu[:, :PAIR], wu[:, PAIR:]
    for u in units:
        vc = tile(v_all, *u)
        bpkp = jnp.concatenate([tile(bp_all, *u), tile(kp_all, *u)], axis=0)
        ry = _dot(a_rb[u], jnp.concatenate([block_diag(w1[u]), block_diag(u1[u])], axis=1))
        rq[u] = (tile(rt_all, *u) + ry[:, :PAIR]).astype(BF16)
        y1[u] = ry[:, PAIR:] + y1v[u]
        lhs = jnp.concatenate([jnp.concatenate([w1[u], u1[u]], axis=1),
                               jnp.concatenate([jnp.zeros_like(vc), vc], axis=1)], axis=0)
        mg = _dot_tn(lhs, bpkp)
        mx[u] = jnp.where(same_head, mg[:PAIR], 0.0).astype(BF16)
        gx[u] = jnp.where(same_head, mg[PAIR:], 0.0)

    y_rows = []
    for c in group_chunks:
        y_parts = []
        for p in range(n_pairs):
            u = (c, p)
            s0 = state_ref[p]
            s0b = s0.astype(BF16)
            y_parts.append(_dot_nt(rq[u], s0b) + y1[u])
            pc = pc_rows[c][:, p * PAIR:(p + 1) * PAIR]
            state_ref[p] = s0 * pc + _dot(s0b, mx[u]) + gx[u]
        y_rows.append(jnp.concatenate(y_parts, axis=1))
    return jnp.concatenate(y_rows, axis=0)


def _rwkv(lidx, ps3, mu, w0, w2p, a0, a2p, g2p, k_k, k_a, r_k, ln_w, ln_b, d_r, tb):
    B, S, d_shift = ps3.shape
    hid = jnp.arange(PAIR) // HEAD_DIM
    bd = (hid[:, None] == hid[None, :]).astype(BF16)
    assert tb % CUMSUM_ROWS == 0
    ci = jnp.arange(CUMSUM_ROWS)
    same = (ci[:, None] // CHUNK) == (ci[None, :] // CHUNK)
    tri = (same & (ci[:, None] >= ci[None, :])).astype(BF16)
    const = lambda a: pl.BlockSpec(a.shape, lambda b, t, l: (0,) * a.ndim)
    vec = _layer_spec((1, d_r), buffered=False)
    lora = _layer_spec((LORA_SLAB, d_r), buffered=False)
    grid_spec = pltpu.PrefetchScalarGridSpec(
        num_scalar_prefetch=1, grid=(B, S // tb),
        in_specs=[pl.BlockSpec((1, tb, d_shift), lambda b, t, l: (b, t, 0)),
                  _layer_spec((1, d_shift), buffered=False),
                  vec, lora, vec, lora, lora, vec, vec, vec, vec, vec,
                  const(bd), const(tri)],
        out_specs=pl.BlockSpec((1, tb, d_r), lambda b, t, l: (b, t, 0)),
        scratch_shapes=[pltpu.VMEM((1, d_shift), F32),
                        pltpu.VMEM((d_r // PAIR, PAIR, PAIR), F32)])
    return pl.pallas_call(
        _rwkv_kernel, grid_spec=grid_spec, name="rwkv7",
        out_shape=jax.ShapeDtypeStruct((B, S, d_r), F32),
        compiler_params=pltpu.CompilerParams(dimension_semantics=("arbitrary", "arbitrary"),
                                             vmem_limit_bytes=V7X_VMEM_LIMIT),
    )(lidx, ps3, mu, w0, w2p, a0, a2p, g2p, k_k, k_a, r_k, ln_w, ln_b, bd, tri)


def _attn_kernel(l_ref, exact_ref, q_ref, k_ref, v_ref, qn_ref, kn_ref, bd_ref, bias_ref, o_ref,
                 q_s, k_s, q4_s, k4_s, v4_s, st_s):
    S = q_ref.shape[1]
    bd = bd_ref[...]
    inv_n = 1.0 / HEAD_DIM

    def head_rms(t, gain):
        ms = _dot_bf16(t * t, bd) * inv_n
        return t * lax.rsqrt(ms + NORM_EPS) * gain

    q_s[...] = head_rms(q_ref[0], qn_ref[...]) * (HEAD_DIM ** -0.5)
    k_s[...] = head_rms(k_ref[0], kn_ref[...])
    head0 = lax.broadcasted_iota(jnp.int32, (ATTN_BLOCK, PAIR), 1) < HEAD_DIM

    s_pre = S // PRE_DILATION
    v_tok = v_ref.at[0]

    def regroup(i, carry):
        c = i // (s_pre // REGROUP_ROWS)
        r0 = (i % (s_pre // REGROUP_ROWS)) * REGROUP_ROWS
        src = pl.ds(c + PRE_DILATION * r0, REGROUP_ROWS, stride=PRE_DILATION)
        dst = pl.ds(pl.multiple_of(c * s_pre + r0, REGROUP_ROWS), REGROUP_ROWS)
        q4_s[dst, :] = q_s[src, :]
        k4_s[dst, :] = k_s[src, :]
        v4_s[dst, :] = v_tok[src, :]
        return carry

    lax.fori_loop(0, S // REGROUP_ROWS, regroup, 0)

    exact = exact_ref[l_ref[0]] == 1
    pl.when(exact)(lambda: _attn_patterns(True, S, s_pre, head0, v_tok, bias_ref, o_ref,
                                          q_s, k_s, q4_s, k4_s, v4_s, st_s))
    pl.when(jnp.logical_not(exact))(lambda: _attn_patterns(False, S, s_pre, head0, v_tok, bias_ref, o_ref,
                                                           q_s, k_s, q4_s, k4_s, v4_s, st_s))


def _attn_patterns(row_max, S, s_pre, head0, v_tok, bias_ref, o_ref, q_s, k_s, q4_s, k4_s, v4_s, st_s):
    group = ATTN_GROUP_EXACT if row_max else ATTN_GROUP
    for pi, (window, dil) in enumerate(DILATED_PATTERNS):
        n_sub = S // dil
        n_blk = n_sub // ATTN_BLOCK
        regrouped = dil % PRE_DILATION == 0
        if regrouped:
            stride = dil // PRE_DILATION
            q_src, k_src, v_src = q4_s, k4_s, v4_s
        else:
            stride = dil
            q_src, k_src, v_src = q_s, k_s, v_tok
        state = tuple(st_s.at[3 * pi + j] for j in range(3))
        span = stride * ATTN_BLOCK

        def group_body(gi, carry):
            blocks = []
            for g in range(group):
                i = gi * group + g
                n = i // dil
                res = i % dil
                if regrouped:
                    q_start = (res % PRE_DILATION) * s_pre + res // PRE_DILATION + span * n
                else:
                    q_start = res + span * n
                first = jnp.where(n == 0, 1, 0)
                k_start = q_start - span * (1 - first)
                if stride > 1:
                    q_rows = pl.ds(q_start, ATTN_BLOCK, stride=stride)
                    k_rows = pl.ds(k_start, 2 * ATTN_BLOCK, stride=stride)
                else:
                    q_rows = pl.ds(pl.multiple_of(q_start, ATTN_BLOCK), ATTN_BLOCK)
                    k_rows = pl.ds(pl.multiple_of(k_start, ATTN_BLOCK), 2 * ATTN_BLOCK)
                blocks.append(dict(first=first, q_rows=q_rows, k_rows=k_rows))
            for blk in blocks:
                q2 = q_src[blk["q_rows"], :]
                zero = jnp.zeros_like(q2)
                blk["q"] = jnp.concatenate([jnp.where(head0, q2, zero), jnp.where(head0, zero, q2)],
                                           axis=0).astype(BF16)
                blk["k"] = k_src[blk["k_rows"], :].astype(BF16)
                blk["v"] = v_src[blk["k_rows"], :].astype(BF16)
            for blk in blocks:
                blk["s"] = _dot_nt(blk["q"], blk["k"]) + bias_ref[pi, blk["first"]]
            if row_max:
                for blk in blocks:
                    blk["m"] = jnp.max(blk["s"], axis=-1, keepdims=True)
                for blk in blocks:
                    blk["e"] = jnp.exp(blk["s"] - blk["m"])
            else:
                for blk in blocks:
                    blk["e"] = jnp.exp(blk["s"])
            for blk in blocks:
                blk["l"] = jnp.sum(blk["e"], axis=-1, keepdims=True)
                blk["o"] = _dot(blk["e"].astype(BF16), blk["v"])
            for blk in blocks:
                pair = lambda t: jnp.where(head0, jnp.broadcast_to(t[:ATTN_BLOCK], (ATTN_BLOCK, PAIR)),
                                           jnp.broadcast_to(t[ATTN_BLOCK:], (ATTN_BLOCK, PAIR)))
                blk["out"] = (pair(blk["o"]), pair(blk["m"]) if row_max else None, pair(blk["l"]))
            for blk in blocks:
                for ref, val in zip(state, blk["out"]):
                    if val is not None:
                        ref[blk["q_rows"], :] = val
            return carry

        lax.fori_loop(0, dil * n_blk // group, group_body, 0)

    def finish(i, carry):
        c = i // (s_pre // ATTN_BLOCK)
        r0 = (i % (s_pre // ATTN_BLOCK)) * ATTN_BLOCK
        tok = pl.ds(c + PRE_DILATION * r0, ATTN_BLOCK, stride=PRE_DILATION)
        grp = pl.ds(pl.multiple_of(c * s_pre + r0, ATTN_BLOCK), ATTN_BLOCK)
        total = None
        for pi, (_, dil) in enumerate(DILATED_PATTERNS):
            rows = grp if dil % PRE_DILATION == 0 else tok
            if row_max:
                part = tuple(st_s[3 * pi + j, rows, :] for j in range(3))
                total = part if total is None else _softmax_merge(total, part)
            else:
                part = (st_s[3 * pi, rows, :], None, st_s[3 * pi + 2, rows, :])
                total = part if total is None else (total[0] + part[0], None, total[2] + part[2])
        o_ref[0, tok, :] = total[0] / total[2]
        return carry

    lax.fori_loop(0, S // ATTN_BLOCK, finish, 0)


def _softmax_merge(a, b):
    acc_a, m_a, l_a = a
    acc_b, m_b, l_b = b
    m_new = jnp.maximum(m_a, m_b)
    w_a = jnp.exp(m_a - m_new)
    w_b = jnp.exp(m_b - m_new)
    return acc_a * w_a + acc_b * w_b, m_new, l_a * w_a + l_b * w_b


def _alibi_slopes(n):
    def pow2(m):
        start = 2.0 ** (-8.0 / m)
        return [start ** (i + 1) for i in range(m)]
    if math.log2(n).is_integer():
        return pow2(n)
    c = 2 ** int(math.floor(math.log2(n)))
    return pow2(c) + pow2(2 * c)[0::2][: n - c]


def _attn_bias(n_heads):
    qi = jnp.arange(ATTN_BLOCK)
    ki = jnp.arange(2 * ATTN_BLOCK)
    dist = qi[:, None] + ATTN_BLOCK - ki[None, :]
    slopes = jnp.asarray(_alibi_slopes(n_heads), F32)
    out = []
    for window, dil in DILATED_PATTERNS:
        valid = (dist >= 0) & (dist <= window // dil)
        bias = -slopes[:, None, None] * (dist * dil).astype(F32)[None]
        rest = jnp.where(valid[None], bias, MASK_VALUE)
        first = jnp.concatenate([rest[..., ATTN_BLOCK:], jnp.full_like(rest[..., ATTN_BLOCK:], MASK_VALUE)], -1)
        out.append(jnp.stack([rest, first], 0).reshape(2, n_heads * ATTN_BLOCK, 2 * ATTN_BLOCK))
    return jnp.stack(out, 0)


def _attn_tables(score_bound, n_heads):
    bias = _attn_bias(n_heads)[None] - score_bound[:, None, None, None, None]
    return bias, (score_bound > SAFE_SOFTMAX_SHIFT).astype(jnp.int32)


def _attention(lidx, qkv3, q_norm, k_norm, bias, exact, d_a):
    B, S, _ = qkv3.shape
    n_heads = d_a // HEAD_DIM
    n_pairs = n_heads // 2
    slab = 2 * HEAD_DIM
    hid = jnp.arange(slab) // HEAD_DIM
    bd = (hid[:, None] == hid[None, :]).astype(BF16)
    col = lambda off: pl.BlockSpec((1, S, slab), lambda b, p, l, e: (b, 0, off + p))
    gain = pl.BlockSpec((None, 1, slab), lambda b, p, l, e: (l[0], 0, 0))
    grid_spec = pltpu.PrefetchScalarGridSpec(
        num_scalar_prefetch=2, grid=(B, n_pairs),
        in_specs=[col(0), col(n_pairs), col(2 * n_pairs), gain, gain,
                  pl.BlockSpec(bd.shape, lambda b, p, l, e: (0, 0)),
                  pl.BlockSpec((None, len(DILATED_PATTERNS), 2, 2 * ATTN_BLOCK, 2 * ATTN_BLOCK),
                               lambda b, p, l, e: (l[0], 0, 0, p, 0))],
        out_specs=pl.BlockSpec((1, S, slab), lambda b, p, l, e: (b, 0, p)),
        scratch_shapes=[pltpu.VMEM((S, slab), F32)] * 5
        + [pltpu.VMEM((3 * len(DILATED_PATTERNS), S, slab), F32)])
    return pl.pallas_call(
        _attn_kernel, grid_spec=grid_spec, name="dilated_attn",
        out_shape=jax.ShapeDtypeStruct((B, S, d_a), F32),
        compiler_params=pltpu.CompilerParams(dimension_semantics=("arbitrary", "arbitrary"),
                                             vmem_limit_bytes=V7X_VMEM_LIMIT),
    )(lidx, exact, qkv3, qkv3, qkv3, q_norm, k_norm, bd, bias)


def _pad_lora(w, offset):
    L, r, d = w.shape
    return jnp.zeros((L, LORA_SLAB, d), F32).at[:, offset:offset + r, :].set(w)


def kernel(x, norm_ffn1, ffn1_w_gate, ffn1_w_up, ffn1_w_down, norm_mix, w_in, shift_mu, rwkv_w0, rwkv_w2, rwkv_a0, rwkv_a2, rwkv_g2, rwkv_k_k, rwkv_k_a, rwkv_r_k, rwkv_ln_w, rwkv_ln_b, attn_q_norm, attn_k_norm, conv_dw_w, conv_dw_b, conv_ln_w, conv_ln_b, w_out, norm_ffn2, ffn2_w_gate, ffn2_w_up, ffn2_w_down):
    B, S, D = x.shape
    depth = w_in.shape[0]
    d_r = rwkv_w0.shape[-1]
    d_c = conv_dw_b.shape[-1]
    d_shift = shift_mu.shape[-1]
    d_a = (w_in.shape[-1] - d_shift - 2 * d_c) // 3
    r_w, r_a, r_g = rwkv_w2.shape[1], rwkv_a2.shape[1], rwkv_g2.shape[1]
    assert d_shift == 3 * d_r + LORA_SLAB and r_w + r_a + r_g == LORA_SLAB
    assert S % (2 * ATTN_BLOCK * DILATED_PATTERNS[-1][1]) == 0 and (d_a // HEAD_DIM) % 2 == 0
    assert (S // ATTN_BLOCK) % max(ATTN_GROUP, ATTN_GROUP_EXACT) == 0
    assert S % (PRE_DILATION * REGROUP_ROWS) == 0
    assert all(d < PRE_DILATION or d % PRE_DILATION == 0 for _, d in DILATED_PATTERNS)
    T = B * S
    assert T % FFN1_ROWS == 0 and T % FFN2_ROWS == 0 and S % RWKV_ROWS == 0 and S % MIX_ROWS == 0

    vec = lambda a: a.reshape(depth, 1, -1)
    bf = lambda a: a.astype(BF16)
    wg1, wu1, wd1 = bf(ffn1_w_gate), bf(ffn1_w_up), bf(ffn1_w_down)
    wg2, wu2, wd2 = bf(ffn2_w_gate), bf(ffn2_w_up), bf(ffn2_w_down)
    win, wo = bf(w_in), bf(w_out)
    g1, gm, g2 = vec(norm_ffn1), vec(norm_mix), vec(norm_ffn2)
    w2p = _pad_lora(rwkv_w2, 0)
    a2p = _pad_lora(rwkv_a2, r_w)
    g2p = _pad_lora(rwkv_g2, r_w + r_a)
    qn = vec(jnp.tile(attn_q_norm, (1, 2)))
    kn = vec(jnp.tile(attn_k_norm, (1, 2)))
    score_bound = (HEAD_DIM ** 0.5) * jnp.max(jnp.abs(attn_q_norm), -1) * jnp.max(jnp.abs(attn_k_norm), -1)
    attn_bias, attn_exact = lax.optimization_barrier(_attn_tables(score_bound, d_a // HEAD_DIM))

    layer_index = lambda l: jnp.reshape(l, (1,)).astype(jnp.int32)

    def project(l, xf):
        return _ffn_proj(layer_index(l), xf, g1, wg1, wu1, wd1, gm, win, (d_shift, 3 * d_a, 2 * d_c),
                         FFN1_ROWS)

    def mix(l, projected):
        lidx = layer_index(l)
        x1, ps, qkv, u = projected
        y_r = _rwkv(lidx, ps.reshape(B, S, d_shift), vec(shift_mu), vec(rwkv_w0), w2p, vec(rwkv_a0),
                    a2p, g2p, vec(rwkv_k_k), vec(rwkv_k_a), vec(rwkv_r_k), vec(rwkv_ln_w),
                    vec(rwkv_ln_b), d_r, RWKV_ROWS)
        y_a = _attention(lidx, qkv.reshape(B, S, 3 * d_a), qn, kn, attn_bias, attn_exact, d_a)
        x2 = _mix_out(lidx, x1, y_r.reshape(T, d_r), y_a.reshape(T, d_a), u, conv_dw_w,
                      vec(conv_dw_b), vec(conv_ln_w), vec(conv_ln_b), wo, MIX_ROWS, S)
        return _ffn(lidx, x2, g2, wg2, wu2, wd2, FFN2_ROWS)

    state = project(jnp.int32(0), x.reshape(T, D))
    state = lax.fori_loop(0, depth - 1, lambda l, st: tuple(project(l + 1, mix(l, st))), tuple(state))
    return mix(jnp.int32(depth - 1), state).reshape(B, S, D)
```

```python
import functools
import math

import jax
import jax.numpy as jnp
from jax import lax
from jax.experimental import pallas as pl
from jax.experimental.pallas import tpu as pltpu

F32 = jnp.float32
BF16 = jnp.bfloat16

LANES = 128
SUBLANES = 8
FFN1_ROWS = 512
FFN2_ROWS = 1024
MIX_ROWS = 1024
RWKV_ROWS = 1024
HEAD_DIM = 64
PAIR = 2 * HEAD_DIM
NORM_EPS = 1e-6
RWKV_GN_EPS = 64e-5
CONV_LN_EPS = 1e-5
KK_EPS = 1e-12
CONV_WIDTH = 31
DILATED_PATTERNS = ((128, 1), (512, 4), (2048, 16))
ATTN_BLOCK = 128
ATTN_GROUP = 32
ATTN_GROUP_EXACT = 4
PRE_DILATION = 4
REGROUP_ROWS = 256
LORA_SLAB = 128
CHUNK = 64
GROUP_CHUNKS = 8
CUMSUM_ROWS = 256
NEUMANN_STEPS = 6
MASK_VALUE = -1e30
SAFE_SOFTMAX_SHIFT = 40.0
EXP_NEG_HALF = math.exp(-0.5)
V7X_VMEM_LIMIT = 56 * 1024 * 1024

NT_DIMS = (((1,), (1,)), ((), ()))
TN_DIMS = (((0,), (0,)), ((), ()))


def _dot(a, b):
    return jnp.dot(a, b, preferred_element_type=F32)


def _dot_nt(a, b):
    return lax.dot_general(a, b, NT_DIMS, preferred_element_type=F32)


def _dot_tn(a, b):
    return lax.dot_general(a, b, TN_DIMS, preferred_element_type=F32)


def _split2(x):
    hi = x.astype(BF16)
    lo = (x - hi.astype(F32)).astype(BF16)
    return hi, lo


def _dot_bf16(x, w):
    return _dot(x.astype(BF16), w.astype(BF16))


def _dot_f32_rhs(w_exact, x):
    hi, lo = _split2(x)
    return _dot(w_exact, hi) + _dot(w_exact, lo)


def _dot_f32(x, w):
    xh, xl = _split2(x)
    wh, wl = _split2(w)
    return _dot(xh, wh) + _dot(xl, wh) + _dot(xh, wl)


def _sigmoid(x):
    return 0.5 + 0.5 * jnp.tanh(0.5 * x)


def _rms_norm(x, g):
    return x * lax.rsqrt(jnp.mean(x * x, axis=-1, keepdims=True) + NORM_EPS) * g


def _swiglu_residual(x, g, wg_ref, wu_ref, wd_ref):
    xn = _rms_norm(x, g).astype(BF16)
    gate = _dot(xn, wg_ref[...])
    up = _dot(xn, wu_ref[...])
    h =(gate * _sigmoid(gate) * up).astype(BF16)
    return x + 0.5 * _dot(h, wd_ref[...])


def _ffn_proj_kernel(l_ref, x_ref, g1_ref, wg_ref, wu_ref, wd_ref, gm_ref, win_ref,
                     x1_ref, ps_ref, qkv_ref, u_ref):
    del l_ref
    x1 = _swiglu_residual(x_ref[...], g1_ref[...], wg_ref, wu_ref, wd_ref)
    x1_ref[...] = x1
    h = _rms_norm(x1, gm_ref[...]).astype(BF16)
    proj = _dot(h, win_ref[...])
    d_shift = ps_ref.shape[-1]
    d_qkv = qkv_ref.shape[-1]
    ps_ref[...] = proj[:, :d_shift]
    qkv_ref[...] = proj[:, d_shift:d_shift + d_qkv]
    u_ref[...] = proj[:, d_shift + d_qkv:]


def _layer_spec(shape, buffered=True):
    nd = len(shape)
    kw = dict(pipeline_mode=pl.Buffered(1)) if buffered else {}
    return pl.BlockSpec((None,) + tuple(shape), lambda *a: (a[-1][0],) + (0,) * nd, **kw)


def _ffn_proj(lidx, x, g1, wg, wu, wd, gm, win, dims, tm):
    T, D = x.shape
    d_shift, d_qkv, d_u = dims
    row = lambda w: pl.BlockSpec((tm, w), lambda i, l: (i, 0))
    grid_spec = pltpu.PrefetchScalarGridSpec(
        num_scalar_prefetch=1, grid=(T // tm,),
        in_specs=[row(D), _layer_spec((1, D)), _layer_spec(wg.shape[1:]), _layer_spec(wu.shape[1:]),
                  _layer_spec(wd.shape[1:]), _layer_spec((1, D)), _layer_spec(win.shape[1:])],
        out_specs=[row(D), row(d_shift), row(d_qkv), row(d_u)])
    return pl.pallas_call(
        _ffn_proj_kernel, grid_spec=grid_spec, name="ffn_proj",
        out_shape=[jax.ShapeDtypeStruct((T, D), F32), jax.ShapeDtypeStruct((T, d_shift), F32),
                   jax.ShapeDtypeStruct((T, d_qkv), F32), jax.ShapeDtypeStruct((T, d_u), F32)],
        compiler_params=pltpu.CompilerParams(dimension_semantics=("arbitrary",),
                                             vmem_limit_bytes=V7X_VMEM_LIMIT),
    )(lidx, x, g1, wg, wu, wd, gm, win)


CONV_PAD = 32
CONV_ROWS = 64


def _conv_rows(z_ref, r0, w_ref, b_ref, lnw_ref, lnb_ref):
    d_c = z_ref.shape[-1]
    shift = CONV_PAD - (CONV_WIDTH - 1)
    n_win = CONV_ROWS + CONV_PAD
    win = z_ref[r0:r0 + n_win, :]
    acc = jnp.zeros((CONV_ROWS, d_c), F32) + b_ref[...]
    for sub in range(SUBLANES):
        rolled = win if sub == 0 else pltpu.roll(win, n_win - sub, 0)
        for j in range(CONV_WIDTH):
            off = shift + j
            if off % SUBLANES == sub:
                base = off - sub
                acc = acc + rolled[base:base + CONV_ROWS, :] * w_ref[j:j + 1, :]
    mean = jnp.mean(acc, axis=-1, keepdims=True)
    d = acc - mean
    var = jnp.mean(d * d, axis=-1, keepdims=True)
    z = d * lax.rsqrt(var + CONV_LN_EPS) * lnw_ref[...] + lnb_ref[...]
    return z * _sigmoid(z)


def _mix_out_kernel(tiles_per_seq, l_ref, x_ref, yr_ref, ya_ref, u_ref, cw_ref, cb_ref, clnw_ref,
                    clnb_ref, wo_ref, o_ref, z_s):
    del l_ref
    i = pl.program_id(0)
    tm = x_ref.shape[0]
    d_c = z_s.shape[-1]

    @pl.when(i == 0)
    def _():
        z_s[...] = jnp.zeros_like(z_s)

    left = z_s[tm:tm + CONV_PAD, :]
    z_s[0:CONV_PAD, :] = jnp.where(i % tiles_per_seq == 0, jnp.zeros_like(left), left)
    z_s[CONV_PAD:, :] = u_ref[:, 0:d_c] * _sigmoid(u_ref[:, d_c:])
    y_c = jnp.concatenate([_conv_rows(z_s, r0, cw_ref, cb_ref, clnw_ref, clnb_ref)
                           for r0 in range(0, tm, CONV_ROWS)], axis=0)
    mix = jnp.concatenate([yr_ref[...].astype(BF16), ya_ref[...].astype(BF16), y_c.astype(BF16)], axis=1)
    o_ref[...] = x_ref[...] + _dot(mix, wo_ref[...])


def _mix_out(lidx, x, yr, ya, u, cw, cb, clnw, clnb, wo, tm, seq_len):
    T, D = x.shape
    d_c = u.shape[1] // 2
    assert seq_len % tm == 0 and tm % CONV_ROWS == 0
    row = lambda w: pl.BlockSpec((tm, w), lambda i, l: (i, 0))
    grid_spec = pltpu.PrefetchScalarGridSpec(
        num_scalar_prefetch=1, grid=(T // tm,),
        in_specs=[row(D), row(yr.shape[1]), row(ya.shape[1]), row(2 * d_c),
                  _layer_spec((CONV_WIDTH, d_c), buffered=False), _layer_spec((1, d_c), buffered=False),
                  _layer_spec((1, d_c), buffered=False), _layer_spec((1, d_c), buffered=False),
                  _layer_spec(wo.shape[1:])],
        out_specs=row(D),
        scratch_shapes=[pltpu.VMEM((CONV_PAD + tm, d_c), F32)])
    return pl.pallas_call(
        functools.partial(_mix_out_kernel, seq_len // tm), grid_spec=grid_spec, name="mix_out",
        out_shape=jax.ShapeDtypeStruct((T, D), F32),
        compiler_params=pltpu.CompilerParams(dimension_semantics=("arbitrary",),
                                             vmem_limit_bytes=V7X_VMEM_LIMIT),
    )(lidx, x, yr, ya, u, cw, cb, clnw, clnb, wo)


def _ffn_kernel(l_ref, x_ref, g_ref, wg_ref, wu_ref, wd_ref, o_ref):
    del l_ref
    o_ref[...] = _swiglu_residual(x_ref[...], g_ref[...], wg_ref, wu_ref, wd_ref)


def _ffn(lidx, x, g, wg, wu, wd, tm):
    T, D = x.shape
    row = pl.BlockSpec((tm, D), lambda i, l: (i, 0))
    grid_spec = pltpu.PrefetchScalarGridSpec(
        num_scalar_prefetch=1, grid=(T // tm,),
        in_specs=[row, _layer_spec((1, D)), _layer_spec(wg.shape[1:]), _layer_spec(wu.shape[1:]),
                  _layer_spec(wd.shape[1:])],
        out_specs=row)
    return pl.pallas_call(
        _ffn_kernel, grid_spec=grid_spec, name="ffn2",
        out_shape=jax.ShapeDtypeStruct((T, D), F32),
        compiler_params=pltpu.CompilerParams(dimension_semantics=("arbitrary",),
                                             vmem_limit_bytes=V7X_VMEM_LIMIT),
    )(lidx, x, g, wg, wu, wd)


def _rwkv_kernel(l_ref, p_ref, mu_ref, w0_ref, w2_ref, a0_ref, a2_ref, g2_ref, kk_ref, ka_ref,
                 rk_ref, lnw_ref, lnb_ref, bd_ref, tri_ref, o_ref,
                 carry_ref, state_ref):
    del l_ref
    tb = p_ref.shape[1]
    d_r = o_ref.shape[-1]

    @pl.when(pl.program_id(1) == 0)
    def _():
        carry_ref[...] = jnp.zeros_like(carry_ref)
        state_ref[...] = jnp.zeros_like(state_ref)

    p = p_ref[0]
    row = lax.broadcasted_iota(jnp.int32, (tb, 1), 0)
    prev = jnp.where(row == 0, carry_ref[...], pltpu.roll(p, 1, 0))
    carry_ref[...] = p[tb - 1:tb, :]
    ps = p + (prev - p) * mu_ref[...]

    r = ps[:, 0:d_r]
    k = ps[:, d_r:2 * d_r]
    v = ps[:, 2 * d_r:3 * d_r]
    xs = ps[:, 3 * d_r:3 * d_r + LORA_SLAB]

    bd = bd_ref[...]
    def seg_sum(t):
        tb16 = t.astype(BF16)
        return jnp.concatenate([_dot(tb16[:, s:s + PAIR], bd) for s in range(0, d_r, PAIR)], axis=1)

    lw = w0_ref[...] + _dot_f32(jnp.tanh(xs), w2_ref[...])
    ld = (-0.5 * EXP_NEG_HALF) * jnp.tanh(0.5 * lw) - 0.5 * EXP_NEG_HALF
    a = _sigmoid(a0_ref[...] + _dot_bf16(xs, a2_ref[...]))
    g = _dot_bf16(_sigmoid(xs), g2_ref[...])
    kk = k * kk_ref[...]
    kk = kk * lax.rsqrt(seg_sum(kk * kk) + KK_EPS)
    ka = ka_ref[...]
    k2 = k * ((1.0 - ka) + a * ka)
    b = kk * a

    lcum = jnp.concatenate(
        [_dot_f32_rhs(tri_ref[...], ld[s:s + CUMSUM_ROWS]) for s in range(0, tb, CUMSUM_ROWS)], axis=0)
    n_chunks = tb // CHUNK
    pc_rows = [jnp.exp(lcum[(c + 1) * CHUNK - 1:(c + 1) * CHUNK, :]) for c in range(n_chunks)]
    at_all = (-kk * jnp.exp(lcum - ld)).astype(BF16)
    rt_all = r * jnp.exp(lcum)
    inv_p = jnp.exp(-lcum)
    bt_all = (b * inv_p).astype(BF16)
    kt_all = (k2 * inv_p).astype(BF16)
    rest = inv_p * jnp.concatenate([jnp.broadcast_to(pc, (CHUNK, d_r)) for pc in pc_rows], axis=0)
    bp_all = (b * rest).astype(BF16)
    kp_all = (k2 * rest).astype(BF16)
    v_all = v.astype(BF16)

    lane = lax.broadcasted_iota(jnp.int32, (CHUNK, PAIR), 1)
    trow = lax.broadcasted_iota(jnp.int32, (CHUNK, PAIR), 0)
    head0 = lane < HEAD_DIM
    scol = lane & (HEAD_DIM - 1)
    strict = trow > scol
    incl = trow >= scol
    eye2 = jnp.where(trow == scol, 1.0, 0.0)
    r2 = lax.broadcasted_iota(jnp.int32, (PAIR, PAIR), 0)
    c2 = lax.broadcasted_iota(jnp.int32, (PAIR, PAIR), 1)
    same_head = (r2 < HEAD_DIM) == (c2 < HEAD_DIM)

    def block_diag(xb):
        zero = jnp.zeros_like(xb)
        return jnp.concatenate([jnp.where(head0, xb, zero), jnp.where(head0, zero, xb)], axis=0)

    n_pairs = d_r // PAIR
    tile = lambda t, c, p: t[c * CHUNK:(c + 1) * CHUNK, p * PAIR:(p + 1) * PAIR]
    inv_n = 1.0 / HEAD_DIM

    for c_lo in range(0, n_chunks, GROUP_CHUNKS):
        group_chunks = range(c_lo, min(c_lo + GROUP_CHUNKS, n_chunks))
        units = [(c, p) for c in group_chunks for p in range(n_pairs)]
        y = _rwkv_group(units, group_chunks, n_pairs, tile, block_diag, state_ref, pc_rows,
                        (at_all, rt_all, bt_all, kt_all, bp_all, kp_all, v_all),
                        (strict, incl, eye2, same_head))
        rows = slice(group_chunks[0] * CHUNK, (group_chunks[-1] + 1) * CHUNK)
        mean = seg_sum(y) * inv_n
        d = y - mean
        var = seg_sum(d * d) * inv_n
        yn = d * lax.rsqrt(var + RWKV_GN_EPS) * lnw_ref[...] + lnb_ref[...]
        bonus = seg_sum(r[rows] * k2[rows] * rk_ref[...]) * v[rows]
        o_ref[0, rows, :] = (yn + bonus) * g[rows]


def _rwkv_group(units, group_chunks, n_pairs, tile, block_diag, state_ref, pc_rows, scaled, masks):
    at_all, rt_all, bt_all, kt_all, bp_all, kp_all, v_all = scaled
    strict, incl, eye2, same_head = masks
    a_ab, a_ak, a_rb, a_rk = {}, {}, {}, {}
    for u in units:
        ar = jnp.concatenate([tile(at_all, *u), tile(rt_all, *u).astype(BF16)], axis=0)
        bk = jnp.concatenate([block_diag(tile(bt_all, *u)), block_diag(tile(kt_all, *u))], axis=0)
        g2 = _dot_nt(ar, bk)
        a_ab[u] = jnp.where(strict, g2[:CHUNK, :PAIR], 0.0)
        a_ak[u] = jnp.where(strict, g2[:CHUNK, PAIR:], 0.0).astype(BF16)
        a_rb[u] = jnp.where(incl, g2[CHUNK:, :PAIR], 0.0).astype(BF16)
        a_rk[u] = jnp.where(incl, g2[CHUNK:, PAIR:], 0.0).astype(BF16)
    tinv = {u: eye2 + a_ab[u] for u in units}
    apow = {}
    for u in units:
        ab = a_ab[u].astype(BF16)
        apow[u] = _dot(ab, block_diag(ab))
    for step in range(1, NEUMANN_STEPS):
        last = step == NEUMANN_STEPS - 1
        for u in units:
            ab = apow[u].astype(BF16)
            lhs = tinv[u].astype(BF16) if last else jnp.concatenate([tinv[u].astype(BF16), ab], axis=0)
            prod = _dot(lhs, block_diag(ab))
            tinv[u] = tinv[u] + prod[:CHUNK]
            if not last:
                apow[u] = prod[CHUNK:]
    av, y1v, w1, u1, rq, y1, mx, gx = {}, {}, {}, {}, {}, {}, {}, {}
    for u in units:
        avk = _dot(jnp.concatenate([a_ak[u], a_rk[u]], axis=0), block_diag(tile(v_all, *u)))
        av[u], y1v[u] = avk[:CHUNK].astype(BF16), avk[CHUNK:]
    for u in units:
        rhs = jnp.concatenate([block_diag(tile(at_all, *u)), block_diag(av[u])], axis=1)
        wu = _dot(tinv[u].astype(BF16), rhs).astype(BF16)
        w1[u], u1[u] = wu[:, :PAIR], wu[:, PAIR:]
    for u in units:
        vc = tile(v_all, *u)
        bpkp = jnp.concatenate([tile(bp_all, *u), tile(kp_all, *u)], axis=0)
        ry = _dot(a_rb[u], jnp.concatenate([block_diag(w1[u]), block_diag(u1[u])], axis=1))
        rq[u] = (tile(rt_all, *u) + ry[:, :PAIR]).astype(BF16)
        y1[u] = ry[:, PAIR:] + y1v[u]
        lhs = jnp.concatenate([jnp.concatenate([w1[u], u1[u]], axis=1),
                               jnp.concatenate([jnp.zeros_like(vc), vc], axis=1)], axis=0)
        mg = _dot_tn(lhs, bpkp)
        mx[u] = jnp.where(same_head, mg[:PAIR], 0.0).astype(BF16)
        gx[u] = jnp.where(same_head, mg[PAIR:], 0.0)

    y_rows = []
    for c in group_chunks:
        y_parts = []
        for p in range(n_pairs):
            u = (c, p)
            s0 = state_ref[p]
            s0b = s0.astype(BF16)
            y_parts.append(_dot_nt(rq[u], s0b) + y1[u])
            pc = pc_rows[c][:, p * PAIR:(p + 1) * PAIR]
            state_ref[p] = s0 * pc + _dot(s0b, mx[u]) + gx[u]
        y_rows.append(jnp.concatenate(y_parts, axis=1))
    return jnp.concatenate(y_rows, axis=0)


def _rwkv(lidx, ps3, mu, w0, w2p, a0, a2p, g2p, k_k, k_a, r_k, ln_w, ln_b, d_r, tb):
    B, S, d_shift = ps3.shape
    hid = jnp.arange(PAIR) // HEAD_DIM
    bd = (hid[:, None] == hid[None, :]).astype(BF16)
    assert tb % CUMSUM_ROWS == 0
    ci = jnp.arange(CUMSUM_ROWS)
    same = (ci[:, None] // CHUNK) == (ci[None, :] // CHUNK)
    tri = (same & (ci[:, None] >= ci[None, :])).astype(BF16)
    const = lambda a: pl.BlockSpec(a.shape, lambda b, t, l: (0,) * a.ndim)
    vec = _layer_spec((1, d_r), buffered=False)
    lora = _layer_spec((LORA_SLAB, d_r), buffered=False)
    grid_spec = pltpu.PrefetchScalarGridSpec(
        num_scalar_prefetch=1, grid=(B, S // tb),
        in_specs=[pl.BlockSpec((1, tb, d_shift), lambda b, t, l: (b, t, 0)),
                  _layer_spec((1, d_shift), buffered=False),
                  vec, lora, vec, lora, lora, vec, vec, vec, vec, vec,
                  const(bd), const(tri)],
        out_specs=pl.BlockSpec((1, tb, d_r), lambda b, t, l: (b, t, 0)),
        scratch_shapes=[pltpu.VMEM((1, d_shift), F32),
                        pltpu.VMEM((d_r // PAIR, PAIR, PAIR), F32)])
    return pl.pallas_call(
        _rwkv_kernel, grid_spec=grid_spec, name="rwkv7",
        out_shape=jax.ShapeDtypeStruct((B, S, d_r), F32),
        compiler_params=pltpu.CompilerParams(dimension_semantics=("arbitrary", "arbitrary"),
                                             vmem_limit_bytes=V7X_VMEM_LIMIT),
    )(lidx, ps3, mu, w0, w2p, a0, a2p, g2p, k_k, k_a, r_k, ln_w, ln_b, bd, tri)


def _attn_kernel(l_ref, exact_ref, q_ref, k_ref, v_ref, qn_ref, kn_ref, bd_ref, bias_ref, o_ref,
                 q_s, k_s, q4_s, k4_s, v4_s, st_s):
    S = q_ref.shape[1]
    bd = bd_ref[...]
    inv_n = 1.0 / HEAD_DIM

    def head_rms(t, gain):
        ms = _dot_bf16(t * t, bd) * inv_n
        return t * lax.rsqrt(ms + NORM_EPS) * gain

    q_s[...] = head_rms(q_ref[0], qn_ref[...]) * (HEAD_DIM ** -0.5)
    k_s[...] = head_rms(k_ref[0], kn_ref[...])
    head0 = lax.broadcasted_iota(jnp.int32, (ATTN_BLOCK, PAIR), 1) < HEAD_DIM

    s_pre = S // PRE_DILATION
    v_tok = v_ref.at[0]

    def regroup(i, carry):
        c = i // (s_pre // REGROUP_ROWS)
        r0 = (i % (s_pre // REGROUP_ROWS)) * REGROUP_ROWS
        src = pl.ds(c + PRE_DILATION * r0, REGROUP_ROWS, stride=PRE_DILATION)
        dst = pl.ds(pl.multiple_of(c * s_pre + r0, REGROUP_ROWS), REGROUP_ROWS)
        q4_s[dst, :] = q_s[src, :]
        k4_s[dst, :] = k_s[src, :]
        v4_s[dst, :] = v_tok[src, :]
        return carry

    lax.fori_loop(0, S // REGROUP_ROWS, regroup, 0)

    exact = exact_ref[l_ref[0]] == 1
    pl.when(exact)(lambda: _attn_patterns(True, S, s_pre, head0, v_tok, bias_ref, o_ref,
                                          q_s, k_s, q4_s, k4_s, v4_s, st_s))
    pl.when(jnp.logical_not(exact))(lambda: _attn_patterns(False, S, s_pre, head0, v_tok, bias_ref, o_ref,
                                                           q_s, k_s, q4_s, k4_s, v4_s, st_s))


def _attn_patterns(row_max, S, s_pre, head0, v_tok, bias_ref, o_ref, q_s, k_s, q4_s, k4_s, v4_s, st_s):
    group = ATTN_GROUP_EXACT if row_max else ATTN_GROUP
    for pi, (window, dil) in enumerate(DILATED_PATTERNS):
        n_sub = S // dil
        n_blk = n_sub // ATTN_BLOCK
        regrouped = dil % PRE_DILATION == 0
        if regrouped:
            stride = dil // PRE_DILATION
            q_src, k_src, v_src = q4_s, k4_s, v4_s
        else:
            stride = dil
            q_src, k_src, v_src = q_s, k_s, v_tok
        state = tuple(st_s.at[3 * pi + j] for j in range(3))
        span = stride * ATTN_BLOCK

        def group_body(gi, carry):
            blocks = []
            for g in range(group):
                i = gi * group + g
                n = i // dil
                res = i % dil
                if regrouped:
                    q_start = (res % PRE_DILATION) * s_pre + res // PRE_DILATION + span * n
                else:
                    q_start = res + span * n
                first = jnp.where(n == 0, 1, 0)
                k_start = q_start - span * (1 - first)
                if stride > 1:
                    q_rows = pl.ds(q_start, ATTN_BLOCK, stride=stride)
                    k_rows = pl.ds(k_start, 2 * ATTN_BLOCK, stride=stride)
                else:
                    q_rows = pl.ds(pl.multiple_of(q_start, ATTN_BLOCK), ATTN_BLOCK)
                    k_rows = pl.ds(pl.multiple_of(k_start, ATTN_BLOCK), 2 * ATTN_BLOCK)
                blocks.append(dict(first=first, q_rows=q_rows, k_rows=k_rows))
            for blk in blocks:
                q2 = q_src[blk["q_rows"], :]
                zero = jnp.zeros_like(q2)
                blk["q"] = jnp.concatenate([jnp.where(head0, q2, zero), jnp.where(head0, zero, q2)],
                                           axis=0).astype(BF16)
                blk["k"] = k_src[blk["k_rows"], :].astype(BF16)
                blk["v"] = v_src[blk["k_rows"], :].astype(BF16)
            for blk in blocks:
                blk["s"] = _dot_nt(blk["q"], blk["k"]) + bias_ref[pi, blk["first"]]
            if row_max:
                for blk in blocks:
                    blk["m"] = jnp.max(blk["s"], axis=-1, keepdims=True)
                for blk in blocks:
                    blk["e"] = jnp.exp(blk["s"] - blk["m"])
            else:
                for blk in blocks:
                    blk["e"] = jnp.exp(blk["s"])
            for blk in blocks:
                blk["l"] = jnp.sum(blk["e"], axis=-1, keepdims=True)
                blk["o"] = _dot(blk["e"].astype(BF16), blk["v"])
            for blk in blocks:
                pair = lambda t: jnp.where(head0, jnp.broadcast_to(t[:ATTN_BLOCK], (ATTN_BLOCK, PAIR)),
                                           jnp.broadcast_to(t[ATTN_BLOCK:], (ATTN_BLOCK, PAIR)))
                blk["out"] = (pair(blk["o"]), pair(blk["m"]) if row_max else None, pair(blk["l"]))
            for blk in blocks:
                for ref, val in zip(state, blk["out"]):
                    if val is not None:
                        ref[blk["q_rows"], :] = val
            return carry

        lax.fori_loop(0, dil * n_blk // group, group_body, 0)

    def finish(i, carry):
        c = i // (s_pre // ATTN_BLOCK)
        r0 = (i % (s_pre // ATTN_BLOCK)) * ATTN_BLOCK
        tok = pl.ds(c + PRE_DILATION * r0, ATTN_BLOCK, stride=PRE_DILATION)
        start = c * s_pre + r0
        grp = pl.ds(start if isinstance(start, int) else pl.multiple_of(start, ATTN_BLOCK), ATTN_BLOCK)
        total = None
        for pi, (_, dil) in enumerate(DILATED_PATTERNS):
            rows = grp if dil % PRE_DILATION == 0 else tok
            if row_max:
                part = tuple(st_s[3 * pi + j, rows, :] for j in range(3))
                total = part if total is None else _softmax_merge(total, part)
            else:
                part = (st_s[3 * pi, rows, :], None, st_s[3 * pi + 2, rows, :])
                total = part if total is None else (total[0] + part[0], None, total[2] + part[2])
        o_ref[0, tok, :] = total[0] / total[2]
        return carry

    if row_max:
        lax.fori_loop(0, S // ATTN_BLOCK, finish, 0)
    else:
        for i in range(S // ATTN_BLOCK):
            finish(i, 0)


def _softmax_merge(a, b):
    acc_a, m_a, l_a = a
    acc_b, m_b, l_b = b
    m_new = jnp.maximum(m_a, m_b)
    w_a = jnp.exp(m_a - m_new)
    w_b = jnp.exp(m_b - m_new)
    return acc_a * w_a + acc_b * w_b, m_new, l_a * w_a + l_b * w_b


def _alibi_slopes(n):
    def pow2(m):
        start = 2.0 ** (-8.0 / m)
        return [start ** (i + 1) for i in range(m)]
    if math.log2(n).is_integer():
        return pow2(n)
    c = 2 ** int(math.floor(math.log2(n)))
    return pow2(c) + pow2(2 * c)[0::2][: n - c]


def _attn_bias(n_heads):
    qi = jnp.arange(ATTN_BLOCK)
    ki = jnp.arange(2 * ATTN_BLOCK)
    dist = qi[:, None] + ATTN_BLOCK - ki[None, :]
    slopes = jnp.asarray(_alibi_slopes(n_heads), F32)
    out = []
    for window, dil in DILATED_PATTERNS:
        valid = (dist >= 0) & (dist <= window // dil)
        bias = -slopes[:, None, None] * (dist * dil).astype(F32)[None]
        rest = jnp.where(valid[None], bias, MASK_VALUE)
        first = jnp.concatenate([rest[..., ATTN_BLOCK:], jnp.full_like(rest[..., ATTN_BLOCK:], MASK_VALUE)], -1)
        out.append(jnp.stack([rest, first], 0).reshape(2, n_heads * ATTN_BLOCK, 2 * ATTN_BLOCK))
    return jnp.stack(out, 0)


def _attn_tables(score_bound, n_heads):
    bias = _attn_bias(n_heads)[None] - score_bound[:, None, None, None, None]
    return bias, (score_bound > SAFE_SOFTMAX_SHIFT).astype(jnp.int32)


def _attention(lidx, qkv3, q_norm, k_norm, bias, exact, d_a):
    B, S, _ = qkv3.shape
    n_heads = d_a // HEAD_DIM
    n_pairs = n_heads // 2
    slab = 2 * HEAD_DIM
    hid = jnp.arange(slab) // HEAD_DIM
    bd = (hid[:, None] == hid[None, :]).astype(BF16)
    col = lambda off: pl.BlockSpec((1, S, slab), lambda b, p, l, e: (b, 0, off + p))
    gain = pl.BlockSpec((None, 1, slab), lambda b, p, l, e: (l[0], 0, 0))
    grid_spec = pltpu.PrefetchScalarGridSpec(
        num_scalar_prefetch=2, grid=(B, n_pairs),
        in_specs=[col(0), col(n_pairs), col(2 * n_pairs), gain, gain,
                  pl.BlockSpec(bd.shape, lambda b, p, l, e: (0, 0)),
                  pl.BlockSpec((None, len(DILATED_PATTERNS), 2, 2 * ATTN_BLOCK, 2 * ATTN_BLOCK),
                               lambda b, p, l, e: (l[0], 0, 0, p, 0))],
        out_specs=pl.BlockSpec((1, S, slab), lambda b, p, l, e: (b, 0, p)),
        scratch_shapes=[pltpu.VMEM((S, slab), F32)] * 5
        + [pltpu.VMEM((3 * len(DILATED_PATTERNS), S, slab), F32)])
    return pl.pallas_call(
        _attn_kernel, grid_spec=grid_spec, name="dilated_attn",
        out_shape=jax.ShapeDtypeStruct((B, S, d_a), F32),
        compiler_params=pltpu.CompilerParams(dimension_semantics=("arbitrary", "arbitrary"),
                                             vmem_limit_bytes=V7X_VMEM_LIMIT),
    )(lidx, exact, qkv3, qkv3, qkv3, q_norm, k_norm, bd, bias)


def _pad_lora(w, offset):
    L, r, d = w.shape
    return jnp.zeros((L, LORA_SLAB, d), F32).at[:, offset:offset + r, :].set(w)


def kernel(x, norm_ffn1, ffn1_w_gate, ffn1_w_up, ffn1_w_down, norm_mix, w_in, shift_mu, rwkv_w0, rwkv_w2, rwkv_a0, rwkv_a2, rwkv_g2, rwkv_k_k, rwkv_k_a, rwkv_r_k, rwkv_ln_w, rwkv_ln_b, attn_q_norm, attn_k_norm, conv_dw_w, conv_dw_b, conv_ln_w, conv_ln_b, w_out, norm_ffn2, ffn2_w_gate, ffn2_w_up, ffn2_w_down):
    B, S, D = x.shape
    depth = w_in.shape[0]
    d_r = rwkv_w0.shape[-1]
    d_c = conv_dw_b.shape[-1]
    d_shift = shift_mu.shape[-1]
    d_a = (w_in.shape[-1] - d_shift - 2 * d_c) // 3
    r_w, r_a, r_g = rwkv_w2.shape[1], rwkv_a2.shape[1], rwkv_g2.shape[1]
    assert d_shift == 3 * d_r + LORA_SLAB and r_w + r_a + r_g == LORA_SLAB
    assert S % (2 * ATTN_BLOCK * DILATED_PATTERNS[-1][1]) == 0 and (d_a // HEAD_DIM) % 2 == 0
    assert (S // ATTN_BLOCK) % max(ATTN_GROUP, ATTN_GROUP_EXACT) == 0
    assert S % (PRE_DILATION * REGROUP_ROWS) == 0
    assert all(d < PRE_DILATION or d % PRE_DILATION == 0 for _, d in DILATED_PATTERNS)
    T = B * S
    assert T % FFN1_ROWS == 0 and T % FFN2_ROWS == 0 and S % RWKV_ROWS == 0 and S % MIX_ROWS == 0

    vec = lambda a: a.reshape(depth, 1, -1)
    bf = lambda a: a.astype(BF16)
    wg1, wu1, wd1 = bf(ffn1_w_gate), bf(ffn1_w_up), bf(ffn1_w_down)
    wg2, wu2, wd2 = bf(ffn2_w_gate), bf(ffn2_w_up), bf(ffn2_w_down)
    win, wo = bf(w_in), bf(w_out)
    g1, gm, g2 = vec(norm_ffn1), vec(norm_mix), vec(norm_ffn2)
    w2p = _pad_lora(rwkv_w2, 0)
    a2p = _pad_lora(rwkv_a2, r_w)
    g2p = _pad_lora(rwkv_g2, r_w + r_a)
    qn = vec(jnp.tile(attn_q_norm, (1, 2)))
    kn = vec(jnp.tile(attn_k_norm, (1, 2)))
    score_bound = (HEAD_DIM ** 0.5) * jnp.max(jnp.abs(attn_q_norm), -1) * jnp.max(jnp.abs(attn_k_norm), -1)
    attn_bias, attn_exact = lax.optimization_barrier(_attn_tables(score_bound, d_a // HEAD_DIM))

    layer_index = lambda l: jnp.reshape(l, (1,)).astype(jnp.int32)

    def project(l, xf):
        return _ffn_proj(layer_index(l), xf, g1, wg1, wu1, wd1, gm, win, (d_shift, 3 * d_a, 2 * d_c),
                         FFN1_ROWS)

    def mix(l, projected):
        lidx = layer_index(l)
        x1, ps, qkv, u = projected
        y_r = _rwkv(lidx, ps.reshape(B, S, d_shift), vec(shift_mu), vec(rwkv_w0), w2p, vec(rwkv_a0),
                    a2p, g2p, vec(rwkv_k_k), vec(rwkv_k_a), vec(rwkv_r_k), vec(rwkv_ln_w),
                    vec(rwkv_ln_b), d_r, RWKV_ROWS)
        y_a = _attention(lidx, qkv.reshape(B, S, 3 * d_a), qn, kn, attn_bias, attn_exact, d_a)
        x2 = _mix_out(lidx, x1, y_r.reshape(T, d_r), y_a.reshape(T, d_a), u, conv_dw_w,
                      vec(conv_dw_b), vec(conv_ln_w), vec(conv_ln_b), wo, MIX_ROWS, S)
        return _ffn(lidx, x2, g2, wg2, wu2, wd2, FFN2_ROWS)

    state = project(jnp.int32(0), x.reshape(T, D))
    state = lax.fori_loop(0, depth - 1, lambda l, st: tuple(project(l + 1, mix(l, st))), tuple(state))
    return mix(jnp.int32(depth - 1), state).reshape(B, S, D)
```
